```python
import jax, jax.numpy as jnp
from jax import lax
import numpy as np

D_MODEL = 1024
BATCH = 1
SEQ = 16384
DEPTH = 1

HEAD_DIM = 64
RWKV_HEADS = 8
RWKV_WIDTH = RWKV_HEADS * HEAD_DIM
DECAY_LORA = 64
AAA_LORA = 64
GATE_LORA = 128
RWKV_PROJ_WIDTH = 3 * RWKV_WIDTH + DECAY_LORA + AAA_LORA + GATE_LORA
GN_EPS = 64e-5
MOBA_HEADS = 8
MOBA_WIDTH = MOBA_HEADS * HEAD_DIM
MOBA_BLOCK = 256
MOBA_TOP_K = 3
MOBA_Q_CHUNK = 64
IN_WIDTH = RWKV_PROJ_WIDTH + 3 * MOBA_WIDTH + 2 * D_MODEL
N_EXPERTS = 32
EXPERT_TOP_K = 4
D_FF = D_MODEL
SWIGLU_LIMIT = 7.0
SWIGLU_ALPHA = 1.702
RMS_EPS = 1e-5
GATE_NEG = -1e30

kernel_name = "hybrid_rwkv7_moba_moe_adaln_block"


def rms_norm(x, g):
    xf = x.astype(jnp.float32)
    y = xf * lax.rsqrt(jnp.mean(xf * xf, axis=-1, keepdims=True) + RMS_EPS)
    return (y * g.astype(jnp.float32)).astype(x.dtype)


def rwkv7_mixer(p, mu, w0, w2, a0, a2, g2, k_k, k_a, r_k, ln_w, ln_b):
    B, T, _ = p.shape
    H, N = RWKV_HEADS, HEAD_DIM
    p_prev = jnp.pad(p, ((0, 0), (1, 0), (0, 0)))[:, :-1]
    ps = p + (p_prev - p) * mu
    cuts = [RWKV_WIDTH, 2 * RWKV_WIDTH, 3 * RWKV_WIDTH,
            3 * RWKV_WIDTH + DECAY_LORA, 3 * RWKV_WIDTH + DECAY_LORA + AAA_LORA]
    r, k, v, xw, xa, xg = jnp.split(ps, cuts, axis=-1)
    w = -jax.nn.softplus(-(w0 + jnp.tanh(xw) @ w2)) - 0.5
    a = jax.nn.sigmoid(a0 + xa @ a2)
    g = jax.nn.sigmoid(xg) @ g2
    f32 = jnp.float32
    kk = (k * k_k).astype(f32).reshape(B, T, H, N)
    kk = kk / jnp.maximum(jnp.sqrt(jnp.sum(kk * kk, axis=-1, keepdims=True)), 1e-12)
    k = k * (1 + (a - 1) * k_a)
    decay = jnp.exp(-jnp.exp(w.astype(f32))).reshape(B, T, H, N)
    r_h = r.astype(f32).reshape(B, T, H, N)
    k_h = k.astype(f32).reshape(B, T, H, N)
    v_h = v.astype(f32).reshape(B, T, H, N)
    a_h = a.astype(f32).reshape(B, T, H, N)
    vec_a = -kk
    vec_b = kk * a_h

    def step(S, inp):
        r_t, w_t, k_t, v_t, a_t, b_t = inp
        sa = jnp.einsum('bhij,bhj->bhi', S, a_t)
        S = S * w_t[:, :, None, :] + sa[..., None] * b_t[:, :, None, :] + v_t[..., None] * k_t[:, :, None, :]
        return S, jnp.einsum('bhij,bhj->bhi', S, r_t)

    xs = tuple(jnp.moveaxis(t, 1, 0) for t in (r_h, decay, k_h, v_h, vec_a, vec_b))
    _, y = lax.scan(step, jnp.zeros((B, H, N, N), f32), xs)
    y = jnp.moveaxis(y, 0, 1)
    mean = jnp.mean(y, axis=-1, keepdims=True)
    var = jnp.mean(jnp.square(y - mean), axis=-1, keepdims=True)
    y = ((y - mean) * lax.rsqrt(var + GN_EPS)).reshape(B, T, H * N)
    y = y * ln_w.astype(f32) + ln_b.astype(f32)
    bonus = jnp.sum(r_h * k_h * r_k.astype(f32), axis=-1, keepdims=True) * v_h
    y = y + bonus.reshape(B, T, H * N)
    return (y * g.astype(f32)).astype(p.dtype)


def moba_attention(q, k, v):
    B, T, H, Dh = q.shape
    BS, QC = MOBA_BLOCK, MOBA_Q_CHUNK
    NB = -(-T // BS)
    Tp = NB * BS
    pad = ((0, 0), (0, Tp - T), (0, 0), (0, 0))
    q, k, v = jnp.pad(q, pad), jnp.pad(k, pad), jnp.pad(v, pad)
    scale = Dh ** -0.5
    k_blk = k.reshape(B, NB, BS, H, Dh).transpose(0, 3, 1, 2, 4)
    v_blk = v.reshape(B, NB, BS, H, Dh).transpose(0, 3, 1, 2, 4)
    n_sel = min(MOBA_TOP_K, NB - 1)
    q_blk_id = jnp.arange(Tp) // BS
    if n_sel > 0:
        k_mean = jnp.mean(k_blk[:, :, :NB - 1].astype(jnp.float32), axis=3)
        gate = jnp.einsum('bthd,bhnd->bhtn', q.astype(jnp.float32), k_mean)
        past = jnp.arange(NB - 1)[None, :] < q_blk_id[:, None]
        gate = jnp.where(past, gate, GATE_NEG)
        _, sel_idx = lax.top_k(gate, n_sel)
    NC = Tp // QC
    q_c = q.reshape(B, NC, QC, H, Dh).transpose(1, 0, 3, 2, 4)
    b_ix = jnp.arange(B)[:, None, None, None]
    h_ix = jnp.arange(H)[None, :, None, None]
    key_off = jnp.arange(BS)

    def chunk(args):
        qc, ci = args
        start = ci * QC
        blk = start // BS
        q_off = start - blk * BS + jnp.arange(QC)
        k_own = lax.dynamic_index_in_dim(k_blk, blk, axis=2, keepdims=False)
        v_own = lax.dynamic_index_in_dim(v_blk, blk, axis=2, keepdims=False)
        s_own = jnp.einsum('bhqd,bhkd->bhqk', qc, k_own).astype(jnp.float32) * scale
        s_own = jnp.where(key_off[None, :] <= q_off[:, None], s_own, -jnp.inf)
        if n_sel == 0:
            p = jax.nn.softmax(s_own, axis=-1).astype(v.dtype)
            return jnp.einsum('bhqk,bhkd->bhqd', p, v_own)
        ic = lax.dynamic_slice_in_dim(sel_idx, start, QC, axis=2)
        k_sel = k_blk[b_ix, h_ix, ic]
        v_sel = v_blk[b_ix, h_ix, ic]
        s_sel = jnp.einsum('bhqd,bhqnkd->bhqnk', qc, k_sel).astype(jnp.float32) * scale
        s_sel = jnp.where((ic < blk)[..., None], s_sel, -jnp.inf)
        s = jnp.concatenate([s_sel.reshape(B, H, QC, n_sel * BS), s_own], axis=-1)
        p = jax.nn.softmax(s, axis=-1).astype(v.dtype)
        p_sel = p[..., :n_sel * BS].reshape(B, H, QC, n_sel, BS)
        p_own = p[..., n_sel * BS:]
        return (jnp.einsum('bhqnk,bhqnkd->bhqd', p_sel, v_sel)
                + jnp.einsum('bhqk,bhkd->bhqd', p_own, v_own))

    out = lax.map(chunk, (q_c, jnp.arange(NC)))
    out = out.transpose(1, 0, 3, 2, 4).reshape(B, Tp, H, Dh)
    return out[:, :T]


def moe_ffn(h, router_w, router_b, w_gate, b_gate, w_up, b_up, w_down, b_down):
    logits = (h @ router_w + router_b).astype(jnp.float32)
    top_v, top_i = lax.top_k(logits, EXPERT_TOP_K)
    top_w = jax.nn.softmax(top_v, axis=-1)
    comb = jnp.sum(jax.nn.one_hot(top_i, N_EXPERTS, dtype=jnp.float32) * top_w[..., None], axis=-2)
    comb = comb.astype(h.dtype)
    y = jnp.zeros_like(h)
    for e in range(N_EXPERTS):
        gt = jnp.minimum(h @ w_gate[e] + b_gate[e], SWIGLU_LIMIT)
        up = jnp.clip(h @ w_up[e] + b_up[e], -SWIGLU_LIMIT, SWIGLU_LIMIT)
        hid = (up + 1) * gt * jax.nn.sigmoid(SWIGLU_ALPHA * gt)
        y = y + comb[..., e:e + 1] * (hid @ w_down[e] + b_down[e])
    return y


def setup_inputs(seed: int = 0) -> dict:
    key = jax.random.key(seed)
    ks = jax.random.split(key, 32)
    D, F, E = D_MODEL, D_FF, N_EXPERTS
    nrm = lambda k, shape, s: jax.random.normal(k, shape, jnp.float32) * s
    return {
        "x": nrm(ks[0], (BATCH, SEQ, D), 1.0),
        "c": nrm(ks[1], (BATCH, D), 1.0),
        "w_ada": nrm(ks[2], (D, 6 * D), D ** -0.5),
        "b_ada": nrm(ks[3], (6 * D,), 0.02),
        "norm1_g": 1.0 + nrm(ks[4], (D,), 0.02),
        "w_in": nrm(ks[5], (D, IN_WIDTH), D ** -0.5),
        "rwkv_mu": jax.random.uniform(ks[6], (RWKV_PROJ_WIDTH,), jnp.float32),
        "rwkv_w0": jax.random.uniform(ks[7], (RWKV_WIDTH,), jnp.float32, -6.0, -1.0),
        "rwkv_w2": nrm(ks[8], (DECAY_LORA, RWKV_WIDTH), 0.1 * DECAY_LORA ** -0.5),
        "rwkv_a0": nrm(ks[9], (RWKV_WIDTH,), 0.5),
        "rwkv_a2": nrm(ks[10], (AAA_LORA, RWKV_WIDTH), 0.1 * AAA_LORA ** -0.5),
        "rwkv_g2": nrm(ks[11], (GATE_LORA, RWKV_WIDTH), GATE_LORA ** -0.5),
        "rwkv_k_k": 0.85 + nrm(ks[12], (RWKV_WIDTH,), 0.02),
        "rwkv_k_a": 1.0 + nrm(ks[13], (RWKV_WIDTH,), 0.02),
        "rwkv_r_k": nrm(ks[14], (RWKV_HEADS, HEAD_DIM), 0.1),
        "rwkv_ln_w": 1.0 + nrm(ks[15], (RWKV_WIDTH,), 0.02),
        "rwkv_ln_b": nrm(ks[16], (RWKV_WIDTH,), 0.02),
        "w_branch_a": nrm(ks[17], (RWKV_WIDTH, D), RWKV_WIDTH ** -0.5),
        "w_branch_b": nrm(ks[18], (MOBA_WIDTH, D), MOBA_WIDTH ** -0.5),
        "w_out": nrm(ks[19], (D, D), D ** -0.5),
        "norm2_g": 1.0 + nrm(ks[20], (D,), 0.02),
        "router_w": nrm(ks[21], (D, E), D ** -0.5),
        "router_b": nrm(ks[22], (E,), 0.01),
        "exp_w_gate": nrm(ks[23], (E, D, F), D ** -0.5),
        "exp_b_gate": nrm(ks[24], (E, F), 0.02),
        "exp_w_up": nrm(ks[25], (E, D, F), D ** -0.5),
        "exp_b_up": nrm(ks[26], (E, F), 0.02),
        "exp_w_down": nrm(ks[27], (E, F, D), F ** -0.5),
        "exp_b_down": nrm(ks[28], (E, D), 0.02),
        "normf_g": 1.0 + nrm(ks[29], (D,), 0.02),
    }


def reference(x, c, w_ada, b_ada, norm1_g, w_in, rwkv_mu, rwkv_w0, rwkv_w2, rwkv_a0, rwkv_a2,
              rwkv_g2, rwkv_k_k, rwkv_k_a, rwkv_r_k, rwkv_ln_w, rwkv_ln_b, w_branch_a, w_branch_b,
              w_out, norm2_g, router_w, router_b, exp_w_gate, exp_b_gate, exp_w_up, exp_b_up,
              exp_w_down, exp_b_down, normf_g):
    B, T, D = x.shape
    mod = (jax.nn.silu(c) @ w_ada + b_ada)[:, None, :]
    shift1, scale1, gate1, shift2, scale2, gate2 = jnp.split(mod, 6, axis=-1)
    cuts = [RWKV_PROJ_WIDTH, RWKV_PROJ_WIDTH + MOBA_WIDTH, RWKV_PROJ_WIDTH + 2 * MOBA_WIDTH,
            RWKV_PROJ_WIDTH + 3 * MOBA_WIDTH, RWKV_PROJ_WIDTH + 3 * MOBA_WIDTH + D]
    for _ in range(DEPTH):
        h = rms_norm(x, norm1_g) * (1 + scale1) + shift1
        proj = h @ w_in
        p_rwkv, q, k, v, g_a, g_b = jnp.split(proj, cuts, axis=-1)
        y_a = rwkv7_mixer(p_rwkv, rwkv_mu, rwkv_w0, rwkv_w2, rwkv_a0, rwkv_a2, rwkv_g2,
                          rwkv_k_k, rwkv_k_a, rwkv_r_k, rwkv_ln_w, rwkv_ln_b) @ w_branch_a
        heads = lambda t: t.reshape(B, T, MOBA_HEADS, HEAD_DIM)
        y_b = moba_attention(heads(q), heads(k), heads(v)).reshape(B, T, MOBA_WIDTH) @ w_branch_b
        mixed = (jax.nn.sigmoid(g_a) * y_a + jax.nn.sigmoid(g_b) * y_b) @ w_out
        x = x + gate1 * mixed
        h2 = rms_norm(x, norm2_g) * (1 + scale2) + shift2
        x = x + gate2 * moe_ffn(h2, router_w, router_b, exp_w_gate, exp_b_gate, exp_w_up,
                                exp_b_up, exp_w_down, exp_b_down)
    return rms_norm(x, normf_g)
```

```python
import functools

import jax
import jax.numpy as jnp
from jax import lax
from jax.experimental import pallas as pl
from jax.experimental.pallas import tpu as pltpu

F32 = jnp.float32
BF16 = jnp.bfloat16
HI = lax.Precision.HIGHEST

HEAD_DIM = 64
N_HEADS = 8
WIDTH = N_HEADS * HEAD_DIM
DECAY_LORA = 64
AAA_LORA = 64
GATE_LORA = 128
RWKV_PROJ = 3 * WIDTH + DECAY_LORA + AAA_LORA + GATE_LORA
GN_EPS = 64e-5
MOBA_BLOCK = 256
MOBA_TOP_K = 3
N_EXPERTS = 32
EXPERT_TOP_K = 4
SWIGLU_LIMIT = 7.0
SWIGLU_ALPHA = 1.702
RMS_EPS = 1e-5
GATE_NEG = -1e30
MASK_NEG = -1e30

SCAN_CHUNK = 64
EXPERT_TILE = 256
VMEM_LIMIT = 56 * 1024 * 1024


def _cparams(*sem):
    return pltpu.CompilerParams(dimension_semantics=sem, vmem_limit_bytes=VMEM_LIMIT)


def _sigmoid(x):
    return 1.0 / (1.0 + jnp.exp(-x))


def _mod_kernel(c_ref, w_ref, b_ref, o_ref):
    c = c_ref[...]
    s = c * _sigmoid(c)
    o_ref[...] = jnp.dot(s, w_ref[...], precision=HI, preferred_element_type=F32) + b_ref[...]


def _mod(c, w_ada, b_ada):
    d = c.shape[-1]
    n = w_ada.shape[1]
    c8 = jnp.broadcast_to(c[:1], (8, d))
    tn = 1024
    out = pl.pallas_call(
        _mod_kernel,
        grid=(n // tn,),
        in_specs=[pl.BlockSpec((8, d), lambda j: (0, 0)),
                  pl.BlockSpec((d, tn), lambda j: (0, j)),
                  pl.BlockSpec((1, tn), lambda j: (0, j))],
        out_specs=pl.BlockSpec((8, tn), lambda j: (0, j)),
        out_shape=jax.ShapeDtypeStruct((8, n), F32),
        compiler_params=_cparams("arbitrary"),
        name="mod",
    )(c8, w_ada, b_ada.reshape(1, n))
    return out[:1]


def _inproj_kernel(x_ref, g_ref, sc_ref, sh_ref, wr_ref, wq_ref, wg_ref, pr_ref, qkv_ref, sg_ref):
    x = x_ref[...]
    ms = jnp.mean(x * x, axis=-1, keepdims=True)
    h = x * lax.rsqrt(ms + RMS_EPS) * g_ref[...]
    h = h * (1.0 + sc_ref[...]) + sh_ref[...]
    hb = h.astype(BF16)
    pr_ref[...] = jnp.dot(hb, wr_ref[...], preferred_element_type=F32)
    qkv_ref[...] = jnp.dot(hb, wq_ref[...], preferred_element_type=F32)
    sg_ref[...] = _sigmoid(jnp.dot(hb, wg_ref[...], preferred_element_type=F32))


def _inproj(x2, norm_g, scale, shift, w_r, w_q, w_g):
    t, d = x2.shape
    tm = 256
    full = lambda a: pl.BlockSpec(a.shape, lambda i: (0, 0))
    row = lambda n: pl.BlockSpec((tm, n), lambda i: (i, 0))
    return pl.pallas_call(
        _inproj_kernel,
        grid=(t // tm,),
        in_specs=[row(d), full(norm_g), full(scale), full(shift), full(w_r), full(w_q), full(w_g)],
        out_specs=[row(w_r.shape[1]), row(w_q.shape[1]), row(w_g.shape[1])],
        out_shape=[jax.ShapeDtypeStruct((t, w_r.shape[1]), F32),
                   jax.ShapeDtypeStruct((t, w_q.shape[1]), F32),
                   jax.ShapeDtypeStruct((t, w_g.shape[1]), F32)],
        compiler_params=_cparams("arbitrary"),
        name="inproj",
    )(x2, norm_g, scale, shift, w_r, w_q, w_g)


def _rwkvprep_kernel(p_ref, mu_ref, w0_ref, w2_ref, a0_ref, a2_ref, g2_ref, kk_ref, ka_ref, bd_ref,
                     r_out, lw_out, k_out, v_out, a_out, b_out, g_out, prev_ref):
    i = pl.program_id(0)

    @pl.when(i == 0)
    def _():
        prev_ref[...] = jnp.zeros_like(prev_ref)

    p = p_ref[...]
    tm = p.shape[0]
    rolled = pltpu.roll(p, 1, 0)
    row0 = lax.broadcasted_iota(jnp.int32, p.shape, 0) == 0
    p_prev = jnp.where(row0, prev_ref[0:1, :], rolled)
    prev_ref[0:1, :] = p[tm - 1:tm, :]
    ps = p + (p_prev - p) * mu_ref[...]
    r = ps[:, 0:WIDTH]
    k = ps[:, WIDTH:2 * WIDTH]
    v = ps[:, 2 * WIDTH:3 * WIDTH]
    xwa = ps[:, 3 * WIDTH:3 * WIDTH + DECAY_LORA + AAA_LORA]
    xg = ps[:, 3 * WIDTH + DECAY_LORA + AAA_LORA:RWKV_PROJ]
    dot = functools.partial(jnp.dot, precision=HI, preferred_element_type=F32)
    z = w0_ref[...] + dot(jnp.tanh(xwa), w2_ref[...])
    nz = -z
    softplus = jnp.maximum(nz, 0.0) + jnp.log(1.0 + jnp.exp(-jnp.abs(nz)))
    w = -softplus - 0.5
    a = _sigmoid(a0_ref[...] + dot(xwa, a2_ref[...]))
    g = dot(_sigmoid(xg), g2_ref[...])
    kk = k * kk_ref[...]
    ss = dot(kk * kk, bd_ref[...])
    kk = kk / jnp.maximum(jnp.sqrt(ss), 1e-12)
    r_out[...] = r
    lw_out[...] = -jnp.exp(w)
    k_out[...] = k * (1.0 + (a - 1.0) * ka_ref[...])
    v_out[...] = v
    a_out[...] = -kk
    b_out[...] = kk * a
    g_out[...] = g


def _rwkvprep(p_rwkv, mu, w0, w2p, a0, a2p, g2, k_k, k_a, bd):
    t = p_rwkv.shape[0]
    tm = 256
    full = lambda a: pl.BlockSpec(a.shape, lambda i: (0, 0))
    row = lambda n: pl.BlockSpec((tm, n), lambda i: (i, 0))
    outs = [jax.ShapeDtypeStruct((t, WIDTH), F32)] * 7
    return pl.pallas_call(
        _rwkvprep_kernel,
        grid=(t // tm,),
        in_specs=[row(RWKV_PROJ), full(mu), full(w0), full(w2p), full(a0), full(a2p), full(g2),
                  full(k_k), full(k_a), full(bd)],
        out_specs=[row(WIDTH)] * 7,
        out_shape=outs,
        scratch_shapes=[pltpu.VMEM((8, RWKV_PROJ), F32)],
        compiler_params=_cparams("arbitrary"),
        name="rwkvprep",
    )(p_rwkv, mu, w0, w2p, a0, a2p, g2, k_k, k_a, bd)


def _bmm(a, b):
    return jnp.einsum("hmk,hkn->hmn", a, b, precision=HI, preferred_element_type=F32)


def _bmm_nt(a, b):
    return jnp.einsum("hmk,hnk->hmn", a, b, precision=HI, preferred_element_type=F32)


def _bmm_tn(a, b):
    return jnp.einsum("hkm,hkn->hmn", a, b, precision=HI, preferred_element_type=F32)


def _rwkvscan_kernel(r_ref, lw_ref, k_ref, v_ref, a_ref, b_ref, g_ref, lnw_ref, lnb_ref, rk_ref,
                     y_ref, s_ref):
    i = pl.program_id(0)

    @pl.when(i == 0)
    def _():
        s_ref[...] = jnp.zeros_like(s_ref)

    c = SCAN_CHUNK
    r = r_ref[...]
    lw = lw_ref[...]
    k = k_ref[...]
    v = v_ref[...]
    a = a_ref[...]
    b = b_ref[...]
    s0 = s_ref[...]
    nh = r.shape[0]

    row = lax.broadcasted_iota(jnp.int32, (nh, c, c), 1)
    col = lax.broadcasted_iota(jnp.int32, (nh, c, c), 2)
    lower = row >= col
    strict = row > col
    cum = _bmm(lower.astype(F32), lw)
    tot = cum[:, c - 1:c, :]
    g_in = jnp.exp(cum)
    g_ex = jnp.exp(cum - lw)
    g_inv = jnp.exp(-cum)
    g_rem = jnp.exp(tot - cum)
    a_t = a * g_ex
    r_t = r * g_in
    b_t = b * g_inv
    k_t = k * g_inv
    l_ab = jnp.where(strict, _bmm_nt(a_t, b_t), 0.0)
    l_ak = jnp.where(strict, _bmm_nt(a_t, k_t), 0.0)
    m_rb = jnp.where(lower, _bmm_nt(r_t, b_t), 0.0)
    m_rk = jnp.where(lower, _bmm_nt(r_t, k_t), 0.0)
    inv = jnp.where(row == col, 1.0, 0.0) + l_ab
    lp = l_ab
    for _ in range(5):
        lp = _bmm(lp, lp)
        inv = inv + _bmm(inv, lp)
    x = _bmm_nt(a_t, s0) + _bmm(l_ak, v)
    u = _bmm(inv, x)
    y = _bmm_nt(r_t, s0) + _bmm(m_rb, u) + _bmm(m_rk, v)
    s_ref[...] = s0 * jnp.exp(tot) + _bmm_tn(u, b * g_rem) + _bmm_tn(v, k * g_rem)

    mean = jnp.mean(y, axis=-1, keepdims=True)
    yc = y - mean
    var = jnp.mean(yc * yc, axis=-1, keepdims=True)
    yn = yc * lax.rsqrt(var + GN_EPS) * lnw_ref[...] + lnb_ref[...]
    bonus = jnp.sum(r * k * rk_ref[...], axis=-1, keepdims=True) * v
    y_ref[...] = (yn + bonus) * g_ref[...]


def _rwkvscan(r, lw, k, v, a, b, g, ln_w, ln_b, r_k):
    nh, t, n = r.shape
    c = SCAN_CHUNK
    blk = pl.BlockSpec((nh, c, n), lambda i: (0, i, 0))
    par = pl.BlockSpec((nh, 1, n), lambda i: (0, 0, 0))
    return pl.pallas_call(
        _rwkvscan_kernel,
        grid=(t // c,),
        in_specs=[blk] * 7 + [par] * 3,
        out_specs=blk,
        out_shape=jax.ShapeDtypeStruct((nh, t, n), F32),
        scratch_shapes=[pltpu.VMEM((nh, n, n), F32)],
        compiler_params=_cparams("arbitrary"),
        name="rwkvscan",
    )(r, lw, k, v, a, b, g, ln_w, ln_b, r_k)


def _kmean_kernel(k_ref, o_ref):
    o_ref[0, 0] = jnp.mean(k_ref[0], axis=0, keepdims=True)


def _kmean(k_hm):
    nh, t, n = k_hm.shape
    nb = t // MOBA_BLOCK
    out = pl.pallas_call(
        _kmean_kernel,
        grid=(nh, nb),
        in_specs=[pl.BlockSpec((1, MOBA_BLOCK, n), lambda h, j: (h, j, 0))],
        out_specs=pl.BlockSpec((1, 1, 1, n), lambda h, j: (h, j, 0, 0)),
        out_shape=jax.ShapeDtypeStruct((nh, nb, 1, n), F32),
        compiler_params=_cparams("arbitrary", "arbitrary"),
        name="kmean",
    )(k_hm)
    return out.reshape(nh, nb, n)


def _mobasel_kernel(q_ref, km_ref, o_ref):
    j = pl.program_id(1)
    q = q_ref[0]
    km = km_ref[0]
    nb = km.shape[0]
    tq = q.shape[0]
    gate = lax.dot_general(km, q, (((1,), (1,)), ((), ())), precision=HI, preferred_element_type=F32)
    n_id = lax.broadcasted_iota(jnp.int32, (nb, tq), 0)
    t_id = lax.broadcasted_iota(jnp.int32, (nb, tq), 1) + j * tq
    q_blk = t_id // MOBA_BLOCK
    cand = n_id < nb - 1
    past = n_id < q_blk
    neg_inf = jnp.float32(-jnp.inf)
    gate = jnp.where(cand, jnp.where(past, gate, GATE_NEG), neg_inf)
    sel = jnp.zeros((nb, tq), F32)
    for _ in range(min(MOBA_TOP_K, nb - 1)):
        m = jnp.max(gate, axis=0, keepdims=True)
        idx = jnp.min(jnp.where(gate == m, n_id, nb), axis=0, keepdims=True)
        pick = n_id == idx
        sel = jnp.where(pick, 1.0, sel)
        gate = jnp.where(pick, neg_inf, gate)
    o_ref[0] = jnp.where(past, sel, 0.0)


def _mobasel(q_hm, kmean):
    nh, t, n = q_hm.shape
    nb = kmean.shape[1]
    tq = min(t, 2048)
    return pl.pallas_call(
        _mobasel_kernel,
        grid=(nh, t // tq),
        in_specs=[pl.BlockSpec((1, tq, n), lambda h, j: (h, j, 0)),
                  pl.BlockSpec((1, nb, n), lambda h, j: (h, 0, 0))],
        out_specs=pl.BlockSpec((1, nb, tq), lambda h, j: (h, 0, j)),
        out_shape=jax.ShapeDtypeStruct((nh, nb, t), F32),
        compiler_params=_cparams("arbitrary", "arbitrary"),
        name="mobasel",
    )(q_hm, kmean)


def _mobaattn_kernel(qt_ref, k_ref, vt_ref, sel_ref, o_ref):
    i = pl.program_id(1)
    bs = MOBA_BLOCK
    qt = qt_ref[0]

    def step(j, carry, mask):
        m, l, acc = carry
        s = jnp.dot(k_ref[0, j], qt, preferred_element_type=F32)
        s = jnp.where(mask, s, MASK_NEG)
        m_new = jnp.maximum(m, jnp.max(s, axis=0, keepdims=True))
        alpha = jnp.exp(m - m_new)
        p = jnp.exp(s - m_new)
        l = alpha * l + jnp.sum(p, axis=0, keepdims=True)
        acc = alpha * acc + jnp.dot(vt_ref[0, j], p.astype(BF16), preferred_element_type=F32)
        return m_new, l, acc

    key_id = lax.broadcasted_iota(jnp.int32, (bs, bs), 0)
    q_id = lax.broadcasted_iota(jnp.int32, (bs, bs), 1)
    init = (jnp.full((1, bs), MASK_NEG, F32), jnp.zeros((1, bs), F32), jnp.zeros((HEAD_DIM, bs), F32))
    carry = step(i, init, key_id <= q_id)

    def body(j, carry):
        return step(j, carry, sel_ref[0, pl.ds(j, 1), :] > 0.5)

    m, l, acc = lax.fori_loop(0, i, body, carry)
    o_ref[0] = acc / l


def _mobaattn(qt, k4, vt4, sel):
    nh, n, t = qt.shape
    nb = t // MOBA_BLOCK
    return pl.pallas_call(
        _mobaattn_kernel,
        grid=(nh, nb),
        in_specs=[pl.BlockSpec((1, n, MOBA_BLOCK), lambda h, i: (h, 0, i)),
                  pl.BlockSpec((1, nb, MOBA_BLOCK, n), lambda h, i: (h, 0, 0, 0)),
                  pl.BlockSpec((1, nb, n, MOBA_BLOCK), lambda h, i: (h, 0, 0, 0)),
                  pl.BlockSpec((1, nb, MOBA_BLOCK), lambda h, i: (h, 0, i))],
        out_specs=pl.BlockSpec((1, n, MOBA_BLOCK), lambda h, i: (h, 0, i)),
        out_shape=jax.ShapeDtypeStruct((nh, n, t), F32),
        compiler_params=_cparams("arbitrary", "arbitrary"),
        name="mobaattn",
    )(qt, k4, vt4, sel)


def _mix_kernel(ya_ref, yb_ref, sg_ref, x_ref, wa_ref, wb_ref, wo_ref, g1_ref, n2_ref, sc_ref, sh_ref,
                rw_ref, rb_ref, x1_ref, h2_ref, eid_ref, rank_ref, wt_ref, cnt_ref, base_ref):
    i = pl.program_id(0)

    @pl.when(i == 0)
    def _():
        base_ref[...] = jnp.zeros_like(base_ref)

    d = x_ref.shape[1]
    tm = x_ref.shape[0]
    pa = jnp.dot(ya_ref[...].astype(BF16), wa_ref[...], preferred_element_type=F32)
    pb = jnp.dot(yb_ref[...].astype(BF16), wb_ref[...], preferred_element_type=F32)
    sg = sg_ref[...]
    mixed = sg[:, :d] * pa + sg[:, d:] * pb
    mixed = jnp.dot(mixed.astype(BF16), wo_ref[...], preferred_element_type=F32)
    x1 = x_ref[...] + g1_ref[...] * mixed
    x1_ref[...] = x1
    ms = jnp.mean(x1 * x1, axis=-1, keepdims=True)
    h2 = x1 * lax.rsqrt(ms + RMS_EPS) * n2_ref[...]
    h2 = h2 * (1.0 + sc_ref[...]) + sh_ref[...]
    h2_ref[...] = h2

    logits = jnp.dot(h2, rw_ref[...], precision=HI, preferred_element_type=F32) + rb_ref[...]
    ne = logits.shape[1]
    e_id = lax.broadcasted_iota(jnp.int32, (tm, ne), 1)
    neg_inf = jnp.float32(-jnp.inf)
    work = logits
    picks, vals = [], []
    for _ in range(EXPERT_TOP_K):
        m = jnp.max(work, axis=-1, keepdims=True)
        idx = jnp.min(jnp.where(work == m, e_id, ne), axis=-1, keepdims=True)
        pick = e_id == idx
        picks.append((idx, pick))
        vals.append(m)
        work = jnp.where(pick, neg_inf, work)
    den = sum(jnp.exp(v - vals[0]) for v in vals)
    chosen = jnp.zeros((tm, ne), F32)
    for _, pick in picks:
        chosen = jnp.where(pick, 1.0, chosen)
    r_id = lax.broadcasted_iota(jnp.int32, (tm, tm), 0)
    c_id = lax.broadcasted_iota(jnp.int32, (tm, tm), 1)
    before = jnp.where(r_id > c_id, 1.0, 0.0).astype(BF16)
    ahead = jnp.dot(before, chosen.astype(BF16), preferred_element_type=F32) + base_ref[0:1, :]
    k_id = lax.broadcasted_iota(jnp.int32, (tm, EXPERT_TOP_K), 1)
    eid = jnp.zeros((tm, EXPERT_TOP_K), jnp.int32)
    rank = jnp.zeros((tm, EXPERT_TOP_K), jnp.int32)
    wt = jnp.zeros((tm, EXPERT_TOP_K), F32)
    for kk, ((idx, pick), v) in enumerate(zip(picks, vals)):
        rk = jnp.sum(jnp.where(pick, ahead, 0.0), axis=-1, keepdims=True).astype(jnp.int32)
        eid = jnp.where(k_id == kk, idx, eid)
        rank = jnp.where(k_id == kk, rk, rank)
        wt = jnp.where(k_id == kk, jnp.exp(v - vals[0]) / den, wt)
    eid_ref[...] = eid
    rank_ref[...] = rank
    wt_ref[...] = wt
    total = base_ref[0:1, :] + jnp.sum(chosen, axis=0, keepdims=True)
    base_ref[0:1, :] = total
    cnt_ref[...] = jnp.broadcast_to(total, cnt_ref.shape)


def _mix(ya, yb, sg, x2, w_a, w_b, w_o, gate1, norm2_g, scale2, shift2, router_w, router_b):
    t, d = x2.shape
    tm = 256
    ne = router_w.shape[1]
    full = lambda a: pl.BlockSpec(a.shape, lambda i: (0, 0))
    row = lambda n: pl.BlockSpec((tm, n), lambda i: (i, 0))
    return pl.pallas_call(
        _mix_kernel,
        grid=(t // tm,),
        in_specs=[row(WIDTH), row(WIDTH), row(2 * d), row(d), full(w_a), full(w_b), full(w_o),
                  full(gate1), full(norm2_g), full(scale2), full(shift2), full(router_w), full(router_b)],
        out_specs=[row(d), row(d), row(EXPERT_TOP_K), row(EXPERT_TOP_K), row(EXPERT_TOP_K),
                   pl.BlockSpec((8, ne), lambda i: (0, 0))],
        out_shape=[jax.ShapeDtypeStruct((t, d), F32), jax.ShapeDtypeStruct((t, d), F32),
                   jax.ShapeDtypeStruct((t, EXPERT_TOP_K), jnp.int32),
                   jax.ShapeDtypeStruct((t, EXPERT_TOP_K), jnp.int32),
                   jax.ShapeDtypeStruct((t, EXPERT_TOP_K), F32),
                   jax.ShapeDtypeStruct((8, ne), F32)],
        scratch_shapes=[pltpu.VMEM((8, ne), F32)],
        compiler_params=_cparams("arbitrary"),
        name="mix",
    )(ya, yb, sg, x2, w_a, w_b, w_o, gate1, norm2_g, scale2, shift2, router_w, router_b)


DISPATCH_TILE = 256


def _dispatch_kernel(eid_ref, rank_ref, off_ref, h_ref, xs_in_ref, xs_ref, sem):
    del xs_in_ref
    i = pl.program_id(0)
    tm = DISPATCH_TILE

    def body(t, _):
        for kk in range(EXPERT_TOP_K):
            q = t * EXPERT_TOP_K + kk
            slot = off_ref[eid_ref[q]] + rank_ref[q]
            pltpu.make_async_copy(h_ref.at[pl.ds(i * tm + t, 1)], xs_ref.at[pl.ds(slot, 1)], sem).start()
        return 0

    lax.fori_loop(0, tm, body, 0)
    n = tm * EXPERT_TOP_K
    pltpu.make_async_copy(h_ref.at[pl.ds(0, n)], xs_ref.at[pl.ds(0, n)], sem).wait()


def _dispatch(eid_flat, rank_flat, off, h2, n_slots):
    t, d = h2.shape
    tm = DISPATCH_TILE
    smem_blk = pl.BlockSpec((tm * EXPERT_TOP_K,), lambda i: (i,), memory_space=pltpu.SMEM)
    return pl.pallas_call(
        _dispatch_kernel,
        grid=(t // tm,),
        in_specs=[smem_blk, smem_blk,
                  pl.BlockSpec(memory_space=pltpu.SMEM),
                  pl.BlockSpec(memory_space=pl.ANY),
                  pl.BlockSpec(memory_space=pl.ANY)],
        out_specs=pl.BlockSpec(memory_space=pl.ANY),
        out_shape=jax.ShapeDtypeStruct((n_slots, d), F32),
        scratch_shapes=[pltpu.SemaphoreType.DMA(())],
        input_output_aliases={4: 0},
        compiler_params=pltpu.CompilerParams(dimension_semantics=("arbitrary",), has_side_effects=True),
        name="dispatch",
    )(eid_flat, rank_flat, off, h2, jnp.zeros((n_slots, d), F32))


def _experts_kernel(te_ref, nu_ref, x_ref, wg_ref, bg_ref, wu_ref, bu_ref, wd_ref, bd_ref, y_ref):
    i = pl.program_id(0)

    @pl.when(i < nu_ref[0])
    def _():
        x = x_ref[...].astype(BF16)
        gt = jnp.minimum(jnp.dot(x, wg_ref[0], preferred_element_type=F32) + bg_ref[0], SWIGLU_LIMIT)
        up = jnp.clip(jnp.dot(x, wu_ref[0], preferred_element_type=F32) + bu_ref[0], -SWIGLU_LIMIT, SWIGLU_LIMIT)
        hid = (up + 1.0) * gt * _sigmoid(SWIGLU_ALPHA * gt)
        y_ref[...] = jnp.dot(hid.astype(BF16), wd_ref[0], preferred_element_type=F32) + bd_ref[0]

    @pl.when(i >= nu_ref[0])
    def _():
        y_ref[...] = jnp.zeros_like(y_ref)


def _experts(tile_expert, n_used, xs, w_gate, b_gate, w_up, b_up, w_down, b_down):
    n_slots, d = xs.shape
    f = w_gate.shape[2]
    tm = EXPERT_TILE
    n_tiles = n_slots // tm
    row = pl.BlockSpec((tm, d), lambda i, te, nu: (i, 0))
    wspec = lambda a, b: pl.BlockSpec((1, a, b), lambda i, te, nu: (te[i], 0, 0))
    grid_spec = pltpu.PrefetchScalarGridSpec(
        num_scalar_prefetch=2,
        grid=(n_tiles,),
        in_specs=[row, wspec(d, f), wspec(1, f), wspec(d, f), wspec(1, f), wspec(f, d), wspec(1, d)],
        out_specs=row,
    )
    return pl.pallas_call(
        _experts_kernel,
        grid_spec=grid_spec,
        out_shape=jax.ShapeDtypeStruct((n_slots, d), F32),
        compiler_params=_cparams("arbitrary"),
        name="experts",
    )(tile_expert, n_used, xs, w_gate, b_gate, w_up, b_up, w_down, b_down)


def _combine_kernel(eid_ref, rank_ref, off_ref, ys_ref, wt_ref, x1_ref, g2_ref, nf_ref, o_ref, buf, sem):
    tm = DISPATCH_TILE

    def body(t, _):
        for kk in range(EXPERT_TOP_K):
            q = t * EXPERT_TOP_K + kk
            slot = off_ref[eid_ref[q]] + rank_ref[q]
            pltpu.make_async_copy(ys_ref.at[pl.ds(slot, 1)], buf.at[kk, pl.ds(t, 1)], sem).start()
        return 0

    lax.fori_loop(0, tm, body, 0)
    for kk in range(EXPERT_TOP_K):
        pltpu.make_async_copy(ys_ref.at[pl.ds(0, tm)], buf.at[kk], sem).wait()
    wt = wt_ref[...]
    moe = wt[:, 0:1] * buf[0]
    for kk in range(1, EXPERT_TOP_K):
        moe = moe + wt[:, kk:kk + 1] * buf[kk]
    x = x1_ref[...] + g2_ref[...] * moe
    ms = jnp.mean(x * x, axis=-1, keepdims=True)
    o_ref[...] = x * lax.rsqrt(ms + RMS_EPS) * nf_ref[...]


def _combine(eid_flat, rank_flat, off, ys, wt, x1, gate2, normf_g):
    t, d = x1.shape
    tm = DISPATCH_TILE
    smem_blk = pl.BlockSpec((tm * EXPERT_TOP_K,), lambda i: (i,), memory_space=pltpu.SMEM)
    full = lambda a: pl.BlockSpec(a.shape, lambda i: (0, 0))
    row = lambda n: pl.BlockSpec((tm, n), lambda i: (i, 0))
    return pl.pallas_call(
        _combine_kernel,
        grid=(t // tm,),
        in_specs=[smem_blk, smem_blk,
                  pl.BlockSpec(memory_space=pltpu.SMEM),
                  pl.BlockSpec(memory_space=pl.ANY),
                  row(EXPERT_TOP_K), row(d), full(gate2), full(normf_g)],
        out_specs=row(d),
        out_shape=jax.ShapeDtypeStruct((t, d), F32),
        scratch_shapes=[pltpu.VMEM((EXPERT_TOP_K, tm, d), F32), pltpu.SemaphoreType.DMA(())],
        compiler_params=_cparams("arbitrary"),
        name="combine",
    )(eid_flat, rank_flat, off, ys, wt, x1, gate2, normf_g)


def _heads(a):
    t = a.shape[0]
    return a.reshape(t, N_HEADS, HEAD_DIM).transpose(1, 0, 2)


def _unheads(a):
    return a.transpose(1, 0, 2).reshape(a.shape[1], WIDTH)


def kernel(x, c, w_ada, b_ada, norm1_g, w_in, rwkv_mu, rwkv_w0, rwkv_w2, rwkv_a0, rwkv_a2, rwkv_g2, rwkv_k_k, rwkv_k_a, rwkv_r_k, rwkv_ln_w, rwkv_ln_b, w_branch_a, w_branch_b, w_out, norm2_g, router_w, router_b, exp_w_gate, exp_b_gate, exp_w_up, exp_b_up, exp_w_down, exp_b_down, normf_g):
    bsz, t, d = x.shape
    assert bsz == 1 and t % MOBA_BLOCK == 0
    x2 = x.reshape(t, d)
    row = lambda a: a.reshape(1, -1)

    mod = _mod(c, w_ada, b_ada)
    shift1, scale1, gate1, shift2, scale2, gate2 = [mod[:, j * d:(j + 1) * d] for j in range(6)]

    qkv_end = RWKV_PROJ + 3 * WIDTH
    w_in_b = w_in.astype(BF16)
    p_rwkv, qkv, sg = _inproj(x2, row(norm1_g), scale1, shift1,
                              w_in_b[:, :RWKV_PROJ], w_in_b[:, RWKV_PROJ:qkv_end], w_in_b[:, qkv_end:])

    zeros_lora = jnp.zeros((DECAY_LORA, WIDTH), F32)
    w2p = jnp.concatenate([rwkv_w2, zeros_lora], axis=0)
    a2p = jnp.concatenate([zeros_lora, rwkv_a2], axis=0)
    head_of = jnp.arange(WIDTH) // HEAD_DIM
    bd = (head_of[:, None] == head_of[None, :]).astype(F32)
    r, lw, k, v, av, bv, g = _rwkvprep(p_rwkv, row(rwkv_mu), row(rwkv_w0), w2p, row(rwkv_a0), a2p, rwkv_g2,
                                       row(rwkv_k_k), row(rwkv_k_a), bd)
    per_head = lambda a: a.reshape(N_HEADS, 1, HEAD_DIM)
    y_a = _rwkvscan(_heads(r), _heads(lw), _heads(k), _heads(v), _heads(av), _heads(bv), _heads(g),
                    per_head(rwkv_ln_w), per_head(rwkv_ln_b), per_head(rwkv_r_k))
    y_a = _unheads(y_a)

    nb = t // MOBA_BLOCK
    q_hm = _heads(qkv[:, :WIDTH])
    k_hm = _heads(qkv[:, WIDTH:2 * WIDTH])
    v_hm = _heads(qkv[:, 2 * WIDTH:])
    if nb > 1:
        sel = _mobasel(q_hm, _kmean(k_hm))
    else:
        sel = jnp.zeros((N_HEADS, nb, t), F32)
    qt = (q_hm * (HEAD_DIM ** -0.5)).astype(BF16).transpose(0, 2, 1)
    k4 = k_hm.astype(BF16).reshape(N_HEADS, nb, MOBA_BLOCK, HEAD_DIM)
    vt4 = v_hm.astype(BF16).reshape(N_HEADS, nb, MOBA_BLOCK, HEAD_DIM).transpose(0, 1, 3, 2)
    y_b = _mobaattn(qt, k4, vt4, sel)
    y_b = y_b.transpose(2, 0, 1).reshape(t, WIDTH)

    x1, h2, eid, rank, wt, cnt = _mix(y_a, y_b, sg, x2, w_branch_a.astype(BF16), w_branch_b.astype(BF16),
                                      w_out.astype(BF16), gate1, row(norm2_g), scale2, shift2,
                                      router_w, row(router_b))

    counts = cnt[0].astype(jnp.int32)
    tiles_per = (counts + EXPERT_TILE - 1) // EXPERT_TILE
    tile_end = jnp.cumsum(tiles_per)
    off = (tile_end - tiles_per) * EXPERT_TILE
    n_tiles = (t * EXPERT_TOP_K + N_EXPERTS * (EXPERT_TILE - 1)) // EXPERT_TILE
    n_used = tile_end[-1:]
    tile_expert = jnp.minimum(jnp.searchsorted(tile_end, jnp.arange(n_tiles), side="right"),
                              N_EXPERTS - 1).astype(jnp.int32)
    last_used = tile_expert[jnp.maximum(n_used[0] - 1, 0)]
    tile_expert = jnp.where(jnp.arange(n_tiles) < n_used[0], tile_expert, last_used)

    eid_flat = eid.reshape(-1)
    rank_flat = rank.reshape(-1)
    xs = _dispatch(eid_flat, rank_flat, off, h2, n_tiles * EXPERT_TILE)
    f = exp_w_gate.shape[2]
    ys = _experts(tile_expert, n_used, xs,
                  exp_w_gate.astype(BF16), exp_b_gate.reshape(N_EXPERTS, 1, f),
                  exp_w_up.astype(BF16), exp_b_up.reshape(N_EXPERTS, 1, f),
                  exp_w_down.astype(BF16), exp_b_down.reshape(N_EXPERTS, 1, d))
    out = _combine(eid_flat, rank_flat, off, ys, wt, x1, gate2, row(normf_g))
    return out.reshape(bsz, t, d)
```

```python
import functools

import jax
import jax.numpy as jnp
from jax import lax
from jax.experimental import pallas as pl
from jax.experimental.pallas import tpu as pltpu

F32 = jnp.float32
BF16 = jnp.bfloat16
HI = lax.Precision.HIGHEST

HEAD_DIM = 64
N_HEADS = 8
WIDTH = N_HEADS * HEAD_DIM
DECAY_LORA = 64
AAA_LORA = 64
GATE_LORA = 128
RWKV_PROJ = 3 * WIDTH + DECAY_LORA + AAA_LORA + GATE_LORA
GN_EPS = 64e-5
MOBA_BLOCK = 256
MOBA_TOP_K = 3
KV_GROUP = 4
N_EXPERTS = 32
EXPERT_TOP_K = 4
SWIGLU_LIMIT = 7.0
SWIGLU_ALPHA = 1.702
RMS_EPS = 1e-5
GATE_NEG = -1e30
MASK_NEG = -1e30

SCAN_CHUNK = 64
EXPERT_TILE = 256
VMEM_LIMIT = 56 * 1024 * 1024


def _cparams(*sem):
    return pltpu.CompilerParams(dimension_semantics=sem, vmem_limit_bytes=VMEM_LIMIT)


def _sigmoid(x):
    return 1.0 / (1.0 + jnp.exp(-x))


def _mod_kernel(c_ref, w_ref, b_ref, o_ref):
    c = c_ref[...]
    s = c * _sigmoid(c)
    o_ref[...] = jnp.dot(s, w_ref[...], precision=HI, preferred_element_type=F32) + b_ref[...]


def _mod(c, w_ada, b_ada):
    d = c.shape[-1]
    n = w_ada.shape[1]
    c8 = jnp.broadcast_to(c[:1], (8, d))
    tn = 1024
    out = pl.pallas_call(
        _mod_kernel,
        grid=(n // tn,),
        in_specs=[pl.BlockSpec((8, d), lambda j: (0, 0)),
                  pl.BlockSpec((d, tn), lambda j: (0, j)),
                  pl.BlockSpec((1, tn), lambda j: (0, j))],
        out_specs=pl.BlockSpec((8, tn), lambda j: (0, j)),
        out_shape=jax.ShapeDtypeStruct((8, n), F32),
        compiler_params=_cparams("arbitrary"),
        name="mod",
    )(c8, w_ada, b_ada.reshape(1, n))
    return out[:1]


def _inproj_kernel(x_ref, g_ref, sc_ref, sh_ref, wr_ref, wq_ref, wg_ref, pr_ref, qkv_ref, sg_ref, km_ref):
    x = x_ref[...]
    ms = jnp.mean(x * x, axis=-1, keepdims=True)
    h = x * lax.rsqrt(ms + RMS_EPS) * g_ref[...]
    h = h * (1.0 + sc_ref[...]) + sh_ref[...]
    hb = h.astype(BF16)
    pr_ref[...] = jnp.dot(hb, wr_ref[...], preferred_element_type=F32)
    qkv = jnp.dot(hb, wq_ref[...], preferred_element_type=F32)
    qkv_ref[...] = qkv
    km_ref[0] = jnp.mean(qkv[:, WIDTH:2 * WIDTH], axis=0, keepdims=True)
    sg_ref[...] = _sigmoid(jnp.dot(hb, wg_ref[...], preferred_element_type=F32))


def _inproj(x2, norm_g, scale, shift, w_r, w_q, w_g):
    t, d = x2.shape
    tm = MOBA_BLOCK
    full = lambda a: pl.BlockSpec(a.shape, lambda i: (0, 0))
    row = lambda n: pl.BlockSpec((tm, n), lambda i: (i, 0))
    return pl.pallas_call(
        _inproj_kernel,
        grid=(t // tm,),
        in_specs=[row(d), full(norm_g), full(scale), full(shift), full(w_r), full(w_q), full(w_g)],
        out_specs=[row(w_r.shape[1]), row(w_q.shape[1]), row(w_g.shape[1]),
                   pl.BlockSpec((1, 1, WIDTH), lambda i: (i, 0, 0))],
        out_shape=[jax.ShapeDtypeStruct((t, w_r.shape[1]), F32),
                   jax.ShapeDtypeStruct((t, w_q.shape[1]), F32),
                   jax.ShapeDtypeStruct((t, w_g.shape[1]), F32),
                   jax.ShapeDtypeStruct((t // tm, 1, WIDTH), F32)],
        compiler_params=_cparams("arbitrary"),
        name="inproj",
    )(x2, norm_g, scale, shift, w_r, w_q, w_g)


def _rwkvprep_kernel(p_ref, mu_ref, w0_ref, w2_ref, a0_ref, a2_ref, g2_ref, kk_ref, ka_ref, bd_ref,
                     r_out, lw_out, k_out, v_out, a_out, b_out, g_out, prev_ref):
    i = pl.program_id(0)

    @pl.when(i == 0)
    def _():
        prev_ref[...] = jnp.zeros_like(prev_ref)

    p = p_ref[...]
    tm = p.shape[0]
    rolled = pltpu.roll(p, 1, 0)
    row0 = lax.broadcasted_iota(jnp.int32, p.shape, 0) == 0
    p_prev = jnp.where(row0, prev_ref[0:1, :], rolled)
    prev_ref[0:1, :] = p[tm - 1:tm, :]
    ps = p + (p_prev - p) * mu_ref[...]
    r = ps[:, 0:WIDTH]
    k = ps[:, WIDTH:2 * WIDTH]
    v = ps[:, 2 * WIDTH:3 * WIDTH]
    xwa = ps[:, 3 * WIDTH:3 * WIDTH + DECAY_LORA + AAA_LORA]
    xg = ps[:, 3 * WIDTH + DECAY_LORA + AAA_LORA:RWKV_PROJ]
    dot = functools.partial(jnp.dot, precision=HI, preferred_element_type=F32)
    z = w0_ref[...] + dot(jnp.tanh(xwa), w2_ref[...])
    nz = -z
    softplus = jnp.maximum(nz, 0.0) + jnp.log(1.0 + jnp.exp(-jnp.abs(nz)))
    w = -softplus - 0.5
    a = _sigmoid(a0_ref[...] + dot(xwa, a2_ref[...]))
    g = dot(_sigmoid(xg), g2_ref[...])
    kk = k * kk_ref[...]
    ss = dot(kk * kk, bd_ref[...])
    kk = kk / jnp.maximum(jnp.sqrt(ss), 1e-12)
    r_out[...] = r
    lw_out[...] = -jnp.exp(w)
    k_out[...] = k * (1.0 + (a - 1.0) * ka_ref[...])
    v_out[...] = v
    a_out[...] = -kk
    b_out[...] = kk * a
    g_out[...] = g


def _rwkvprep(p_rwkv, mu, w0, w2p, a0, a2p, g2, k_k, k_a, bd):
    t = p_rwkv.shape[0]
    tm = 256
    full = lambda a: pl.BlockSpec(a.shape, lambda i: (0, 0))
    row = lambda n: pl.BlockSpec((tm, n), lambda i: (i, 0))
    outs = [jax.ShapeDtypeStruct((t, WIDTH), F32)] * 7
    return pl.pallas_call(
        _rwkvprep_kernel,
        grid=(t // tm,),
        in_specs=[row(RWKV_PROJ), full(mu), full(w0), full(w2p), full(a0), full(a2p), full(g2),
                  full(k_k), full(k_a), full(bd)],
        out_specs=[row(WIDTH)] * 7,
        out_shape=outs,
        scratch_shapes=[pltpu.VMEM((8, RWKV_PROJ), F32)],
        compiler_params=_cparams("arbitrary"),
        name="rwkvprep",
    )(p_rwkv, mu, w0, w2p, a0, a2p, g2, k_k, k_a, bd)


def _bmm(a, b):
    return jnp.einsum("hmk,hkn->hmn", a, b, precision=HI, preferred_element_type=F32)


def _bmm_nt(a, b):
    return jnp.einsum("hmk,hnk->hmn", a, b, precision=HI, preferred_element_type=F32)


def _bmm_tn(a, b):
    return jnp.einsum("hkm,hkn->hmn", a, b, precision=HI, preferred_element_type=F32)


def _rwkvscan_kernel(r_ref, lw_ref, k_ref, v_ref, a_ref, b_ref, g_ref, lnw_ref, lnb_ref, rk_ref,
                     y_ref, s_ref):
    i = pl.program_id(0)

    @pl.when(i == 0)
    def _():
        s_ref[...] = jnp.zeros_like(s_ref)

    c = SCAN_CHUNK
    r = r_ref[...]
    lw = lw_ref[...]
    k = k_ref[...]
    v = v_ref[...]
    a = a_ref[...]
    b = b_ref[...]
    s0 = s_ref[...]
    nh = r.shape[0]

    row = lax.broadcasted_iota(jnp.int32, (nh, c, c), 1)
    col = lax.broadcasted_iota(jnp.int32, (nh, c, c), 2)
    lower = row >= col
    strict = row > col
    cum = _bmm(lower.astype(F32), lw)
    tot = cum[:, c - 1:c, :]
    g_in = jnp.exp(cum)
    g_ex = jnp.exp(cum - lw)
    g_inv = jnp.exp(-cum)
    g_rem = jnp.exp(tot - cum)
    a_t = a * g_ex
    r_t = r * g_in
    b_t = b * g_inv
    k_t = k * g_inv
    l_ab = jnp.where(strict, _bmm_nt(a_t, b_t), 0.0)
    l_ak = jnp.where(strict, _bmm_nt(a_t, k_t), 0.0)
    m_rb = jnp.where(lower, _bmm_nt(r_t, b_t), 0.0)
    m_rk = jnp.where(lower, _bmm_nt(r_t, k_t), 0.0)
    inv = jnp.where(row == col, 1.0, 0.0) + l_ab
    lp = l_ab
    for _ in range(5):
        lp = _bmm(lp, lp)
        inv = inv + _bmm(inv, lp)
    x = _bmm_nt(a_t, s0) + _bmm(l_ak, v)
    u = _bmm(inv, x)
    y = _bmm_nt(r_t, s0) + _bmm(m_rb, u) + _bmm(m_rk, v)
    s_ref[...] = s0 * jnp.exp(tot) + _bmm_tn(u, b * g_rem) + _bmm_tn(v, k * g_rem)

    mean = jnp.mean(y, axis=-1, keepdims=True)
    yc = y - mean
    var = jnp.mean(yc * yc, axis=-1, keepdims=True)
    yn = yc * lax.rsqrt(var + GN_EPS) * lnw_ref[...] + lnb_ref[...]
    bonus = jnp.sum(r * k * rk_ref[...], axis=-1, keepdims=True) * v
    y_ref[...] = (yn + bonus) * g_ref[...]


def _rwkvscan(r, lw, k, v, a, b, g, ln_w, ln_b, r_k):
    nh, t, n = r.shape
    c = SCAN_CHUNK
    blk = pl.BlockSpec((nh, c, n), lambda i: (0, i, 0))
    par = pl.BlockSpec((nh, 1, n), lambda i: (0, 0, 0))
    return pl.pallas_call(
        _rwkvscan_kernel,
        grid=(t // c,),
        in_specs=[blk] * 7 + [par] * 3,
        out_specs=blk,
        out_shape=jax.ShapeDtypeStruct((nh, t, n), F32),
        scratch_shapes=[pltpu.VMEM((nh, n, n), F32)],
        compiler_params=_cparams("arbitrary"),
        name="rwkvscan",
    )(r, lw, k, v, a, b, g, ln_w, ln_b, r_k)


def _mobasel_kernel(q_ref, km_ref, o_ref):
    j = pl.program_id(1)
    q = q_ref[0]
    km = km_ref[0]
    nb = km.shape[0]
    tq = q.shape[0]
    gate = lax.dot_general(km, q, (((1,), (1,)), ((), ())), precision=HI, preferred_element_type=F32)
    n_id = lax.broadcasted_iota(jnp.int32, (nb, tq), 0)
    t_id = lax.broadcasted_iota(jnp.int32, (nb, tq), 1) + j * tq
    q_blk = t_id // MOBA_BLOCK
    cand = n_id < nb - 1
    past = n_id < q_blk
    neg_inf = jnp.float32(-jnp.inf)
    gate = jnp.where(cand, jnp.where(past, gate, GATE_NEG), neg_inf)
    sel = jnp.zeros((nb, tq), F32)
    for _ in range(min(MOBA_TOP_K, nb - 1)):
        m = jnp.max(gate, axis=0, keepdims=True)
        idx = jnp.min(jnp.where(gate == m, n_id, nb), axis=0, keepdims=True)
        pick = n_id == idx
        sel = jnp.where(pick, 1.0, sel)
        gate = jnp.where(pick, neg_inf, gate)
    o_ref[0] = jnp.where(past, sel, 0.0)


def _mobasel(q_hm, kmean):
    nh, t, n = q_hm.shape
    nb = kmean.shape[1]
    tq = min(t, 2048)
    return pl.pallas_call(
        _mobasel_kernel,
        grid=(nh, t // tq),
        in_specs=[pl.BlockSpec((1, tq, n), lambda h, j: (h, j, 0)),
                  pl.BlockSpec((1, nb, n), lambda h, j: (h, 0, 0))],
        out_specs=pl.BlockSpec((1, nb, tq), lambda h, j: (h, 0, j)),
        out_shape=jax.ShapeDtypeStruct((nh, nb, t), F32),
        compiler_params=_cparams("arbitrary", "arbitrary"),
        name="mobasel",
    )(q_hm, kmean)


def _mobaattn_kernel(qt_ref, k_ref, vt_ref, sel_ref, o_ref):
    i = pl.program_id(1)
    bs = MOBA_BLOCK
    qt = qt_ref[0]

    def sel_mask(g):
        rows = sel_ref[0, g]
        return [jnp.broadcast_to(rows[b:b + 1, :], (bs, bs)) > 0.5 for b in range(KV_GROUP)]

    def step(g, carry, masks):
        m, l, acc = carry
        s = jnp.dot(k_ref[0, g], qt, preferred_element_type=F32)
        s = jnp.where(jnp.concatenate(masks, axis=0), s, MASK_NEG)
        m_new = jnp.maximum(m, jnp.max(s, axis=0, keepdims=True))
        alpha = jnp.exp(m - m_new)
        p = jnp.exp(s - m_new)
        l = alpha * l + jnp.sum(p, axis=0, keepdims=True)
        acc = alpha * acc + jnp.dot(vt_ref[0, g], p.astype(BF16), preferred_element_type=F32)
        return m_new, l, acc

    gd = i // KV_GROUP
    causal = lax.broadcasted_iota(jnp.int32, (bs, bs), 0) <= lax.broadcasted_iota(jnp.int32, (bs, bs), 1)
    own = [jnp.logical_or(mk, jnp.logical_and(causal, i % KV_GROUP == b)) for b, mk in enumerate(sel_mask(gd))]
    init = (jnp.full((1, bs), MASK_NEG, F32), jnp.zeros((1, bs), F32), jnp.zeros((HEAD_DIM, bs), F32))
    carry = step(gd, init, own)

    def body(g, carry):
        return step(g, carry, sel_mask(g))

    m, l, acc = lax.fori_loop(0, gd, body, carry)
    o_ref[0] = acc / l


def _mobaattn(qt, k4, vt4, sel4):
    nh, n, t = qt.shape
    gs = MOBA_BLOCK * KV_GROUP
    ng = t // gs
    return pl.pallas_call(
        _mobaattn_kernel,
        grid=(nh, t // MOBA_BLOCK),
        in_specs=[pl.BlockSpec((1, n, MOBA_BLOCK), lambda h, i: (h, 0, i)),
                  pl.BlockSpec((1, ng, gs, n), lambda h, i: (h, 0, 0, 0)),
                  pl.BlockSpec((1, ng, n, gs), lambda h, i: (h, 0, 0, 0)),
                  pl.BlockSpec((1, ng, KV_GROUP, MOBA_BLOCK), lambda h, i: (h, 0, 0, i))],
        out_specs=pl.BlockSpec((1, n, MOBA_BLOCK), lambda h, i: (h, 0, i)),
        out_shape=jax.ShapeDtypeStruct((nh, n, t), F32),
        compiler_params=_cparams("arbitrary", "arbitrary"),
        name="mobaattn",
    )(qt, k4, vt4, sel4)


def _mix_kernel(ya_ref, yb_ref, sg_ref, x_ref, wa_ref, wb_ref, wo_ref, g1_ref, n2_ref, sc_ref, sh_ref,
                rw_ref, rb_ref, x1_ref, h2_ref, eid_ref, rank_ref, wt_ref, cnt_ref, base_ref):
    i = pl.program_id(0)

    @pl.when(i == 0)
    def _():
        base_ref[...] = jnp.zeros_like(base_ref)

    d = x_ref.shape[1]
    tm = x_ref.shape[0]
    pa = jnp.dot(ya_ref[...].astype(BF16), wa_ref[...], preferred_element_type=F32)
    pb = jnp.dot(yb_ref[...].astype(BF16), wb_ref[...], preferred_element_type=F32)
    sg = sg_ref[...]
    mixed = sg[:, :d] * pa + sg[:, d:] * pb
    mixed = jnp.dot(mixed.astype(BF16), wo_ref[...], preferred_element_type=F32)
    x1 = x_ref[...] + g1_ref[...] * mixed
    x1_ref[...] = x1
    ms = jnp.mean(x1 * x1, axis=-1, keepdims=True)
    h2 = x1 * lax.rsqrt(ms + RMS_EPS) * n2_ref[...]
    h2 = h2 * (1.0 + sc_ref[...]) + sh_ref[...]
    h2_ref[...] = h2

    logits = jnp.dot(h2, rw_ref[...], precision=HI, preferred_element_type=F32) + rb_ref[...]
    ne = logits.shape[1]
    e_id = lax.broadcasted_iota(jnp.int32, (tm, ne), 1)
    neg_inf = jnp.float32(-jnp.inf)
    work = logits
    picks, vals = [], []
    for _ in range(EXPERT_TOP_K):
        m = jnp.max(work, axis=-1, keepdims=True)
        idx = jnp.min(jnp.where(work == m, e_id, ne), axis=-1, keepdims=True)
        pick = e_id == idx
        picks.append((idx, pick))
        vals.append(m)
        work = jnp.where(pick, neg_inf, work)
    den = sum(jnp.exp(v - vals[0]) for v in vals)
    chosen = jnp.zeros((tm, ne), F32)
    for _, pick in picks:
        chosen = jnp.where(pick, 1.0, chosen)
    r_id = lax.broadcasted_iota(jnp.int32, (tm, tm), 0)
    c_id = lax.broadcasted_iota(jnp.int32, (tm, tm), 1)
    before = jnp.where(r_id > c_id, 1.0, 0.0).astype(BF16)
    ahead = jnp.dot(before, chosen.astype(BF16), preferred_element_type=F32) + base_ref[0:1, :]
    k_id = lax.broadcasted_iota(jnp.int32, (tm, EXPERT_TOP_K), 1)
    eid = jnp.zeros((tm, EXPERT_TOP_K), jnp.int32)
    rank = jnp.zeros((tm, EXPERT_TOP_K), jnp.int32)
    wt = jnp.zeros((tm, EXPERT_TOP_K), F32)
    for kk, ((idx, pick), v) in enumerate(zip(picks, vals)):
        rk = jnp.sum(jnp.where(pick, ahead, 0.0), axis=-1, keepdims=True).astype(jnp.int32)
        eid = jnp.where(k_id == kk, idx, eid)
        rank = jnp.where(k_id == kk, rk, rank)
        wt = jnp.where(k_id == kk, jnp.exp(v - vals[0]) / den, wt)
    eid_ref[...] = eid
    rank_ref[...] = rank
    wt_ref[...] = wt
    total = base_ref[0:1, :] + jnp.sum(chosen, axis=0, keepdims=True)
    base_ref[0:1, :] = total
    cnt_ref[...] = jnp.broadcast_to(total, cnt_ref.shape)


def _mix(ya, yb, sg, x2, w_a, w_b, w_o, gate1, norm2_g, scale2, shift2, router_w, router_b):
    t, d = x2.shape
    tm = 256
    ne = router_w.shape[1]
    full = lambda a: pl.BlockSpec(a.shape, lambda i: (0, 0))
    row = lambda n: pl.BlockSpec((tm, n), lambda i: (i, 0))
    return pl.pallas_call(
        _mix_kernel,
        grid=(t // tm,),
        in_specs=[row(WIDTH), row(WIDTH), row(2 * d), row(d), full(w_a), full(w_b), full(w_o),
                  full(gate1), full(norm2_g), full(scale2), full(shift2), full(router_w), full(router_b)],
        out_specs=[row(d), row(d), row(EXPERT_TOP_K), row(EXPERT_TOP_K), row(EXPERT_TOP_K),
                   pl.BlockSpec((8, ne), lambda i: (0, 0))],
        out_shape=[jax.ShapeDtypeStruct((t, d), F32), jax.ShapeDtypeStruct((t, d), F32),
                   jax.ShapeDtypeStruct((t, EXPERT_TOP_K), jnp.int32),
                   jax.ShapeDtypeStruct((t, EXPERT_TOP_K), jnp.int32),
                   jax.ShapeDtypeStruct((t, EXPERT_TOP_K), F32),
                   jax.ShapeDtypeStruct((8, ne), F32)],
        scratch_shapes=[pltpu.VMEM((8, ne), F32)],
        compiler_params=_cparams("arbitrary"),
        name="mix",
    )(ya, yb, sg, x2, w_a, w_b, w_o, gate1, norm2_g, scale2, shift2, router_w, router_b)


DISPATCH_TILE = 256


def _dispatch_kernel(eid_ref, rank_ref, off_ref, h_ref, xs_in_ref, xs_ref, sem):
    del xs_in_ref
    tm = DISPATCH_TILE

    def body(t, _):
        for kk in range(EXPERT_TOP_K):
            q = t * EXPERT_TOP_K + kk
            slot = off_ref[eid_ref[q]] + rank_ref[q]
            pltpu.make_async_copy(h_ref.at[pl.ds(t, 1)], xs_ref.at[pl.ds(slot, 1)], sem).start()
        return 0

    lax.fori_loop(0, tm, body, 0)
    for _ in range(EXPERT_TOP_K):
        pltpu.make_async_copy(h_ref, xs_ref.at[pl.ds(0, tm)], sem).wait()


def _dispatch(eid_flat, rank_flat, off, h2, n_slots):
    t, d = h2.shape
    tm = DISPATCH_TILE
    smem_blk = pl.BlockSpec((tm * EXPERT_TOP_K,), lambda i: (i,), memory_space=pltpu.SMEM)
    return pl.pallas_call(
        _dispatch_kernel,
        grid=(t // tm,),
        in_specs=[smem_blk, smem_blk,
                  pl.BlockSpec(memory_space=pltpu.SMEM),
                  pl.BlockSpec((tm, d), lambda i: (i, 0)),
                  pl.BlockSpec(memory_space=pl.ANY)],
        out_specs=pl.BlockSpec(memory_space=pl.ANY),
        out_shape=jax.ShapeDtypeStruct((n_slots, d), F32),
        scratch_shapes=[pltpu.SemaphoreType.DMA(())],
        input_output_aliases={4: 0},
        compiler_params=pltpu.CompilerParams(dimension_semantics=("arbitrary",), has_side_effects=True),
        name="dispatch",
    )(eid_flat, rank_flat, off, h2, jnp.zeros((n_slots, d), F32))


def _experts_kernel(te_ref, nu_ref, x_ref, wg_ref, bg_ref, wu_ref, bu_ref, wd_ref, bd_ref, y_ref):
    i = pl.program_id(0)

    @pl.when(i < nu_ref[0])
    def _():
        x = x_ref[...].astype(BF16)
        gt = jnp.minimum(jnp.dot(x, wg_ref[0], preferred_element_type=F32) + bg_ref[0], SWIGLU_LIMIT)
        up = jnp.clip(jnp.dot(x, wu_ref[0], preferred_element_type=F32) + bu_ref[0], -SWIGLU_LIMIT, SWIGLU_LIMIT)
        hid = (up + 1.0) * gt * _sigmoid(SWIGLU_ALPHA * gt)
        y_ref[...] = jnp.dot(hid.astype(BF16), wd_ref[0], preferred_element_type=F32) + bd_ref[0]

    @pl.when(i >= nu_ref[0])
    def _():
        y_ref[...] = jnp.zeros_like(y_ref)


def _experts(tile_expert, n_used, xs, w_gate, b_gate, w_up, b_up, w_down, b_down):
    n_slots, d = xs.shape
    f = w_gate.shape[2]
    tm = EXPERT_TILE
    n_tiles = n_slots // tm
    row = pl.BlockSpec((tm, d), lambda i, te, nu: (i, 0))
    wspec = lambda a, b: pl.BlockSpec((1, a, b), lambda i, te, nu: (te[i], 0, 0))
    grid_spec = pltpu.PrefetchScalarGridSpec(
        num_scalar_prefetch=2,
        grid=(n_tiles,),
        in_specs=[row, wspec(d, f), wspec(1, f), wspec(d, f), wspec(1, f), wspec(f, d), wspec(1, d)],
        out_specs=row,
    )
    return pl.pallas_call(
        _experts_kernel,
        grid_spec=grid_spec,
        out_shape=jax.ShapeDtypeStruct((n_slots, d), F32),
        compiler_params=_cparams("arbitrary"),
        name="experts",
    )(tile_expert, n_used, xs, w_gate, b_gate, w_up, b_up, w_down, b_down)


def _combine_kernel(eid_ref, rank_ref, off_ref, ys_ref, wt_ref, x1_ref, g2_ref, nf_ref, o_ref, buf, sem):
    tm = DISPATCH_TILE

    def body(t, _):
        for kk in range(EXPERT_TOP_K):
            q = t * EXPERT_TOP_K + kk
            slot = off_ref[eid_ref[q]] + rank_ref[q]
            pltpu.make_async_copy(ys_ref.at[pl.ds(slot, 1)], buf.at[kk, pl.ds(t, 1)], sem).start()
        return 0

    lax.fori_loop(0, tm, body, 0)
    for kk in range(EXPERT_TOP_K):
        pltpu.make_async_copy(ys_ref.at[pl.ds(0, tm)], buf.at[kk], sem).wait()
    wt = wt_ref[...]
    moe = wt[:, 0:1] * buf[0]
    for kk in range(1, EXPERT_TOP_K):
        moe = moe + wt[:, kk:kk + 1] * buf[kk]
    x = x1_ref[...] + g2_ref[...] * moe
    ms = jnp.mean(x * x, axis=-1, keepdims=True)
    o_ref[...] = x * lax.rsqrt(ms + RMS_EPS) * nf_ref[...]


def _combine(eid_flat, rank_flat, off, ys, wt, x1, gate2, normf_g):
    t, d = x1.shape
    tm = DISPATCH_TILE
    smem_blk = pl.BlockSpec((tm * EXPERT_TOP_K,), lambda i: (i,), memory_space=pltpu.SMEM)
    full = lambda a: pl.BlockSpec(a.shape, lambda i: (0, 0))
    row = lambda n: pl.BlockSpec((tm, n), lambda i: (i, 0))
    return pl.pallas_call(
        _combine_kernel,
        grid=(t // tm,),
        in_specs=[smem_blk, smem_blk,
                  pl.BlockSpec(memory_space=pltpu.SMEM),
                  pl.BlockSpec(memory_space=pl.ANY),
                  row(EXPERT_TOP_K), row(d), full(gate2), full(normf_g)],
        out_specs=row(d),
        out_shape=jax.ShapeDtypeStruct((t, d), F32),
        scratch_shapes=[pltpu.VMEM((EXPERT_TOP_K, tm, d), F32), pltpu.SemaphoreType.DMA(())],
        compiler_params=_cparams("arbitrary"),
        name="combine",
    )(eid_flat, rank_flat, off, ys, wt, x1, gate2, normf_g)


def _heads(a):
    t = a.shape[0]
    return a.reshape(t, N_HEADS, HEAD_DIM).transpose(1, 0, 2)


def _unheads(a):
    return a.transpose(1, 0, 2).reshape(a.shape[1], WIDTH)


def kernel(x, c, w_ada, b_ada, norm1_g, w_in, rwkv_mu, rwkv_w0, rwkv_w2, rwkv_a0, rwkv_a2, rwkv_g2, rwkv_k_k, rwkv_k_a, rwkv_r_k, rwkv_ln_w, rwkv_ln_b, w_branch_a, w_branch_b, w_out, norm2_g, router_w, router_b, exp_w_gate, exp_b_gate, exp_w_up, exp_b_up, exp_w_down, exp_b_down, normf_g):
    bsz, t, d = x.shape
    assert bsz == 1 and t % (MOBA_BLOCK * KV_GROUP) == 0
    x2 = x.reshape(t, d)
    row = lambda a: a.reshape(1, -1)

    mod = _mod(c, w_ada, b_ada)
    shift1, scale1, gate1, shift2, scale2, gate2 = [mod[:, j * d:(j + 1) * d] for j in range(6)]

    qkv_end = RWKV_PROJ + 3 * WIDTH
    w_in_b = w_in.astype(BF16)
    p_rwkv, qkv, sg, kmean = _inproj(x2, row(norm1_g), scale1, shift1,
                              w_in_b[:, :RWKV_PROJ], w_in_b[:, RWKV_PROJ:qkv_end], w_in_b[:, qkv_end:])

    zeros_lora = jnp.zeros((DECAY_LORA, WIDTH), F32)
    w2p = jnp.concatenate([rwkv_w2, zeros_lora], axis=0)
    a2p = jnp.concatenate([zeros_lora, rwkv_a2], axis=0)
    head_of = jnp.arange(WIDTH) // HEAD_DIM
    bd = (head_of[:, None] == head_of[None, :]).astype(F32)
    r, lw, k, v, av, bv, g = _rwkvprep(p_rwkv, row(rwkv_mu), row(rwkv_w0), w2p, row(rwkv_a0), a2p, rwkv_g2,
                                       row(rwkv_k_k), row(rwkv_k_a), bd)
    per_head = lambda a: a.reshape(N_HEADS, 1, HEAD_DIM)
    y_a = _rwkvscan(_heads(r), _heads(lw), _heads(k), _heads(v), _heads(av), _heads(bv), _heads(g),
                    per_head(rwkv_ln_w), per_head(rwkv_ln_b), per_head(rwkv_r_k))
    y_a = _unheads(y_a)

    nb = t // MOBA_BLOCK
    q_hm = _heads(qkv[:, :WIDTH])
    k_hm = _heads(qkv[:, WIDTH:2 * WIDTH])
    v_hm = _heads(qkv[:, 2 * WIDTH:])
    if nb > 1:
        sel = _mobasel(q_hm, kmean.reshape(nb, N_HEADS, HEAD_DIM).transpose(1, 0, 2))
    else:
        sel = jnp.zeros((N_HEADS, nb, t), F32)
    gs = MOBA_BLOCK * KV_GROUP
    ng = t // gs
    qt = (q_hm * (HEAD_DIM ** -0.5)).astype(BF16).transpose(0, 2, 1)
    k4 = k_hm.astype(BF16).reshape(N_HEADS, ng, gs, HEAD_DIM)
    vt4 = v_hm.astype(BF16).reshape(N_HEADS, ng, gs, HEAD_DIM).transpose(0, 1, 3, 2)
    y_b = _mobaattn(qt, k4, vt4, sel.reshape(N_HEADS, ng, KV_GROUP, t))
    y_b = y_b.transpose(2, 0, 1).reshape(t, WIDTH)

    x1, h2, eid, rank, wt, cnt = _mix(y_a, y_b, sg, x2, w_branch_a.astype(BF16), w_branch_b.astype(BF16),
                                      w_out.astype(BF16), gate1, row(norm2_g), scale2, shift2,
                                      router_w, row(router_b))

    counts = cnt[0].astype(jnp.int32)
    tiles_per = (counts + EXPERT_TILE - 1) // EXPERT_TILE
    tile_end = jnp.cumsum(tiles_per)
    off = (tile_end - tiles_per) * EXPERT_TILE
    n_tiles = (t * EXPERT_TOP_K + N_EXPERTS * (EXPERT_TILE - 1)) // EXPERT_TILE
    n_used = tile_end[-1:]
    tile_expert = jnp.minimum(jnp.sum(tile_end[None, :] <= jnp.arange(n_tiles)[:, None], axis=1),
                              N_EXPERTS - 1).astype(jnp.int32)
    last_used = tile_expert[jnp.maximum(n_used[0] - 1, 0)]
    tile_expert = jnp.where(jnp.arange(n_tiles) < n_used[0], tile_expert, last_used)

    eid_flat = eid.reshape(-1)
    rank_flat = rank.reshape(-1)
    xs = _dispatch(eid_flat, rank_flat, off, h2, n_tiles * EXPERT_TILE)
    f = exp_w_gate.shape[2]
    ys = _experts(tile_expert, n_used, xs,
                  exp_w_gate.astype(BF16), exp_b_gate.reshape(N_EXPERTS, 1, f),
                  exp_w_up.astype(BF16), exp_b_up.reshape(N_EXPERTS, 1, f),
                  exp_w_down.astype(BF16), exp_b_down.reshape(N_EXPERTS, 1, d))
    out = _combine(eid_flat, rank_flat, off, ys, wt, x1, gate2, row(normf_g))
    return out.reshape(bsz, t, d)
```

```python
import functools

import jax
import jax.numpy as jnp
from jax import lax
from jax.experimental import pallas as pl
from jax.experimental.pallas import tpu as pltpu

F32 = jnp.float32
BF16 = jnp.bfloat16
HI = lax.Precision.HIGHEST

HEAD_DIM = 64
N_HEADS = 8
WIDTH = N_HEADS * HEAD_DIM
DECAY_LORA = 64
AAA_LORA = 64
GATE_LORA = 128
RWKV_PROJ = 3 * WIDTH + DECAY_LORA + AAA_LORA + GATE_LORA
GN_EPS = 64e-5
MOBA_BLOCK = 256
MOBA_TOP_K = 3
KV_GROUP = 4
BIAS_ROWS = 16
K_AUG = 128
V_AUG = HEAD_DIM + 16
HEADS_PER_STEP = 2
LOG2E = 1.4426950408889634
N_EXPERTS = 32
EXPERT_TOP_K = 4
SWIGLU_LIMIT = 7.0
SWIGLU_ALPHA = 1.702
RMS_EPS = 1e-5
GATE_NEG = -1e30
MASK_NEG = -1e30

SCAN_CHUNK = 64
EXPERT_TILE = 256
VMEM_LIMIT = 56 * 1024 * 1024


def _cparams(*sem):
    return pltpu.CompilerParams(dimension_semantics=sem, vmem_limit_bytes=VMEM_LIMIT)


def _sigmoid(x):
    return 1.0 / (1.0 + jnp.exp(-x))


def _mod_kernel(c_ref, w_ref, b_ref, o_ref):
    c = c_ref[...]
    s = c * _sigmoid(c)
    o_ref[...] = jnp.dot(s, w_ref[...], precision=HI, preferred_element_type=F32) + b_ref[...]


def _mod(c, w_ada, b_ada):
    d = c.shape[-1]
    n = w_ada.shape[1]
    c8 = jnp.broadcast_to(c[:1], (8, d))
    tn = 1024
    out = pl.pallas_call(
        _mod_kernel,
        grid=(n // tn,),
        in_specs=[pl.BlockSpec((8, d), lambda j: (0, 0)),
                  pl.BlockSpec((d, tn), lambda j: (0, j)),
                  pl.BlockSpec((1, tn), lambda j: (0, j))],
        out_specs=pl.BlockSpec((8, tn), lambda j: (0, j)),
        out_shape=jax.ShapeDtypeStruct((8, n), F32),
        compiler_params=_cparams("arbitrary"),
        name="mod",
    )(c8, w_ada, b_ada.reshape(1, n))
    return out[:1]


def _inproj_kernel(x_ref, g_ref, sc_ref, sh_ref, wr_ref, wq_ref, wg_ref, pr_ref, qkv_ref, sg_ref, km_ref):
    x = x_ref[...]
    ms = jnp.mean(x * x, axis=-1, keepdims=True)
    h = x * lax.rsqrt(ms + RMS_EPS) * g_ref[...]
    h = h * (1.0 + sc_ref[...]) + sh_ref[...]
    hb = h.astype(BF16)
    pr_ref[...] = jnp.dot(hb, wr_ref[...], preferred_element_type=F32)
    qkv = jnp.dot(hb, wq_ref[...], preferred_element_type=F32)
    qkv_ref[...] = qkv
    km_ref[0] = jnp.mean(qkv[:, WIDTH:2 * WIDTH], axis=0, keepdims=True)
    sg_ref[...] = _sigmoid(jnp.dot(hb, wg_ref[...], preferred_element_type=F32))


def _inproj(x2, norm_g, scale, shift, w_r, w_q, w_g):
    t, d = x2.shape
    tm = MOBA_BLOCK
    full = lambda a: pl.BlockSpec(a.shape, lambda i: (0, 0))
    row = lambda n: pl.BlockSpec((tm, n), lambda i: (i, 0))
    return pl.pallas_call(
        _inproj_kernel,
        grid=(t // tm,),
        in_specs=[row(d), full(norm_g), full(scale), full(shift), full(w_r), full(w_q), full(w_g)],
        out_specs=[row(w_r.shape[1]), row(w_q.shape[1]), row(w_g.shape[1]),
                   pl.BlockSpec((1, 1, WIDTH), lambda i: (i, 0, 0))],
        out_shape=[jax.ShapeDtypeStruct((t, w_r.shape[1]), F32),
                   jax.ShapeDtypeStruct((t, w_q.shape[1]), F32),
                   jax.ShapeDtypeStruct((t, w_g.shape[1]), F32),
                   jax.ShapeDtypeStruct((t // tm, 1, WIDTH), F32)],
        compiler_params=_cparams("arbitrary"),
        name="inproj",
    )(x2, norm_g, scale, shift, w_r, w_q, w_g)


def _rwkvprep_kernel(p_ref, mu_ref, w0_ref, w2_ref, a0_ref, a2_ref, g2_ref, kk_ref, ka_ref, bd_ref,
                     r_out, lw_out, k_out, v_out, a_out, b_out, g_out, prev_ref):
    i = pl.program_id(0)

    @pl.when(i == 0)
    def _():
        prev_ref[...] = jnp.zeros_like(prev_ref)

    p = p_ref[...]
    tm = p.shape[0]
    rolled = pltpu.roll(p, 1, 0)
    row0 = lax.broadcasted_iota(jnp.int32, p.shape, 0) == 0
    p_prev = jnp.where(row0, prev_ref[0:1, :], rolled)
    prev_ref[0:1, :] = p[tm - 1:tm, :]
    ps = p + (p_prev - p) * mu_ref[...]
    r = ps[:, 0:WIDTH]
    k = ps[:, WIDTH:2 * WIDTH]
    v = ps[:, 2 * WIDTH:3 * WIDTH]
    xwa = ps[:, 3 * WIDTH:3 * WIDTH + DECAY_LORA + AAA_LORA]
    xg = ps[:, 3 * WIDTH + DECAY_LORA + AAA_LORA:RWKV_PROJ]
    dot = functools.partial(jnp.dot, precision=HI, preferred_element_type=F32)
    z = w0_ref[...] + dot(jnp.tanh(xwa), w2_ref[...])
    nz = -z
    softplus = jnp.maximum(nz, 0.0) + jnp.log(1.0 + jnp.exp(-jnp.abs(nz)))
    w = -softplus - 0.5
    a = _sigmoid(a0_ref[...] + dot(xwa, a2_ref[...]))
    g = dot(_sigmoid(xg), g2_ref[...])
    kk = k * kk_ref[...]
    ss = dot(kk * kk, bd_ref[...])
    kk = kk / jnp.maximum(jnp.sqrt(ss), 1e-12)
    r_out[...] = r
    lw_out[...] = -jnp.exp(w)
    k_out[...] = k * (1.0 + (a - 1.0) * ka_ref[...])
    v_out[...] = v
    a_out[...] = -kk
    b_out[...] = kk * a
    g_out[...] = g


def _rwkvprep(p_rwkv, mu, w0, w2p, a0, a2p, g2, k_k, k_a, bd):
    t = p_rwkv.shape[0]
    tm = 256
    full = lambda a: pl.BlockSpec(a.shape, lambda i: (0, 0))
    row = lambda n: pl.BlockSpec((tm, n), lambda i: (i, 0))
    outs = [jax.ShapeDtypeStruct((t, WIDTH), F32)] * 7
    return pl.pallas_call(
        _rwkvprep_kernel,
        grid=(t // tm,),
        in_specs=[row(RWKV_PROJ), full(mu), full(w0), full(w2p), full(a0), full(a2p), full(g2),
                  full(k_k), full(k_a), full(bd)],
        out_specs=[row(WIDTH)] * 7,
        out_shape=outs,
        scratch_shapes=[pltpu.VMEM((8, RWKV_PROJ), F32)],
        compiler_params=_cparams("arbitrary"),
        name="rwkvprep",
    )(p_rwkv, mu, w0, w2p, a0, a2p, g2, k_k, k_a, bd)


def _bmm(a, b):
    return jnp.einsum("hmk,hkn->hmn", a.astype(BF16), b.astype(BF16), preferred_element_type=F32)


def _bmm_nt(a, b):
    return jnp.einsum("hmk,hnk->hmn", a.astype(BF16), b.astype(BF16), preferred_element_type=F32)


def _bmm_tn(a, b):
    return jnp.einsum("hkm,hkn->hmn", a.astype(BF16), b.astype(BF16), preferred_element_type=F32)


def _rwkvscan_kernel(r_ref, lw_ref, k_ref, v_ref, a_ref, b_ref, g_ref, lnw_ref, lnb_ref, rk_ref,
                     y_ref, s_ref):
    i = pl.program_id(0)

    @pl.when(i == 0)
    def _():
        s_ref[...] = jnp.zeros_like(s_ref)

    c = SCAN_CHUNK
    r = r_ref[...]
    lw = lw_ref[...]
    k = k_ref[...]
    v = v_ref[...]
    a = a_ref[...]
    b = b_ref[...]
    s0 = s_ref[...]
    nh = r.shape[0]

    row = lax.broadcasted_iota(jnp.int32, (nh, c, c), 1)
    col = lax.broadcasted_iota(jnp.int32, (nh, c, c), 2)
    lower = row >= col
    strict = row > col
    cum = jnp.einsum("hmk,hkn->hmn", lower.astype(F32), lw, precision=HI,
                     preferred_element_type=F32)
    tot = cum[:, c - 1:c, :]
    g_in = jnp.exp(cum)
    g_ex = jnp.exp(cum - lw)
    g_inv = jnp.exp(-cum)
    g_rem = jnp.exp(tot - cum)
    a_t = a * g_ex
    r_t = r * g_in
    b_t = b * g_inv
    k_t = k * g_inv
    l_ab = jnp.where(strict, _bmm_nt(a_t, b_t), 0.0)
    l_ak = jnp.where(strict, _bmm_nt(a_t, k_t), 0.0)
    m_rb = jnp.where(lower, _bmm_nt(r_t, b_t), 0.0)
    m_rk = jnp.where(lower, _bmm_nt(r_t, k_t), 0.0)
    same16 = (row // 16) == (col // 16)
    same32 = (row // 32) == (col // 32)
    diag16 = jnp.where(same16, l_ab, 0.0)
    inv = jnp.where(row == col, 1.0, 0.0) + diag16
    lp = diag16
    for _ in range(3):
        lp = _bmm(lp, lp)
        inv = inv + _bmm(inv, lp)
    off32 = jnp.where(jnp.logical_and(same32, jnp.logical_not(same16)), l_ab, 0.0)
    inv = inv + _bmm(_bmm(inv, off32), inv)
    off64 = jnp.where(same32, 0.0, l_ab)
    inv = inv + _bmm(_bmm(inv, off64), inv)
    x = _bmm_nt(a_t, s0) + _bmm(l_ak, v)
    u = _bmm(inv, x)
    y = _bmm_nt(r_t, s0) + _bmm(m_rb, u) + _bmm(m_rk, v)
    s_ref[...] = s0 * jnp.exp(tot) + _bmm_tn(u, b * g_rem) + _bmm_tn(v, k * g_rem)

    mean = jnp.mean(y, axis=-1, keepdims=True)
    yc = y - mean
    var = jnp.mean(yc * yc, axis=-1, keepdims=True)
    yn = yc * lax.rsqrt(var + GN_EPS) * lnw_ref[...] + lnb_ref[...]
    bonus = jnp.sum(r * k * rk_ref[...], axis=-1, keepdims=True) * v
    y_ref[...] = (yn + bonus) * g_ref[...]


def _rwkvscan(r, lw, k, v, a, b, g, ln_w, ln_b, r_k):
    nh, t, n = r.shape
    c = SCAN_CHUNK
    blk = pl.BlockSpec((nh, c, n), lambda i: (0, i, 0))
    par = pl.BlockSpec((nh, 1, n), lambda i: (0, 0, 0))
    return pl.pallas_call(
        _rwkvscan_kernel,
        grid=(t // c,),
        in_specs=[blk] * 7 + [par] * 3,
        out_specs=blk,
        out_shape=jax.ShapeDtypeStruct((nh, t, n), F32),
        scratch_shapes=[pltpu.VMEM((nh, n, n), F32)],
        compiler_params=_cparams("arbitrary"),
        name="rwkvscan",
    )(r, lw, k, v, a, b, g, ln_w, ln_b, r_k)


def _mobasel_kernel(q_ref, km_ref, o_ref):
    j = pl.program_id(1)
    q = q_ref[0]
    km = km_ref[0]
    nb = km.shape[0]
    tq = q.shape[0]
    gate = lax.dot_general(km, q, (((1,), (1,)), ((), ())), precision=HI, preferred_element_type=F32)
    n_id = lax.broadcasted_iota(jnp.int32, (nb, tq), 0)
    t_id = lax.broadcasted_iota(jnp.int32, (nb, tq), 1) + j * tq
    q_blk = t_id // MOBA_BLOCK
    cand = n_id < nb - 1
    past = n_id < q_blk
    neg_inf = jnp.float32(-jnp.inf)
    gate = jnp.where(cand, jnp.where(past, gate, GATE_NEG), neg_inf)
    sel = jnp.zeros((nb, tq), F32)
    for _ in range(min(MOBA_TOP_K, nb - 1)):
        m = jnp.max(gate, axis=0, keepdims=True)
        idx = jnp.min(jnp.where(gate == m, n_id, nb), axis=0, keepdims=True)
        pick = n_id == idx
        sel = jnp.where(pick, 1.0, sel)
        gate = jnp.where(pick, neg_inf, gate)
    bias = jnp.where(jnp.logical_or(jnp.logical_and(past, sel > 0.5), n_id == q_blk), 0.0, MASK_NEG)
    pad = jnp.zeros((BIAS_ROWS - KV_GROUP, tq), F32)
    for g in range(nb // KV_GROUP):
        o_ref[0, g] = jnp.concatenate([bias[g * KV_GROUP:(g + 1) * KV_GROUP], pad], axis=0).astype(BF16)


def _mobasel(q_hm, kmean):
    nh, t, n = q_hm.shape
    nb = kmean.shape[1]
    ng = nb // KV_GROUP
    tq = min(t, 2048)
    return pl.pallas_call(
        _mobasel_kernel,
        grid=(nh, t // tq),
        in_specs=[pl.BlockSpec((1, tq, n), lambda h, j: (h, j, 0)),
                  pl.BlockSpec((1, nb, n), lambda h, j: (h, 0, 0))],
        out_specs=pl.BlockSpec((1, ng, BIAS_ROWS, tq), lambda h, j: (h, 0, 0, j)),
        out_shape=jax.ShapeDtypeStruct((nh, ng, BIAS_ROWS, t), BF16),
        compiler_params=_cparams("arbitrary", "arbitrary"),
        name="mobasel",
    )(q_hm, kmean)


def _mobaattn_kernel(qt_ref, k_ref, vt_ref, bias_ref, o_ref, sa_ref, sb_ref, m_ref, acc_ref):
    i = pl.program_id(1)
    bs = MOBA_BLOCK
    gd = i // KV_GROUP
    q_pad = jnp.zeros((K_AUG - HEAD_DIM - BIAS_ROWS, bs), BF16)
    heads = range(HEADS_PER_STEP)

    def scores_to(dst, g):
        for hh in heads:
            q_aug = jnp.concatenate([qt_ref[hh], bias_ref[hh, g], q_pad], axis=0)
            dst[hh] = jnp.dot(k_ref[hh, g], q_aug, preferred_element_type=F32)

    def update_from(src, g, keep=None):
        for hh in heads:
            s = src[hh] if keep is None else jnp.where(keep, src[hh], MASK_NEG)
            m = m_ref[hh]
            m_new = jnp.maximum(m, jnp.max(s, axis=0, keepdims=True))
            alpha = jnp.exp2(m - m_new)
            p = jnp.exp2((s - m_new).astype(BF16))
            acc_ref[hh] = alpha * acc_ref[hh] + jnp.dot(vt_ref[hh, g], p, preferred_element_type=F32)
            m_ref[hh] = m_new

    def finish(src):
        causal = lax.broadcasted_iota(jnp.int32, (bs, bs), 0) <= lax.broadcasted_iota(jnp.int32, (bs, bs), 1)
        own = jnp.concatenate([jnp.logical_or(causal, i % KV_GROUP != b) for b in range(KV_GROUP)], axis=0)
        update_from(src, gd, own)
        for hh in heads:
            acc = acc_ref[hh]
            o_ref[hh] = acc[:HEAD_DIM] / acc[HEAD_DIM:HEAD_DIM + 1]

    m_ref[...] = jnp.full(m_ref.shape, MASK_NEG, F32)
    acc_ref[...] = jnp.zeros(acc_ref.shape, F32)
    scores_to(sa_ref, 0)

    def pair(k, _):
        g = 2 * k
        scores_to(sb_ref, g + 1)
        update_from(sa_ref, g)
        scores_to(sa_ref, g + 2)
        update_from(sb_ref, g + 1)
        return 0

    lax.fori_loop(0, gd // 2, pair, 0)

    @pl.when(gd % 2 == 1)
    def _():
        scores_to(sb_ref, gd)
        update_from(sa_ref, gd - 1)
        finish(sb_ref)

    @pl.when(gd % 2 == 0)
    def _():
        finish(sa_ref)


def _mobaattn(qt, k_aug, vt_aug, bias):
    nh, n, t = qt.shape
    gs = MOBA_BLOCK * KV_GROUP
    ng = t // gs
    hp = HEADS_PER_STEP
    return pl.pallas_call(
        _mobaattn_kernel,
        grid=(nh // hp, t // MOBA_BLOCK),
        in_specs=[pl.BlockSpec((hp, n, MOBA_BLOCK), lambda h, i: (h, 0, i)),
                  pl.BlockSpec((hp, ng, gs, K_AUG), lambda h, i: (h, 0, 0, 0)),
                  pl.BlockSpec((hp, ng, V_AUG, gs), lambda h, i: (h, 0, 0, 0)),
                  pl.BlockSpec((hp, ng, BIAS_ROWS, MOBA_BLOCK), lambda h, i: (h, 0, 0, i))],
        out_specs=pl.BlockSpec((hp, n, MOBA_BLOCK), lambda h, i: (h, 0, i)),
        out_shape=jax.ShapeDtypeStruct((nh, n, t), F32),
        scratch_shapes=[pltpu.VMEM((hp, gs, MOBA_BLOCK), F32), pltpu.VMEM((hp, gs, MOBA_BLOCK), F32),
                        pltpu.VMEM((hp, 1, MOBA_BLOCK), F32), pltpu.VMEM((hp, V_AUG, MOBA_BLOCK), F32)],
        compiler_params=_cparams("arbitrary", "arbitrary"),
        name="mobaattn",
    )(qt, k_aug, vt_aug, bias)


def _mix_kernel(ya_ref, yb_ref, sg_ref, x_ref, wa_ref, wb_ref, wo_ref, g1_ref, n2_ref, sc_ref, sh_ref,
                rw_ref, rb_ref, x1_ref, h2_ref, eid_ref, rank_ref, wt_ref, cnt_ref, base_ref):
    i = pl.program_id(0)

    @pl.when(i == 0)
    def _():
        base_ref[...] = jnp.zeros_like(base_ref)

    d = x_ref.shape[1]
    tm = x_ref.shape[0]
    pa = jnp.dot(ya_ref[...].astype(BF16), wa_ref[...], preferred_element_type=F32)
    pb = jnp.dot(yb_ref[...].astype(BF16), wb_ref[...], preferred_element_type=F32)
    sg = sg_ref[...]
    mixed = sg[:, :d] * pa + sg[:, d:] * pb
    mixed = jnp.dot(mixed.astype(BF16), wo_ref[...], preferred_element_type=F32)
    x1 = x_ref[...] + g1_ref[...] * mixed
    x1_ref[...] = x1
    ms = jnp.mean(x1 * x1, axis=-1, keepdims=True)
    h2 = x1 * lax.rsqrt(ms + RMS_EPS) * n2_ref[...]
    h2 = h2 * (1.0 + sc_ref[...]) + sh_ref[...]
    h2_ref[...] = h2

    logits = jnp.dot(h2, rw_ref[...], precision=HI, preferred_element_type=F32) + rb_ref[...]
    ne = logits.shape[1]
    e_id = lax.broadcasted_iota(jnp.int32, (tm, ne), 1)
    neg_inf = jnp.float32(-jnp.inf)
    work = logits
    picks, vals = [], []
    for _ in range(EXPERT_TOP_K):
        m = jnp.max(work, axis=-1, keepdims=True)
        idx = jnp.min(jnp.where(work == m, e_id, ne), axis=-1, keepdims=True)
        pick = e_id == idx
        picks.append((idx, pick))
        vals.append(m)
        work = jnp.where(pick, neg_inf, work)
    den = sum(jnp.exp(v - vals[0]) for v in vals)
    chosen = jnp.zeros((tm, ne), F32)
    for _, pick in picks:
        chosen = jnp.where(pick, 1.0, chosen)
    r_id = lax.broadcasted_iota(jnp.int32, (tm, tm), 0)
    c_id = lax.broadcasted_iota(jnp.int32, (tm, tm), 1)
    before = jnp.where(r_id > c_id, 1.0, 0.0).astype(BF16)
    ahead = jnp.dot(before, chosen.astype(BF16), preferred_element_type=F32) + base_ref[0:1, :]
    k_id = lax.broadcasted_iota(jnp.int32, (tm, EXPERT_TOP_K), 1)
    eid = jnp.zeros((tm, EXPERT_TOP_K), jnp.int32)
    rank = jnp.zeros((tm, EXPERT_TOP_K), jnp.int32)
    wt = jnp.zeros((tm, EXPERT_TOP_K), F32)
    for kk, ((idx, pick), v) in enumerate(zip(picks, vals)):
        rk = jnp.sum(jnp.where(pick, ahead, 0.0), axis=-1, keepdims=True).astype(jnp.int32)
        eid = jnp.where(k_id == kk, idx, eid)
        rank = jnp.where(k_id == kk, rk, rank)
        wt = jnp.where(k_id == kk, jnp.exp(v - vals[0]) / den, wt)
    eid_ref[...] = eid
    rank_ref[...] = rank
    wt_ref[...] = wt
    total = base_ref[0:1, :] + jnp.sum(chosen, axis=0, keepdims=True)
    base_ref[0:1, :] = total
    cnt_ref[...] = jnp.broadcast_to(total, cnt_ref.shape)


def _mix(ya, yb, sg, x2, w_a, w_b, w_o, gate1, norm2_g, scale2, shift2, router_w, router_b):
    t, d = x2.shape
    tm = 256
    ne = router_w.shape[1]
    full = lambda a: pl.BlockSpec(a.shape, lambda i: (0, 0))
    row = lambda n: pl.BlockSpec((tm, n), lambda i: (i, 0))
    return pl.pallas_call(
        _mix_kernel,
        grid=(t // tm,),
        in_specs=[row(WIDTH), row(WIDTH), row(2 * d), row(d), full(w_a), full(w_b), full(w_o),
                  full(gate1), full(norm2_g), full(scale2), full(shift2), full(router_w), full(router_b)],
        out_specs=[row(d), row(d), row(EXPERT_TOP_K), row(EXPERT_TOP_K), row(EXPERT_TOP_K),
                   pl.BlockSpec((8, ne), lambda i: (0, 0))],
        out_shape=[jax.ShapeDtypeStruct((t, d), F32), jax.ShapeDtypeStruct((t, d), F32),
                   jax.ShapeDtypeStruct((t, EXPERT_TOP_K), jnp.int32),
                   jax.ShapeDtypeStruct((t, EXPERT_TOP_K), jnp.int32),
                   jax.ShapeDtypeStruct((t, EXPERT_TOP_K), F32),
                   jax.ShapeDtypeStruct((8, ne), F32)],
        scratch_shapes=[pltpu.VMEM((8, ne), F32)],
        compiler_params=_cparams("arbitrary"),
        name="mix",
    )(ya, yb, sg, x2, w_a, w_b, w_o, gate1, norm2_g, scale2, shift2, router_w, router_b)


DISPATCH_TILE = 256


def _dispatch_kernel(eid_ref, rank_ref, off_ref, h_ref, xs_in_ref, xs_ref, sem):
    del xs_in_ref
    tm = DISPATCH_TILE

    def body(t, _):
        for kk in range(EXPERT_TOP_K):
            q = t * EXPERT_TOP_K + kk
            slot = off_ref[eid_ref[q]] + rank_ref[q]
            pltpu.make_async_copy(h_ref.at[pl.ds(t, 1)], xs_ref.at[pl.ds(slot, 1)], sem).start()
        return 0

    lax.fori_loop(0, tm, body, 0)
    for _ in range(EXPERT_TOP_K):
        pltpu.make_async_copy(h_ref, xs_ref.at[pl.ds(0, tm)], sem).wait()


def _dispatch(eid_flat, rank_flat, off, h2, n_slots):
    t, d = h2.shape
    tm = DISPATCH_TILE
    smem_blk = pl.BlockSpec((tm * EXPERT_TOP_K,), lambda i: (i,), memory_space=pltpu.SMEM)
    return pl.pallas_call(
        _dispatch_kernel,
        grid=(t // tm,),
        in_specs=[smem_blk, smem_blk,
                  pl.BlockSpec(memory_space=pltpu.SMEM),
                  pl.BlockSpec((tm, d), lambda i: (i, 0)),
                  pl.BlockSpec(memory_space=pl.ANY)],
        out_specs=pl.BlockSpec(memory_space=pl.ANY),
        out_shape=jax.ShapeDtypeStruct((n_slots, d), F32),
        scratch_shapes=[pltpu.SemaphoreType.DMA(())],
        input_output_aliases={4: 0},
        compiler_params=pltpu.CompilerParams(dimension_semantics=("arbitrary",), has_side_effects=True),
        name="dispatch",
    )(eid_flat, rank_flat, off, h2, jnp.zeros((n_slots, d), F32))


def _experts_kernel(te_ref, nu_ref, x_ref, wg_ref, bg_ref, wu_ref, bu_ref, wd_ref, bd_ref, y_ref):
    i = pl.program_id(0)

    @pl.when(i < nu_ref[0])
    def _():
        x = x_ref[...].astype(BF16)
        gt = jnp.minimum(jnp.dot(x, wg_ref[0], preferred_element_type=F32) + bg_ref[0], SWIGLU_LIMIT)
        up = jnp.clip(jnp.dot(x, wu_ref[0], preferred_element_type=F32) + bu_ref[0], -SWIGLU_LIMIT, SWIGLU_LIMIT)
        hid = (up + 1.0) * gt * _sigmoid(SWIGLU_ALPHA * gt)
        y_ref[...] = jnp.dot(hid.astype(BF16), wd_ref[0], preferred_element_type=F32) + bd_ref[0]

    @pl.when(i >= nu_ref[0])
    def _():
        y_ref[...] = jnp.zeros_like(y_ref)


def _experts(tile_expert, n_used, xs, w_gate, b_gate, w_up, b_up, w_down, b_down):
    n_slots, d = xs.shape
    f = w_gate.shape[2]
    tm = EXPERT_TILE
    n_tiles = n_slots // tm
    row = pl.BlockSpec((tm, d), lambda i, te, nu: (i, 0))
    wspec = lambda a, b: pl.BlockSpec((1, a, b), lambda i, te, nu: (te[i], 0, 0))
    grid_spec = pltpu.PrefetchScalarGridSpec(
        num_scalar_prefetch=2,
        grid=(n_tiles,),
        in_specs=[row, wspec(d, f), wspec(1, f), wspec(d, f), wspec(1, f), wspec(f, d), wspec(1, d)],
        out_specs=row,
    )
    return pl.pallas_call(
        _experts_kernel,
        grid_spec=grid_spec,
        out_shape=jax.ShapeDtypeStruct((n_slots, d), F32),
        compiler_params=_cparams("arbitrary"),
        name="experts",
    )(tile_expert, n_used, xs, w_gate, b_gate, w_up, b_up, w_down, b_down)


def _combine_kernel(eid_ref, rank_ref, off_ref, ys_ref, wt_ref, x1_ref, g2_ref, nf_ref, o_ref, buf, sem):
    tm = DISPATCH_TILE

    def body(t, _):
        for kk in range(EXPERT_TOP_K):
            q = t * EXPERT_TOP_K + kk
            slot = off_ref[eid_ref[q]] + rank_ref[q]
            pltpu.make_async_copy(ys_ref.at[pl.ds(slot, 1)], buf.at[kk, pl.ds(t, 1)], sem).start()
        return 0

    lax.fori_loop(0, tm, body, 0)
    for kk in range(EXPERT_TOP_K):
        pltpu.make_async_copy(ys_ref.at[pl.ds(0, tm)], buf.at[kk], sem).wait()
    wt = wt_ref[...]
    moe = wt[:, 0:1] * buf[0]
    for kk in range(1, EXPERT_TOP_K):
        moe = moe + wt[:, kk:kk + 1] * buf[kk]
    x = x1_ref[...] + g2_ref[...] * moe
    ms = jnp.mean(x * x, axis=-1, keepdims=True)
    o_ref[...] = x * lax.rsqrt(ms + RMS_EPS) * nf_ref[...]


def _combine(eid_flat, rank_flat, off, ys, wt, x1, gate2, normf_g):
    t, d = x1.shape
    tm = DISPATCH_TILE
    smem_blk = pl.BlockSpec((tm * EXPERT_TOP_K,), lambda i: (i,), memory_space=pltpu.SMEM)
    full = lambda a: pl.BlockSpec(a.shape, lambda i: (0, 0))
    row = lambda n: pl.BlockSpec((tm, n), lambda i: (i, 0))
    return pl.pallas_call(
        _combine_kernel,
        grid=(t // tm,),
        in_specs=[smem_blk, smem_blk,
                  pl.BlockSpec(memory_space=pltpu.SMEM),
                  pl.BlockSpec(memory_space=pl.ANY),
                  row(EXPERT_TOP_K), row(d), full(gate2), full(normf_g)],
        out_specs=row(d),
        out_shape=jax.ShapeDtypeStruct((t, d), F32),
        scratch_shapes=[pltpu.VMEM((EXPERT_TOP_K, tm, d), F32), pltpu.SemaphoreType.DMA(())],
        compiler_params=_cparams("arbitrary"),
        name="combine",
    )(eid_flat, rank_flat, off, ys, wt, x1, gate2, normf_g)


def _heads(a):
    t = a.shape[0]
    return a.reshape(t, N_HEADS, HEAD_DIM).transpose(1, 0, 2)


def _unheads(a):
    return a.transpose(1, 0, 2).reshape(a.shape[1], WIDTH)


def kernel(x, c, w_ada, b_ada, norm1_g, w_in, rwkv_mu, rwkv_w0, rwkv_w2, rwkv_a0, rwkv_a2, rwkv_g2, rwkv_k_k, rwkv_k_a, rwkv_r_k, rwkv_ln_w, rwkv_ln_b, w_branch_a, w_branch_b, w_out, norm2_g, router_w, router_b, exp_w_gate, exp_b_gate, exp_w_up, exp_b_up, exp_w_down, exp_b_down, normf_g):
    bsz, t, d = x.shape
    assert bsz == 1 and t % (MOBA_BLOCK * KV_GROUP) == 0
    x2 = x.reshape(t, d)
    row = lambda a: a.reshape(1, -1)

    mod = _mod(c, w_ada, b_ada)
    shift1, scale1, gate1, shift2, scale2, gate2 = [mod[:, j * d:(j + 1) * d] for j in range(6)]

    qkv_end = RWKV_PROJ + 3 * WIDTH
    w_in_b = w_in.astype(BF16)
    p_rwkv, qkv, sg, kmean = _inproj(x2, row(norm1_g), scale1, shift1,
                              w_in_b[:, :RWKV_PROJ], w_in_b[:, RWKV_PROJ:qkv_end], w_in_b[:, qkv_end:])

    zeros_lora = jnp.zeros((DECAY_LORA, WIDTH), F32)
    w2p = jnp.concatenate([rwkv_w2, zeros_lora], axis=0)
    a2p = jnp.concatenate([zeros_lora, rwkv_a2], axis=0)
    head_of = jnp.arange(WIDTH) // HEAD_DIM
    bd = (head_of[:, None] == head_of[None, :]).astype(F32)
    r, lw, k, v, av, bv, g = _rwkvprep(p_rwkv, row(rwkv_mu), row(rwkv_w0), w2p, row(rwkv_a0), a2p, rwkv_g2,
                                       row(rwkv_k_k), row(rwkv_k_a), bd)
    per_head = lambda a: a.reshape(N_HEADS, 1, HEAD_DIM)
    y_a = _rwkvscan(_heads(r), _heads(lw), _heads(k), _heads(v), _heads(av), _heads(bv), _heads(g),
                    per_head(rwkv_ln_w), per_head(rwkv_ln_b), per_head(rwkv_r_k))
    y_a = _unheads(y_a)

    nb = t // MOBA_BLOCK
    q_hm = _heads(qkv[:, :WIDTH])
    k_hm = _heads(qkv[:, WIDTH:2 * WIDTH])
    v_hm = _heads(qkv[:, 2 * WIDTH:])
    gs = MOBA_BLOCK * KV_GROUP
    ng = t // gs
    bias = _mobasel(q_hm, kmean.reshape(nb, N_HEADS, HEAD_DIM).transpose(1, 0, 2))
    qt = (q_hm * (HEAD_DIM ** -0.5 * LOG2E)).astype(BF16).transpose(0, 2, 1)
    blk_in_group = (jnp.arange(t) % gs) // MOBA_BLOCK
    indicator = (blk_in_group[:, None] == jnp.arange(K_AUG - HEAD_DIM)[None, :]).astype(BF16)
    k_aug = jnp.concatenate([k_hm.astype(BF16), jnp.broadcast_to(indicator, (N_HEADS, t, K_AUG - HEAD_DIM))],
                            axis=2).reshape(N_HEADS, ng, gs, K_AUG)
    ones_row = (jnp.arange(V_AUG - HEAD_DIM) == 0).astype(BF16)
    vt_aug = jnp.concatenate([v_hm.astype(BF16).transpose(0, 2, 1),
                              jnp.broadcast_to(ones_row[None, :, None], (N_HEADS, V_AUG - HEAD_DIM, t))], axis=1)
    vt_aug = vt_aug.reshape(N_HEADS, V_AUG, ng, gs).transpose(0, 2, 1, 3)
    y_b = _mobaattn(qt, k_aug, vt_aug, bias)
    y_b = y_b.transpose(2, 0, 1).reshape(t, WIDTH)

    x1, h2, eid, rank, wt, cnt = _mix(y_a, y_b, sg, x2, w_branch_a.astype(BF16), w_branch_b.astype(BF16),
                                      w_out.astype(BF16), gate1, row(norm2_g), scale2, shift2,
                                      router_w, row(router_b))

    counts = cnt[0].astype(jnp.int32)
    tiles_per = (counts + EXPERT_TILE - 1) // EXPERT_TILE
    tile_end = jnp.cumsum(tiles_per)
    off = (tile_end - tiles_per) * EXPERT_TILE
    n_tiles = (t * EXPERT_TOP_K + N_EXPERTS * (EXPERT_TILE - 1)) // EXPERT_TILE
    n_used = tile_end[-1:]
    tile_expert = jnp.minimum(jnp.sum(tile_end[None, :] <= jnp.arange(n_tiles)[:, None], axis=1),
                              N_EXPERTS - 1).astype(jnp.int32)
    last_used = tile_expert[jnp.maximum(n_used[0] - 1, 0)]
    tile_expert = jnp.where(jnp.arange(n_tiles) < n_used[0], tile_expert, last_used)

    eid_flat = eid.reshape(-1)
    rank_flat = rank.reshape(-1)
    xs = _dispatch(eid_flat, rank_flat, off, h2, n_tiles * EXPERT_TILE)
    f = exp_w_gate.shape[2]
    ys = _experts(tile_expert, n_used, xs,
                  exp_w_gate.astype(BF16), exp_b_gate.reshape(N_EXPERTS, 1, f),
                  exp_w_up.astype(BF16), exp_b_up.reshape(N_EXPERTS, 1, f),
                  exp_w_down.astype(BF16), exp_b_down.reshape(N_EXPERTS, 1, d))
    out = _combine(eid_flat, rank_flat, off, ys, wt, x1, gate2, row(normf_g))
    return out.reshape(bsz, t, d)
```

```python
import functools

import jax
import jax.numpy as jnp
from jax import lax
from jax.experimental import pallas as pl
from jax.experimental.pallas import tpu as pltpu

F32 = jnp.float32
BF16 = jnp.bfloat16
HI = lax.Precision.HIGHEST

HEAD_DIM = 64
N_HEADS = 8
WIDTH = N_HEADS * HEAD_DIM
DECAY_LORA = 64
AAA_LORA = 64
GATE_LORA = 128
RWKV_PROJ = 3 * WIDTH + DECAY_LORA + AAA_LORA + GATE_LORA
GN_EPS = 64e-5
MOBA_BLOCK = 256
MOBA_TOP_K = 3
KV_GROUP = 4
BIAS_ROWS = 16
K_AUG = 128
V_AUG = HEAD_DIM + 16
HEADS_PER_STEP = 2
LOG2E = 1.4426950408889634
N_EXPERTS = 32
EXPERT_TOP_K = 4
SWIGLU_LIMIT = 7.0
SWIGLU_ALPHA = 1.702
RMS_EPS = 1e-5
GATE_NEG = -1e30
MASK_NEG = -1e30

SCAN_CHUNK = 64
SCAN_CHUNKS = 2
EXPERT_TILE = 256
VMEM_LIMIT = 56 * 1024 * 1024


def _cparams(*sem):
    return pltpu.CompilerParams(dimension_semantics=sem, vmem_limit_bytes=VMEM_LIMIT)


def _sigmoid(x):
    return 1.0 / (1.0 + jnp.exp(-x))


def _mod_kernel(c_ref, w_ref, b_ref, o_ref):
    c = c_ref[...]
    s = c * _sigmoid(c)
    o_ref[...] = jnp.dot(s, w_ref[...], precision=HI, preferred_element_type=F32) + b_ref[...]


def _mod(c, w_ada, b_ada):
    d = c.shape[-1]
    n = w_ada.shape[1]
    c8 = jnp.broadcast_to(c[:1], (8, d))
    tn = 1024
    out = pl.pallas_call(
        _mod_kernel,
        grid=(n // tn,),
        in_specs=[pl.BlockSpec((8, d), lambda j: (0, 0)),
                  pl.BlockSpec((d, tn), lambda j: (0, j)),
                  pl.BlockSpec((1, tn), lambda j: (0, j))],
        out_specs=pl.BlockSpec((8, tn), lambda j: (0, j)),
        out_shape=jax.ShapeDtypeStruct((8, n), F32),
        compiler_params=_cparams("arbitrary"),
        name="mod",
    )(c8, w_ada, b_ada.reshape(1, n))
    return out[:1]


def _inproj_kernel(x_ref, g_ref, sc_ref, sh_ref, wr_ref, wq_ref, wg_ref, pr_ref, qkv_ref, sg_ref, km_ref):
    x = x_ref[...]
    ms = jnp.mean(x * x, axis=-1, keepdims=True)
    h = x * lax.rsqrt(ms + RMS_EPS) * g_ref[...]
    h = h * (1.0 + sc_ref[...]) + sh_ref[...]
    hb = h.astype(BF16)
    pr_ref[...] = jnp.dot(hb, wr_ref[...], preferred_element_type=F32)
    qkv = jnp.dot(hb, wq_ref[...], preferred_element_type=F32)
    qkv_ref[...] = qkv
    km_ref[0] = jnp.mean(qkv[:, WIDTH:2 * WIDTH], axis=0, keepdims=True)
    sg_ref[...] = _sigmoid(jnp.dot(hb, wg_ref[...], preferred_element_type=F32))


def _inproj(x2, norm_g, scale, shift, w_r, w_q, w_g):
    t, d = x2.shape
    tm = MOBA_BLOCK
    full = lambda a: pl.BlockSpec(a.shape, lambda i: (0, 0))
    row = lambda n: pl.BlockSpec((tm, n), lambda i: (i, 0))
    return pl.pallas_call(
        _inproj_kernel,
        grid=(t // tm,),
        in_specs=[row(d), full(norm_g), full(scale), full(shift), full(w_r), full(w_q), full(w_g)],
        out_specs=[row(w_r.shape[1]), row(w_q.shape[1]), row(w_g.shape[1]),
                   pl.BlockSpec((1, 1, WIDTH), lambda i: (i, 0, 0))],
        out_shape=[jax.ShapeDtypeStruct((t, w_r.shape[1]), F32),
                   jax.ShapeDtypeStruct((t, w_q.shape[1]), F32),
                   jax.ShapeDtypeStruct((t, w_g.shape[1]), F32),
                   jax.ShapeDtypeStruct((t // tm, 1, WIDTH), F32)],
        compiler_params=_cparams("arbitrary"),
        name="inproj",
    )(x2, norm_g, scale, shift, w_r, w_q, w_g)


def _rwkvprep_kernel(p_ref, mu_ref, w0_ref, w2_ref, a0_ref, a2_ref, g2_ref, kk_ref, ka_ref, bd_ref,
                     r_out, lw_out, k_out, v_out, a_out, b_out, g_out, prev_ref):
    i = pl.program_id(0)

    @pl.when(i == 0)
    def _():
        prev_ref[...] = jnp.zeros_like(prev_ref)

    p = p_ref[...]
    tm = p.shape[0]
    rolled = pltpu.roll(p, 1, 0)
    row0 = lax.broadcasted_iota(jnp.int32, p.shape, 0) == 0
    p_prev = jnp.where(row0, prev_ref[0:1, :], rolled)
    prev_ref[0:1, :] = p[tm - 1:tm, :]
    ps = p + (p_prev - p) * mu_ref[...]
    r = ps[:, 0:WIDTH]
    k = ps[:, WIDTH:2 * WIDTH]
    v = ps[:, 2 * WIDTH:3 * WIDTH]
    xwa = ps[:, 3 * WIDTH:3 * WIDTH + DECAY_LORA + AAA_LORA]
    xg = ps[:, 3 * WIDTH + DECAY_LORA + AAA_LORA:RWKV_PROJ]
    dot = lambda lhs, w_ref: jnp.dot(lhs.astype(BF16), w_ref[...], preferred_element_type=F32)
    z = w0_ref[...] + dot(jnp.tanh(xwa), w2_ref)
    nz = -z
    softplus = jnp.maximum(nz, 0.0) + jnp.log(1.0 + jnp.exp(-jnp.abs(nz)))
    w = -softplus - 0.5
    a = _sigmoid(a0_ref[...] + dot(xwa, a2_ref))
    g = dot(_sigmoid(xg), g2_ref)
    kk = k * kk_ref[...]
    sq = kk * kk
    sq_hi = sq.astype(BF16)
    ss = dot(sq_hi, bd_ref) + dot(sq - sq_hi.astype(F32), bd_ref)
    kk = kk / jnp.maximum(jnp.sqrt(ss), 1e-12)
    r_out[...] = r
    lw_out[...] = -jnp.exp(w)
    k_out[...] = k * (1.0 + (a - 1.0) * ka_ref[...])
    v_out[...] = v
    a_out[...] = -kk
    b_out[...] = kk * a
    g_out[...] = g


def _rwkvprep(p_rwkv, mu, w0, w2p, a0, a2p, g2, k_k, k_a, bd):
    t = p_rwkv.shape[0]
    tm = 256
    full = lambda a: pl.BlockSpec(a.shape, lambda i: (0, 0))
    row = lambda n: pl.BlockSpec((tm, n), lambda i: (i, 0))
    outs = [jax.ShapeDtypeStruct((t, WIDTH), F32)] * 7
    return pl.pallas_call(
        _rwkvprep_kernel,
        grid=(t // tm,),
        in_specs=[row(RWKV_PROJ), full(mu), full(w0), full(w2p), full(a0), full(a2p), full(g2),
                  full(k_k), full(k_a), full(bd)],
        out_specs=[row(WIDTH)] * 7,
        out_shape=outs,
        scratch_shapes=[pltpu.VMEM((8, RWKV_PROJ), F32)],
        compiler_params=_cparams("arbitrary"),
        name="rwkvprep",
    )(p_rwkv, mu, w0, w2p, a0, a2p, g2, k_k, k_a, bd)


def _bmm(a, b):
    return jnp.einsum("hmk,hkn->hmn", a.astype(BF16), b.astype(BF16), preferred_element_type=F32)


def _bmm_nt(a, b):
    return jnp.einsum("hmk,hnk->hmn", a.astype(BF16), b.astype(BF16), preferred_element_type=F32)


def _bmm_tn(a, b):
    return jnp.einsum("hkm,hkn->hmn", a.astype(BF16), b.astype(BF16), preferred_element_type=F32)


def _rwkvscan_kernel(r_ref, lw_ref, k_ref, v_ref, a_ref, b_ref, g_ref, lnw_ref, lnb_ref, rk_ref,
                     y_ref, s_ref):
    i = pl.program_id(0)

    @pl.when(i == 0)
    def _():
        s_ref[...] = jnp.zeros_like(s_ref)

    c = SCAN_CHUNK
    nc = SCAN_CHUNKS
    nh, _, n = r_ref.shape
    nb = nh * nc
    split = lambda ref: ref[...].reshape(nb, c, n)
    r, lw, k, v, a, b = (split(ref) for ref in (r_ref, lw_ref, k_ref, v_ref, a_ref, b_ref))

    row = lax.broadcasted_iota(jnp.int32, (nb, c, c), 1)
    col = lax.broadcasted_iota(jnp.int32, (nb, c, c), 2)
    lower = row >= col
    strict = row > col
    cum = jnp.einsum("hmk,hkn->hmn", lower.astype(F32), lw, precision=HI,
                     preferred_element_type=F32)
    tot = cum[:, c - 1:c, :]
    g_in = jnp.exp(cum)
    g_ex = jnp.exp(cum - lw)
    g_inv = jnp.exp(-cum)
    g_rem = jnp.exp(tot - cum)
    a_t = a * g_ex
    r_t = r * g_in
    b_t = b * g_inv
    k_t = k * g_inv
    l_ab = jnp.where(strict, _bmm_nt(a_t, b_t), 0.0)
    l_ak = jnp.where(strict, _bmm_nt(a_t, k_t), 0.0)
    m_rb = jnp.where(lower, _bmm_nt(r_t, b_t), 0.0)
    m_rk = jnp.where(lower, _bmm_nt(r_t, k_t), 0.0)
    same16 = (row // 16) == (col // 16)
    same32 = (row // 32) == (col // 32)
    diag16 = jnp.where(same16, l_ab, 0.0)
    inv = jnp.where(row == col, 1.0, 0.0) + diag16
    lp = diag16
    for _ in range(3):
        lp = _bmm(lp, lp)
        inv = inv + _bmm(inv, lp)
    off32 = jnp.where(jnp.logical_and(same32, jnp.logical_not(same16)), l_ab, 0.0)
    inv = inv + _bmm(_bmm(inv, off32), inv)
    off64 = jnp.where(same32, 0.0, l_ab)
    inv = inv + _bmm(_bmm(inv, off64), inv)
    a_hat = _bmm(inv, a_t)
    w_hat = _bmm(inv, _bmm(l_ak, v))
    r_hat = r_t + _bmm(m_rb, a_hat)
    y0 = _bmm(m_rb, w_hat) + _bmm(m_rk, v)
    b_hat = b * g_rem
    k_hat = k * g_rem
    decay = jnp.exp(tot)

    chunk = lambda x, j: x.reshape(nh, nc, x.shape[1], n)[:, j]
    s = s_ref[...]
    ys = []
    for j in range(nc):
        u = _bmm_nt(chunk(a_hat, j), s) + chunk(w_hat, j)
        ys.append(_bmm_nt(chunk(r_hat, j), s) + chunk(y0, j))
        s = s * chunk(decay, j) + _bmm_tn(u, chunk(b_hat, j)) + _bmm_tn(chunk(v, j), chunk(k_hat, j))
    s_ref[...] = s
    y = jnp.concatenate(ys, axis=1)

    r = r_ref[...]
    mean = jnp.mean(y, axis=-1, keepdims=True)
    yc = y - mean
    var = jnp.mean(yc * yc, axis=-1, keepdims=True)
    yn = yc * lax.rsqrt(var + GN_EPS) * lnw_ref[...] + lnb_ref[...]
    bonus = jnp.sum(r * k_ref[...] * rk_ref[...], axis=-1, keepdims=True) * v_ref[...]
    y_ref[...] = (yn + bonus) * g_ref[...]


def _rwkvscan(r, lw, k, v, a, b, g, ln_w, ln_b, r_k):
    nh, t, n = r.shape
    rows = SCAN_CHUNK * SCAN_CHUNKS
    blk = pl.BlockSpec((nh, rows, n), lambda i: (0, i, 0))
    par = pl.BlockSpec((nh, 1, n), lambda i: (0, 0, 0))
    return pl.pallas_call(
        _rwkvscan_kernel,
        grid=(t // rows,),
        in_specs=[blk] * 7 + [par] * 3,
        out_specs=blk,
        out_shape=jax.ShapeDtypeStruct((nh, t, n), F32),
        scratch_shapes=[pltpu.VMEM((nh, n, n), F32)],
        compiler_params=_cparams("arbitrary"),
        name="rwkvscan",
    )(r, lw, k, v, a, b, g, ln_w, ln_b, r_k)


def _mobasel_kernel(q_ref, km_ref, o_ref):
    j = pl.program_id(1)
    q = q_ref[0]
    km = km_ref[0]
    nb = km.shape[0]
    tq = q.shape[0]
    gate = lax.dot_general(km, q, (((1,), (1,)), ((), ())), precision=HI, preferred_element_type=F32)
    n_id = lax.broadcasted_iota(jnp.int32, (nb, tq), 0)
    t_id = lax.broadcasted_iota(jnp.int32, (nb, tq), 1) + j * tq
    q_blk = t_id // MOBA_BLOCK
    cand = n_id < nb - 1
    past = n_id < q_blk
    neg_inf = jnp.float32(-jnp.inf)
    gate = jnp.where(cand, jnp.where(past, gate, GATE_NEG), neg_inf)
    sel = jnp.zeros((nb, tq), F32)
    for _ in range(min(MOBA_TOP_K, nb - 1)):
        m = jnp.max(gate, axis=0, keepdims=True)
        idx = jnp.min(jnp.where(gate == m, n_id, nb), axis=0, keepdims=True)
        pick = n_id == idx
        sel = jnp.where(pick, 1.0, sel)
        gate = jnp.where(pick, neg_inf, gate)
    bias = jnp.where(jnp.logical_or(jnp.logical_and(past, sel > 0.5), n_id == q_blk), 0.0, MASK_NEG)
    pad = jnp.zeros((BIAS_ROWS - KV_GROUP, tq), F32)
    for g in range(nb // KV_GROUP):
        o_ref[0, g] = jnp.concatenate([bias[g * KV_GROUP:(g + 1) * KV_GROUP], pad], axis=0).astype(BF16)


def _mobasel(q_hm, kmean):
    nh, t, n = q_hm.shape
    nb = kmean.shape[1]
    ng = nb // KV_GROUP
    tq = min(t, 2048)
    return pl.pallas_call(
        _mobasel_kernel,
        grid=(nh, t // tq),
        in_specs=[pl.BlockSpec((1, tq, n), lambda h, j: (h, j, 0)),
                  pl.BlockSpec((1, nb, n), lambda h, j: (h, 0, 0))],
        out_specs=pl.BlockSpec((1, ng, BIAS_ROWS, tq), lambda h, j: (h, 0, 0, j)),
        out_shape=jax.ShapeDtypeStruct((nh, ng, BIAS_ROWS, t), BF16),
        compiler_params=_cparams("arbitrary", "arbitrary"),
        name="mobasel",
    )(q_hm, kmean)


def _mobaattn_kernel(qt_ref, k_ref, vt_ref, bias_ref, o_ref, sa_ref, sb_ref, m_ref, acc_ref):
    i = pl.program_id(1)
    bs = MOBA_BLOCK
    gd = i // KV_GROUP
    q_pad = jnp.zeros((K_AUG - HEAD_DIM - BIAS_ROWS, bs), BF16)
    heads = range(HEADS_PER_STEP)

    def scores_to(dst, g):
        for hh in heads:
            q_aug = jnp.concatenate([qt_ref[hh], bias_ref[hh, g], q_pad], axis=0)
            dst[hh] = jnp.dot(k_ref[hh, g], q_aug, preferred_element_type=F32)

    def update_from(src, g, keep=None):
        for hh in heads:
            s = src[hh] if keep is None else jnp.where(keep, src[hh], MASK_NEG)
            m = m_ref[hh]
            m_new = jnp.maximum(m, jnp.max(s, axis=0, keepdims=True))
            alpha = jnp.exp2(m - m_new)
            p = jnp.exp2((s - m_new).astype(BF16))
            acc_ref[hh] = alpha * acc_ref[hh] + jnp.dot(vt_ref[hh, g], p, preferred_element_type=F32)
            m_ref[hh] = m_new

    def finish(src):
        causal = lax.broadcasted_iota(jnp.int32, (bs, bs), 0) <= lax.broadcasted_iota(jnp.int32, (bs, bs), 1)
        own = jnp.concatenate([jnp.logical_or(causal, i % KV_GROUP != b) for b in range(KV_GROUP)], axis=0)
        update_from(src, gd, own)
        for hh in heads:
            acc = acc_ref[hh]
            o_ref[hh] = acc[:HEAD_DIM] / acc[HEAD_DIM:HEAD_DIM + 1]

    m_ref[...] = jnp.full(m_ref.shape, MASK_NEG, F32)
    acc_ref[...] = jnp.zeros(acc_ref.shape, F32)
    scores_to(sa_ref, 0)

    def pair(k, _):
        g = 2 * k
        scores_to(sb_ref, g + 1)
        update_from(sa_ref, g)
        scores_to(sa_ref, g + 2)
        update_from(sb_ref, g + 1)
        return 0

    lax.fori_loop(0, gd // 2, pair, 0)

    @pl.when(gd % 2 == 1)
    def _():
        scores_to(sb_ref, gd)
        update_from(sa_ref, gd - 1)
        finish(sb_ref)

    @pl.when(gd % 2 == 0)
    def _():
        finish(sa_ref)


def _mobaattn(qt, k_aug, vt_aug, bias):
    nh, n, t = qt.shape
    gs = MOBA_BLOCK * KV_GROUP
    ng = t // gs
    hp = HEADS_PER_STEP
    return pl.pallas_call(
        _mobaattn_kernel,
        grid=(nh // hp, t // MOBA_BLOCK),
        in_specs=[pl.BlockSpec((hp, n, MOBA_BLOCK), lambda h, i: (h, 0, i)),
                  pl.BlockSpec((hp, ng, gs, K_AUG), lambda h, i: (h, 0, 0, 0)),
                  pl.BlockSpec((hp, ng, V_AUG, gs), lambda h, i: (h, 0, 0, 0)),
                  pl.BlockSpec((hp, ng, BIAS_ROWS, MOBA_BLOCK), lambda h, i: (h, 0, 0, i))],
        out_specs=pl.BlockSpec((hp, n, MOBA_BLOCK), lambda h, i: (h, 0, i)),
        out_shape=jax.ShapeDtypeStruct((nh, n, t), F32),
        scratch_shapes=[pltpu.VMEM((hp, gs, MOBA_BLOCK), F32), pltpu.VMEM((hp, gs, MOBA_BLOCK), F32),
                        pltpu.VMEM((hp, 1, MOBA_BLOCK), F32), pltpu.VMEM((hp, V_AUG, MOBA_BLOCK), F32)],
        compiler_params=_cparams("arbitrary", "arbitrary"),
        name="mobaattn",
    )(qt, k_aug, vt_aug, bias)


def _mix_kernel(ya_ref, yb_ref, sg_ref, x_ref, wa_ref, wb_ref, wo_ref, g1_ref, n2_ref, sc_ref, sh_ref,
                rw_ref, rb_ref, x1_ref, h2_ref, eid_ref, rank_ref, wt_ref, cnt_ref, base_ref):
    i = pl.program_id(0)

    @pl.when(i == 0)
    def _():
        base_ref[...] = jnp.zeros_like(base_ref)

    d = x_ref.shape[1]
    tm = x_ref.shape[0]
    pa = jnp.dot(ya_ref[...].astype(BF16), wa_ref[...], preferred_element_type=F32)
    pb = jnp.dot(yb_ref[...].astype(BF16), wb_ref[...], preferred_element_type=F32)
    sg = sg_ref[...]
    mixed = sg[:, :d] * pa + sg[:, d:] * pb
    mixed = jnp.dot(mixed.astype(BF16), wo_ref[...], preferred_element_type=F32)
    x1 = x_ref[...] + g1_ref[...] * mixed
    x1_ref[...] = x1
    ms = jnp.mean(x1 * x1, axis=-1, keepdims=True)
    h2 = x1 * lax.rsqrt(ms + RMS_EPS) * n2_ref[...]
    h2 = h2 * (1.0 + sc_ref[...]) + sh_ref[...]
    h2_ref[...] = h2

    logits = jnp.dot(h2, rw_ref[...], precision=HI, preferred_element_type=F32) + rb_ref[...]
    ne = logits.shape[1]
    e_id = lax.broadcasted_iota(jnp.int32, (tm, ne), 1)
    neg_inf = jnp.float32(-jnp.inf)
    work = logits
    picks, vals = [], []
    for _ in range(EXPERT_TOP_K):
        m = jnp.max(work, axis=-1, keepdims=True)
        idx = jnp.min(jnp.where(work == m, e_id, ne), axis=-1, keepdims=True)
        pick = e_id == idx
        picks.append((idx, pick))
        vals.append(m)
        work = jnp.where(pick, neg_inf, work)
    den = sum(jnp.exp(v - vals[0]) for v in vals)
    chosen = jnp.zeros((tm, ne), F32)
    for _, pick in picks:
        chosen = jnp.where(pick, 1.0, chosen)
    r_id = lax.broadcasted_iota(jnp.int32, (tm, tm), 0)
    c_id = lax.broadcasted_iota(jnp.int32, (tm, tm), 1)
    before = jnp.where(r_id > c_id, 1.0, 0.0).astype(BF16)
    ahead = jnp.dot(before, chosen.astype(BF16), preferred_element_type=F32) + base_ref[0:1, :]
    k_id = lax.broadcasted_iota(jnp.int32, (tm, EXPERT_TOP_K), 1)
    eid = jnp.zeros((tm, EXPERT_TOP_K), jnp.int32)
    rank = jnp.zeros((tm, EXPERT_TOP_K), jnp.int32)
    wt = jnp.zeros((tm, EXPERT_TOP_K), F32)
    for kk, ((idx, pick), v) in enumerate(zip(picks, vals)):
        rk = jnp.sum(jnp.where(pick, ahead, 0.0), axis=-1, keepdims=True).astype(jnp.int32)
        eid = jnp.where(k_id == kk, idx, eid)
        rank = jnp.where(k_id == kk, rk, rank)
        wt = jnp.where(k_id == kk, jnp.exp(v - vals[0]) / den, wt)
    eid_ref[...] = eid
    rank_ref[...] = rank
    wt_ref[...] = wt
    total = base_ref[0:1, :] + jnp.sum(chosen, axis=0, keepdims=True)
    base_ref[0:1, :] = total
    cnt_ref[...] = jnp.broadcast_to(total, cnt_ref.shape)


def _mix(ya, yb, sg, x2, w_a, w_b, w_o, gate1, norm2_g, scale2, shift2, router_w, router_b):
    t, d = x2.shape
    tm = 512
    ne = router_w.shape[1]
    full = lambda a: pl.BlockSpec(a.shape, lambda i: (0, 0))
    row = lambda n: pl.BlockSpec((tm, n), lambda i: (i, 0))
    return pl.pallas_call(
        _mix_kernel,
        grid=(t // tm,),
        in_specs=[row(WIDTH), row(WIDTH), row(2 * d), row(d), full(w_a), full(w_b), full(w_o),
                  full(gate1), full(norm2_g), full(scale2), full(shift2), full(router_w), full(router_b)],
        out_specs=[row(d), row(d), row(EXPERT_TOP_K), row(EXPERT_TOP_K), row(EXPERT_TOP_K),
                   pl.BlockSpec((8, ne), lambda i: (0, 0))],
        out_shape=[jax.ShapeDtypeStruct((t, d), F32), jax.ShapeDtypeStruct((t, d), F32),
                   jax.ShapeDtypeStruct((t, EXPERT_TOP_K), jnp.int32),
                   jax.ShapeDtypeStruct((t, EXPERT_TOP_K), jnp.int32),
                   jax.ShapeDtypeStruct((t, EXPERT_TOP_K), F32),
                   jax.ShapeDtypeStruct((8, ne), F32)],
        scratch_shapes=[pltpu.VMEM((8, ne), F32)],
        compiler_params=_cparams("arbitrary"),
        name="mix",
    )(ya, yb, sg, x2, w_a, w_b, w_o, gate1, norm2_g, scale2, shift2, router_w, router_b)


DISPATCH_TILE = 256


def _slots_kernel(off_ref, eid_ref, rank_ref, o_ref):
    eid = eid_ref[...]
    slot = rank_ref[...]
    for e in range(N_EXPERTS):
        slot = slot + jnp.where(eid == e, off_ref[e], 0)
    o_ref[...] = slot


def _slots(off, eid, rank):
    n = eid.size
    lanes = 128
    shape = (n // lanes, lanes)
    full = pl.BlockSpec(shape, lambda i: (0, 0))
    out = pl.pallas_call(
        _slots_kernel,
        grid=(1,),
        in_specs=[pl.BlockSpec(memory_space=pltpu.SMEM), full, full],
        out_specs=full,
        out_shape=jax.ShapeDtypeStruct(shape, jnp.int32),
        compiler_params=_cparams("arbitrary"),
        name="slots",
    )(off, eid.reshape(shape), rank.reshape(shape))
    return out.reshape(n)


def _dispatch_kernel(slot_ref, h_ref, xs_in_ref, xs_ref, sem):
    del xs_in_ref
    tm = DISPATCH_TILE

    def body(t, _):
        for kk in range(EXPERT_TOP_K):
            slot = slot_ref[t * EXPERT_TOP_K + kk]
            pltpu.make_async_copy(h_ref.at[pl.ds(t, 1)], xs_ref.at[pl.ds(slot, 1)], sem).start()
        return 0

    lax.fori_loop(0, tm, body, 0, unroll=2)
    for _ in range(EXPERT_TOP_K):
        pltpu.make_async_copy(h_ref, xs_ref.at[pl.ds(0, tm)], sem).wait()


def _dispatch(slot_flat, h2, n_slots):
    t, d = h2.shape
    tm = DISPATCH_TILE
    return pl.pallas_call(
        _dispatch_kernel,
        grid=(t // tm,),
        in_specs=[pl.BlockSpec((tm * EXPERT_TOP_K,), lambda i: (i,), memory_space=pltpu.SMEM),
                  pl.BlockSpec((tm, d), lambda i: (i, 0)),
                  pl.BlockSpec(memory_space=pl.ANY)],
        out_specs=pl.BlockSpec(memory_space=pl.ANY),
        out_shape=jax.ShapeDtypeStruct((n_slots, d), F32),
        scratch_shapes=[pltpu.SemaphoreType.DMA(())],
        input_output_aliases={2: 0},
        compiler_params=pltpu.CompilerParams(dimension_semantics=("arbitrary",), has_side_effects=True),
        name="dispatch",
    )(slot_flat, h2, jnp.zeros((n_slots, d), F32))


def _experts_kernel(te_ref, nu_ref, x_ref, wg_ref, bg_ref, wu_ref, bu_ref, wd_ref, bd_ref, y_ref,
                    wg_s, wu_s, wd_s):
    i = pl.program_id(0)
    used = i < nu_ref[0]
    new_expert = jnp.logical_or(i == 0, te_ref[i] != te_ref[jnp.maximum(i - 1, 0)])

    @pl.when(jnp.logical_and(used, new_expert))
    def _():
        wg_s[...] = wg_ref[0].astype(BF16)
        wu_s[...] = wu_ref[0].astype(BF16)
        wd_s[...] = wd_ref[0].astype(BF16)

    @pl.when(used)
    def _():
        x = x_ref[...].astype(BF16)
        gt = jnp.minimum(jnp.dot(x, wg_s[...], preferred_element_type=F32) + bg_ref[0], SWIGLU_LIMIT)
        up = jnp.clip(jnp.dot(x, wu_s[...], preferred_element_type=F32) + bu_ref[0], -SWIGLU_LIMIT, SWIGLU_LIMIT)
        hid = (up + 1.0) * gt * _sigmoid(SWIGLU_ALPHA * gt)
        y_ref[...] = jnp.dot(hid.astype(BF16), wd_s[...], preferred_element_type=F32) + bd_ref[0]

    @pl.when(jnp.logical_not(used))
    def _():
        y_ref[...] = jnp.zeros_like(y_ref)


def _experts(tile_expert, n_used, xs, w_gate, b_gate, w_up, b_up, w_down, b_down):
    n_slots, d = xs.shape
    f = w_gate.shape[2]
    tm = EXPERT_TILE
    n_tiles = n_slots // tm
    row = pl.BlockSpec((tm, d), lambda i, te, nu: (i, 0))
    wspec = lambda a, b: pl.BlockSpec((1, a, b), lambda i, te, nu: (te[i], 0, 0))
    grid_spec = pltpu.PrefetchScalarGridSpec(
        num_scalar_prefetch=2,
        grid=(n_tiles,),
        in_specs=[row, wspec(d, f), wspec(1, f), wspec(d, f), wspec(1, f), wspec(f, d), wspec(1, d)],
        out_specs=row,
        scratch_shapes=[pltpu.VMEM((d, f), BF16), pltpu.VMEM((d, f), BF16), pltpu.VMEM((f, d), BF16)],
    )
    return pl.pallas_call(
        _experts_kernel,
        grid_spec=grid_spec,
        out_shape=jax.ShapeDtypeStruct((n_slots, d), F32),
        compiler_params=_cparams("arbitrary"),
        name="experts",
    )(tile_expert, n_used, xs, w_gate, b_gate, w_up, b_up, w_down, b_down)


def _combine_kernel(slot_ref, ys_ref, wt_ref, x1_ref, g2_ref, nf_ref, o_ref, buf, sem):
    tm = DISPATCH_TILE

    def body(t, _):
        for kk in range(EXPERT_TOP_K):
            slot = slot_ref[t * EXPERT_TOP_K + kk]
            pltpu.make_async_copy(ys_ref.at[pl.ds(slot, 1)], buf.at[kk, pl.ds(t, 1)], sem).start()
        return 0

    lax.fori_loop(0, tm, body, 0, unroll=2)
    for kk in range(EXPERT_TOP_K):
        pltpu.make_async_copy(ys_ref.at[pl.ds(0, tm)], buf.at[kk], sem).wait()
    wt = wt_ref[...]
    moe = wt[:, 0:1] * buf[0]
    for kk in range(1, EXPERT_TOP_K):
        moe = moe + wt[:, kk:kk + 1] * buf[kk]
    x = x1_ref[...] + g2_ref[...] * moe
    ms = jnp.mean(x * x, axis=-1, keepdims=True)
    o_ref[...] = x * lax.rsqrt(ms + RMS_EPS) * nf_ref[...]


def _combine(slot_flat, ys, wt, x1, gate2, normf_g):
    t, d = x1.shape
    tm = DISPATCH_TILE
    full = lambda a: pl.BlockSpec(a.shape, lambda i: (0, 0))
    row = lambda n: pl.BlockSpec((tm, n), lambda i: (i, 0))
    return pl.pallas_call(
        _combine_kernel,
        grid=(t // tm,),
        in_specs=[pl.BlockSpec((tm * EXPERT_TOP_K,), lambda i: (i,), memory_space=pltpu.SMEM),
                  pl.BlockSpec(memory_space=pl.ANY),
                  row(EXPERT_TOP_K), row(d), full(gate2), full(normf_g)],
        out_specs=row(d),
        out_shape=jax.ShapeDtypeStruct((t, d), F32),
        scratch_shapes=[pltpu.VMEM((EXPERT_TOP_K, tm, d), F32), pltpu.SemaphoreType.DMA(())],
        compiler_params=_cparams("arbitrary"),
        name="combine",
    )(slot_flat, ys, wt, x1, gate2, normf_g)


def _heads(a):
    t = a.shape[0]
    return a.reshape(t, N_HEADS, HEAD_DIM).transpose(1, 0, 2)


def _unheads(a):
    return a.transpose(1, 0, 2).reshape(a.shape[1], WIDTH)


def kernel(x, c, w_ada, b_ada, norm1_g, w_in, rwkv_mu, rwkv_w0, rwkv_w2, rwkv_a0, rwkv_a2, rwkv_g2, rwkv_k_k, rwkv_k_a, rwkv_r_k, rwkv_ln_w, rwkv_ln_b, w_branch_a, w_branch_b, w_out, norm2_g, router_w, router_b, exp_w_gate, exp_b_gate, exp_w_up, exp_b_up, exp_w_down, exp_b_down, normf_g):
    bsz, t, d = x.shape
    assert bsz == 1 and t % (MOBA_BLOCK * KV_GROUP) == 0
    x2 = x.reshape(t, d)
    row = lambda a: a.reshape(1, -1)

    mod = _mod(c, w_ada, b_ada)
    shift1, scale1, gate1, shift2, scale2, gate2 = [mod[:, j * d:(j + 1) * d] for j in range(6)]

    qkv_end = RWKV_PROJ + 3 * WIDTH
    w_in_b = w_in.astype(BF16)
    p_rwkv, qkv, sg, kmean = _inproj(x2, row(norm1_g), scale1, shift1,
                              w_in_b[:, :RWKV_PROJ], w_in_b[:, RWKV_PROJ:qkv_end], w_in_b[:, qkv_end:])

    zeros_lora = jnp.zeros((DECAY_LORA, WIDTH), F32)
    w2p = jnp.concatenate([rwkv_w2, zeros_lora], axis=0)
    a2p = jnp.concatenate([zeros_lora, rwkv_a2], axis=0)
    head_of = jnp.arange(WIDTH) // HEAD_DIM
    bd = (head_of[:, None] == head_of[None, :]).astype(F32)
    r, lw, k, v, av, bv, g = _rwkvprep(p_rwkv, row(rwkv_mu), row(rwkv_w0), w2p.astype(BF16), row(rwkv_a0),
                                       a2p.astype(BF16), rwkv_g2.astype(BF16), row(rwkv_k_k), row(rwkv_k_a),
                                       bd.astype(BF16))
    per_head = lambda a: a.reshape(N_HEADS, 1, HEAD_DIM)
    y_a = _rwkvscan(_heads(r), _heads(lw), _heads(k), _heads(v), _heads(av), _heads(bv), _heads(g),
                    per_head(rwkv_ln_w), per_head(rwkv_ln_b), per_head(rwkv_r_k))
    y_a = _unheads(y_a)

    nb = t // MOBA_BLOCK
    q_hm = _heads(qkv[:, :WIDTH])
    k_hm = _heads(qkv[:, WIDTH:2 * WIDTH])
    v_hm = _heads(qkv[:, 2 * WIDTH:])
    gs = MOBA_BLOCK * KV_GROUP
    ng = t // gs
    bias = _mobasel(q_hm, kmean.reshape(nb, N_HEADS, HEAD_DIM).transpose(1, 0, 2))
    qt = (q_hm * (HEAD_DIM ** -0.5 * LOG2E)).astype(BF16).transpose(0, 2, 1)
    blk_in_group = (jnp.arange(t) % gs) // MOBA_BLOCK
    indicator = (blk_in_group[:, None] == jnp.arange(K_AUG - HEAD_DIM)[None, :]).astype(BF16)
    k_aug = jnp.concatenate([k_hm.astype(BF16), jnp.broadcast_to(indicator, (N_HEADS, t, K_AUG - HEAD_DIM))],
                            axis=2).reshape(N_HEADS, ng, gs, K_AUG)
    ones_row = (jnp.arange(V_AUG - HEAD_DIM) == 0).astype(BF16)
    vt_aug = jnp.concatenate([v_hm.astype(BF16).transpose(0, 2, 1),
                              jnp.broadcast_to(ones_row[None, :, None], (N_HEADS, V_AUG - HEAD_DIM, t))], axis=1)
    vt_aug = vt_aug.reshape(N_HEADS, V_AUG, ng, gs).transpose(0, 2, 1, 3)
    y_b = _mobaattn(qt, k_aug, vt_aug, bias)
    y_b = y_b.transpose(2, 0, 1).reshape(t, WIDTH)

    x1, h2, eid, rank, wt, cnt = _mix(y_a, y_b, sg, x2, w_branch_a.astype(BF16), w_branch_b.astype(BF16),
                                      w_out.astype(BF16), gate1, row(norm2_g), scale2, shift2,
                                      router_w, row(router_b))

    counts = cnt[0].astype(jnp.int32)
    tiles_per = (counts + EXPERT_TILE - 1) // EXPERT_TILE
    tile_end = jnp.cumsum(tiles_per)
    off = (tile_end - tiles_per) * EXPERT_TILE
    n_tiles = (t * EXPERT_TOP_K + N_EXPERTS * (EXPERT_TILE - 1)) // EXPERT_TILE
    n_used = tile_end[-1:]
    tile_expert = jnp.minimum(jnp.sum(tile_end[None, :] <= jnp.arange(n_tiles)[:, None], axis=1),
                              N_EXPERTS - 1).astype(jnp.int32)
    last_used = tile_expert[jnp.maximum(n_used[0] - 1, 0)]
    tile_expert = jnp.where(jnp.arange(n_tiles) < n_used[0], tile_expert, last_used)

    slot_flat = _slots(off, eid, rank)
    xs = _dispatch(slot_flat, h2, n_tiles * EXPERT_TILE)
    f = exp_w_gate.shape[2]
    ys = _experts(tile_expert, n_used, xs,
                  exp_w_gate, exp_b_gate.reshape(N_EXPERTS, 1, f),
                  exp_w_up, exp_b_up.reshape(N_EXPERTS, 1, f),
                  exp_w_down, exp_b_down.reshape(N_EXPERTS, 1, d))
    out = _combine(slot_flat, ys, wt, x1, gate2, row(normf_g))
    return out.reshape(bsz, t, d)
```

```python
import jax
import jax.numpy as jnp
from jax import lax
from jax.experimental import pallas as pl
from jax.experimental.pallas import tpu as pltpu

F32 = jnp.float32
BF16 = jnp.bfloat16
HI = lax.Precision.HIGHEST

HEAD_DIM = 64
N_HEADS = 8
WIDTH = N_HEADS * HEAD_DIM
DECAY_LORA = 64
AAA_LORA = 64
GATE_LORA = 128
RWKV_PROJ = 3 * WIDTH + DECAY_LORA + AAA_LORA + GATE_LORA
GN_EPS = 64e-5
MOBA_BLOCK = 256
MOBA_TOP_K = 3
KV_GROUP = 4
BIAS_ROWS = 16
K_AUG = 128
V_AUG = HEAD_DIM + 16
HEADS_PER_STEP = 2
LOG2E = 1.4426950408889634
N_EXPERTS = 32
EXPERT_TOP_K = 4
SWIGLU_LIMIT = 7.0
SWIGLU_ALPHA = 1.702
RMS_EPS = 1e-5
GATE_NEG = -1e30
MASK_NEG = -1e30

SCAN_CHUNK = 64
SCAN_CHUNKS = 2
EXPERT_TILE = 256
VMEM_LIMIT = 56 * 1024 * 1024


def _cparams(*sem):
    return pltpu.CompilerParams(dimension_semantics=sem, vmem_limit_bytes=VMEM_LIMIT)


def _sigmoid(x):
    return 1.0 / (1.0 + jnp.exp(-x))


def _mod_kernel(c_ref, w_ref, b_ref, o_ref):
    c = c_ref[...]
    s = c * _sigmoid(c)
    o_ref[...] = jnp.dot(s, w_ref[...], precision=HI, preferred_element_type=F32) + b_ref[...]


def _mod(c, w_ada, b_ada):
    d = c.shape[-1]
    n = w_ada.shape[1]
    c8 = jnp.broadcast_to(c[:1], (8, d))
    tn = 1024
    out = pl.pallas_call(
        _mod_kernel,
        grid=(n // tn,),
        in_specs=[pl.BlockSpec((8, d), lambda j: (0, 0)),
                  pl.BlockSpec((d, tn), lambda j: (0, j)),
                  pl.BlockSpec((1, tn), lambda j: (0, j))],
        out_specs=pl.BlockSpec((8, tn), lambda j: (0, j)),
        out_shape=jax.ShapeDtypeStruct((8, n), F32),
        compiler_params=_cparams("arbitrary"),
        name="mod",
    )(c8, w_ada, b_ada.reshape(1, n))
    return out[:1]


def _inproj_kernel(x_ref, g_ref, sc_ref, sh_ref, wr_ref, wq_ref, wg_ref,
                   pr_ref, sg_ref, km_ref, qh_ref, qt_ref, ka_ref, va_ref):
    i = pl.program_id(0)
    x = x_ref[...]
    tm = x.shape[0]
    ms = jnp.mean(x * x, axis=-1, keepdims=True)
    h = x * lax.rsqrt(ms + RMS_EPS) * g_ref[...]
    h = h * (1.0 + sc_ref[...]) + sh_ref[...]
    hb = h.astype(BF16)
    pr_ref[...] = jnp.dot(hb, wr_ref[...], preferred_element_type=F32)
    sg_ref[...] = _sigmoid(jnp.dot(hb, wg_ref[...], preferred_element_type=F32))
    qkv = jnp.dot(hb, wq_ref[...], preferred_element_type=F32)
    km_ref[0] = jnp.mean(qkv[:, WIDTH:2 * WIDTH], axis=0, keepdims=True)
    col = lax.broadcasted_iota(jnp.int32, (tm, K_AUG - HEAD_DIM), 1)
    indicator = jnp.where(col == i % KV_GROUP, 1.0, 0.0).astype(BF16)
    row = lax.broadcasted_iota(jnp.int32, (V_AUG - HEAD_DIM, tm), 0)
    ones_row = jnp.where(row == 0, 1.0, 0.0).astype(BF16)
    for hd in range(N_HEADS):
        lo = hd * HEAD_DIM
        q = qkv[:, lo:lo + HEAD_DIM]
        kh = qkv[:, WIDTH + lo:WIDTH + lo + HEAD_DIM]
        vh = qkv[:, 2 * WIDTH + lo:2 * WIDTH + lo + HEAD_DIM]
        qh_ref[hd] = q
        qt_ref[hd] = (q * (HEAD_DIM ** -0.5 * LOG2E)).T.astype(BF16)
        ka_ref[hd, 0] = jnp.concatenate([kh.astype(BF16), indicator], axis=1)
        va_ref[hd, 0] = jnp.concatenate([vh.T.astype(BF16), ones_row], axis=0)


def _inproj(x2, norm_g, scale, shift, w_r, w_q, w_g):
    t, d = x2.shape
    tm = MOBA_BLOCK
    gs = MOBA_BLOCK * KV_GROUP
    ng = t // gs
    full = lambda a: pl.BlockSpec(a.shape, lambda i: (0, 0))
    row = lambda n: pl.BlockSpec((tm, n), lambda i: (i, 0))
    return pl.pallas_call(
        _inproj_kernel,
        grid=(t // tm,),
        in_specs=[row(d), full(norm_g), full(scale), full(shift), full(w_r), full(w_q), full(w_g)],
        out_specs=[row(w_r.shape[1]), row(w_g.shape[1]),
                   pl.BlockSpec((1, 1, WIDTH), lambda i: (i, 0, 0)),
                   pl.BlockSpec((N_HEADS, tm, HEAD_DIM), lambda i: (0, i, 0)),
                   pl.BlockSpec((N_HEADS, HEAD_DIM, tm), lambda i: (0, 0, i)),
                   pl.BlockSpec((N_HEADS, 1, tm, K_AUG), lambda i: (0, i // KV_GROUP, i % KV_GROUP, 0)),
                   pl.BlockSpec((N_HEADS, 1, V_AUG, tm), lambda i: (0, i // KV_GROUP, 0, i % KV_GROUP))],
        out_shape=[jax.ShapeDtypeStruct((t, w_r.shape[1]), F32),
                   jax.ShapeDtypeStruct((t, w_g.shape[1]), F32),
                   jax.ShapeDtypeStruct((t // tm, 1, WIDTH), F32),
                   jax.ShapeDtypeStruct((N_HEADS, t, HEAD_DIM), F32),
                   jax.ShapeDtypeStruct((N_HEADS, HEAD_DIM, t), BF16),
                   jax.ShapeDtypeStruct((N_HEADS, ng, gs, K_AUG), BF16),
                   jax.ShapeDtypeStruct((N_HEADS, ng, V_AUG, gs), BF16)],
        compiler_params=_cparams("arbitrary"),
        name="inproj",
    )(x2, norm_g, scale, shift, w_r, w_q, w_g)


def _rwkvprep_kernel(p_ref, mu_ref, w0_ref, w2_ref, a0_ref, a2_ref, g2_ref, kk_ref, ka_ref, bd_ref,
                     r_out, lw_out, k_out, v_out, a_out, b_out, g_out, prev_ref):
    i = pl.program_id(0)

    @pl.when(i == 0)
    def _():
        prev_ref[...] = jnp.zeros_like(prev_ref)

    p = p_ref[...]
    tm = p.shape[0]
    rolled = pltpu.roll(p, 1, 0)
    row0 = lax.broadcasted_iota(jnp.int32, p.shape, 0) == 0
    p_prev = jnp.where(row0, prev_ref[0:1, :], rolled)
    prev_ref[0:1, :] = p[tm - 1:tm, :]
    ps = p + (p_prev - p) * mu_ref[...]
    r = ps[:, 0:WIDTH]
    k = ps[:, WIDTH:2 * WIDTH]
    v = ps[:, 2 * WIDTH:3 * WIDTH]
    xwa = ps[:, 3 * WIDTH:3 * WIDTH + DECAY_LORA + AAA_LORA]
    xg = ps[:, 3 * WIDTH + DECAY_LORA + AAA_LORA:RWKV_PROJ]
    dot = lambda lhs, w_ref: jnp.dot(lhs.astype(BF16), w_ref[...], preferred_element_type=F32)
    z = w0_ref[...] + dot(jnp.tanh(xwa), w2_ref)
    nz = -z
    softplus = jnp.maximum(nz, 0.0) + jnp.log(1.0 + jnp.exp(-jnp.abs(nz)))
    w = -softplus - 0.5
    a = _sigmoid(a0_ref[...] + dot(xwa, a2_ref))
    g = dot(_sigmoid(xg), g2_ref)
    kk = k * kk_ref[...]
    sq = kk * kk
    sq_hi = sq.astype(BF16)
    ss = dot(sq_hi, bd_ref) + dot(sq - sq_hi.astype(F32), bd_ref)
    kk = kk / jnp.maximum(jnp.sqrt(ss), 1e-12)
    outs = ((r_out, r), (lw_out, -jnp.exp(w)),
            (k_out, k * (1.0 + (a - 1.0) * ka_ref[...])), (v_out, v), (a_out, -kk), (b_out, kk * a), (g_out, g))
    for ref, val in outs:
        for hd in range(N_HEADS):
            ref[hd] = val[:, hd * HEAD_DIM:(hd + 1) * HEAD_DIM]


def _rwkvprep(p_rwkv, mu, w0, w2p, a0, a2p, g2, k_k, k_a, bd):
    t = p_rwkv.shape[0]
    tm = 256
    full = lambda a: pl.BlockSpec(a.shape, lambda i: (0, 0))
    row = lambda n: pl.BlockSpec((tm, n), lambda i: (i, 0))
    outs = [jax.ShapeDtypeStruct((N_HEADS, t, HEAD_DIM), F32)] * 7
    return pl.pallas_call(
        _rwkvprep_kernel,
        grid=(t // tm,),
        in_specs=[row(RWKV_PROJ), full(mu), full(w0), full(w2p), full(a0), full(a2p), full(g2),
                  full(k_k), full(k_a), full(bd)],
        out_specs=[pl.BlockSpec((N_HEADS, tm, HEAD_DIM), lambda i: (0, i, 0))] * 7,
        out_shape=outs,
        scratch_shapes=[pltpu.VMEM((8, RWKV_PROJ), F32)],
        compiler_params=_cparams("arbitrary"),
        name="rwkvprep",
    )(p_rwkv, mu, w0, w2p, a0, a2p, g2, k_k, k_a, bd)


def _bmm(a, b):
    return jnp.einsum("hmk,hkn->hmn", a.astype(BF16), b.astype(BF16), preferred_element_type=F32)


def _bmm_nt(a, b):
    return jnp.einsum("hmk,hnk->hmn", a.astype(BF16), b.astype(BF16), preferred_element_type=F32)


def _bmm_tn(a, b):
    return jnp.einsum("hkm,hkn->hmn", a.astype(BF16), b.astype(BF16), preferred_element_type=F32)


def _rwkvscan_kernel(r_ref, lw_ref, k_ref, v_ref, a_ref, b_ref, g_ref, lnw_ref, lnb_ref, rk_ref,
                     y_ref, s_ref):
    i = pl.program_id(0)

    @pl.when(i == 0)
    def _():
        s_ref[...] = jnp.zeros_like(s_ref)

    c = SCAN_CHUNK
    nc = SCAN_CHUNKS
    nh, _, n = r_ref.shape
    nb = nh * nc
    split = lambda ref: ref[...].reshape(nb, c, n)
    r, lw, k, v, a, b = (split(ref) for ref in (r_ref, lw_ref, k_ref, v_ref, a_ref, b_ref))

    row = lax.broadcasted_iota(jnp.int32, (nb, c, c), 1)
    col = lax.broadcasted_iota(jnp.int32, (nb, c, c), 2)
    lower = row >= col
    strict = row > col
    cum = jnp.einsum("hmk,hkn->hmn", lower.astype(F32), lw, precision=HI,
                     preferred_element_type=F32)
    tot = cum[:, c - 1:c, :]
    g_in = jnp.exp(cum)
    g_ex = jnp.exp(cum - lw)
    g_inv = jnp.exp(-cum)
    g_rem = jnp.exp(tot - cum)
    a_t = a * g_ex
    r_t = r * g_in
    b_t = b * g_inv
    k_t = k * g_inv
    l_ab = jnp.where(strict, _bmm_nt(a_t, b_t), 0.0)
    l_ak = jnp.where(strict, _bmm_nt(a_t, k_t), 0.0)
    m_rb = jnp.where(lower, _bmm_nt(r_t, b_t), 0.0)
    m_rk = jnp.where(lower, _bmm_nt(r_t, k_t), 0.0)
    same16 = (row // 16) == (col // 16)
    same32 = (row // 32) == (col // 32)
    diag16 = jnp.where(same16, l_ab, 0.0)
    inv = jnp.where(row == col, 1.0, 0.0) + diag16
    lp = diag16
    for _ in range(3):
        lp = _bmm(lp, lp)
        inv = inv + _bmm(inv, lp)
    off32 = jnp.where(jnp.logical_and(same32, jnp.logical_not(same16)), l_ab, 0.0)
    inv = inv + _bmm(_bmm(inv, off32), inv)
    off64 = jnp.where(same32, 0.0, l_ab)
    inv = inv + _bmm(_bmm(inv, off64), inv)
    a_hat = _bmm(inv, a_t)
    w_hat = _bmm(inv, _bmm(l_ak, v))
    r_hat = r_t + _bmm(m_rb, a_hat)
    y0 = _bmm(m_rb, w_hat) + _bmm(m_rk, v)
    b_hat = b * g_rem
    k_hat = k * g_rem
    decay = jnp.exp(tot)

    chunk = lambda x, j: x.reshape(nh, nc, x.shape[1], n)[:, j]
    s = s_ref[...]
    ys = []
    for j in range(nc):
        u = _bmm_nt(chunk(a_hat, j), s) + chunk(w_hat, j)
        ys.append(_bmm_nt(chunk(r_hat, j), s) + chunk(y0, j))
        s = s * chunk(decay, j) + _bmm_tn(u, chunk(b_hat, j)) + _bmm_tn(chunk(v, j), chunk(k_hat, j))
    s_ref[...] = s
    y = jnp.concatenate(ys, axis=1)

    r = r_ref[...]
    mean = jnp.mean(y, axis=-1, keepdims=True)
    yc = y - mean
    var = jnp.mean(yc * yc, axis=-1, keepdims=True)
    yn = yc * lax.rsqrt(var + GN_EPS) * lnw_ref[...] + lnb_ref[...]
    bonus = jnp.sum(r * k_ref[...] * rk_ref[...], axis=-1, keepdims=True) * v_ref[...]
    out = (yn + bonus) * g_ref[...]
    for hd in range(nh):
        y_ref[:, hd * n:(hd + 1) * n] = out[hd]


def _rwkvscan(r, lw, k, v, a, b, g, ln_w, ln_b, r_k):
    nh, t, n = r.shape
    rows = SCAN_CHUNK * SCAN_CHUNKS
    blk = pl.BlockSpec((nh, rows, n), lambda i: (0, i, 0))
    par = pl.BlockSpec((nh, 1, n), lambda i: (0, 0, 0))
    return pl.pallas_call(
        _rwkvscan_kernel,
        grid=(t // rows,),
        in_specs=[blk] * 7 + [par] * 3,
        out_specs=pl.BlockSpec((rows, nh * n), lambda i: (i, 0)),
        out_shape=jax.ShapeDtypeStruct((t, nh * n), F32),
        scratch_shapes=[pltpu.VMEM((nh, n, n), F32)],
        compiler_params=_cparams("arbitrary"),
        name="rwkvscan",
    )(r, lw, k, v, a, b, g, ln_w, ln_b, r_k)


def _mobasel_kernel(q_ref, km_ref, o_ref):
    j = pl.program_id(1)
    q = q_ref[0]
    km = km_ref[0]
    nb = km.shape[0]
    tq = q.shape[0]
    gate = lax.dot_general(km, q, (((1,), (1,)), ((), ())), precision=HI, preferred_element_type=F32)
    n_id = lax.broadcasted_iota(jnp.int32, (nb, tq), 0)
    t_id = lax.broadcasted_iota(jnp.int32, (nb, tq), 1) + j * tq
    q_blk = t_id // MOBA_BLOCK
    cand = n_id < nb - 1
    past = n_id < q_blk
    neg_inf = jnp.float32(-jnp.inf)
    gate = jnp.where(cand, jnp.where(past, gate, GATE_NEG), neg_inf)
    sel = jnp.zeros((nb, tq), F32)
    for _ in range(min(MOBA_TOP_K, nb - 1)):
        m = jnp.max(gate, axis=0, keepdims=True)
        idx = jnp.min(jnp.where(gate == m, n_id, nb), axis=0, keepdims=True)
        pick = n_id == idx
        sel = jnp.where(pick, 1.0, sel)
        gate = jnp.where(pick, neg_inf, gate)
    bias = jnp.where(jnp.logical_or(jnp.logical_and(past, sel > 0.5), n_id == q_blk), 0.0, MASK_NEG)
    pad = jnp.zeros((BIAS_ROWS - KV_GROUP, tq), F32)
    for g in range(nb // KV_GROUP):
        o_ref[0, g] = jnp.concatenate([bias[g * KV_GROUP:(g + 1) * KV_GROUP], pad], axis=0).astype(BF16)


def _mobasel(q_hm, kmean):
    nh, t, n = q_hm.shape
    nb = kmean.shape[1]
    ng = nb // KV_GROUP
    tq = min(t, 2048)
    return pl.pallas_call(
        _mobasel_kernel,
        grid=(nh, t // tq),
        in_specs=[pl.BlockSpec((1, tq, n), lambda h, j: (h, j, 0)),
                  pl.BlockSpec((1, nb, n), lambda h, j: (h, 0, 0))],
        out_specs=pl.BlockSpec((1, ng, BIAS_ROWS, tq), lambda h, j: (h, 0, 0, j)),
        out_shape=jax.ShapeDtypeStruct((nh, ng, BIAS_ROWS, t), BF16),
        compiler_params=_cparams("arbitrary", "arbitrary"),
        name="mobasel",
    )(q_hm, kmean)


def _mobaattn_kernel(qt_ref, k_ref, vt_ref, bias_ref, o_ref, sa_ref, sb_ref, m_ref, acc_ref):
    i = pl.program_id(1)
    bs = MOBA_BLOCK
    gd = i // KV_GROUP
    q_pad = jnp.zeros((K_AUG - HEAD_DIM - BIAS_ROWS, bs), BF16)
    heads = range(HEADS_PER_STEP)

    def scores_to(dst, g):
        for hh in heads:
            q_aug = jnp.concatenate([qt_ref[hh], bias_ref[hh, g], q_pad], axis=0)
            dst[hh] = jnp.dot(k_ref[hh, g], q_aug, preferred_element_type=F32)

    def update_from(src, g, keep=None):
        for hh in heads:
            s = src[hh] if keep is None else jnp.where(keep, src[hh], MASK_NEG)
            m = m_ref[hh]
            m_new = jnp.maximum(m, jnp.max(s, axis=0, keepdims=True))
            alpha = jnp.exp2(m - m_new)
            p = jnp.exp2((s - m_new).astype(BF16))
            acc_ref[hh] = alpha * acc_ref[hh] + jnp.dot(vt_ref[hh, g], p, preferred_element_type=F32)
            m_ref[hh] = m_new

    def finish(src):
        causal = lax.broadcasted_iota(jnp.int32, (bs, bs), 0) <= lax.broadcasted_iota(jnp.int32, (bs, bs), 1)
        own = jnp.concatenate([jnp.logical_or(causal, i % KV_GROUP != b) for b in range(KV_GROUP)], axis=0)
        update_from(src, gd, own)
        for hh in heads:
            acc = acc_ref[hh]
            o_ref[hh] = acc[:HEAD_DIM] / acc[HEAD_DIM:HEAD_DIM + 1]

    m_ref[...] = jnp.full(m_ref.shape, MASK_NEG, F32)
    acc_ref[...] = jnp.zeros(acc_ref.shape, F32)
    scores_to(sa_ref, 0)

    def pair(k, _):
        g = 2 * k
        scores_to(sb_ref, g + 1)
        update_from(sa_ref, g)
        scores_to(sa_ref, g + 2)
        update_from(sb_ref, g + 1)
        return 0

    lax.fori_loop(0, gd // 2, pair, 0)

    @pl.when(gd % 2 == 1)
    def _():
        scores_to(sb_ref, gd)
        update_from(sa_ref, gd - 1)
        finish(sb_ref)

    @pl.when(gd % 2 == 0)
    def _():
        finish(sa_ref)


def _mobaattn(qt, k_aug, vt_aug, bias):
    nh, n, t = qt.shape
    gs = MOBA_BLOCK * KV_GROUP
    ng = t // gs
    hp = HEADS_PER_STEP
    return pl.pallas_call(
        _mobaattn_kernel,
        grid=(nh // hp, t // MOBA_BLOCK),
        in_specs=[pl.BlockSpec((hp, n, MOBA_BLOCK), lambda h, i: (h, 0, i)),
                  pl.BlockSpec((hp, ng, gs, K_AUG), lambda h, i: (h, 0, 0, 0)),
                  pl.BlockSpec((hp, ng, V_AUG, gs), lambda h, i: (h, 0, 0, 0)),
                  pl.BlockSpec((hp, ng, BIAS_ROWS, MOBA_BLOCK), lambda h, i: (h, 0, 0, i))],
        out_specs=pl.BlockSpec((hp, n, MOBA_BLOCK), lambda h, i: (h, 0, i)),
        out_shape=jax.ShapeDtypeStruct((nh, n, t), F32),
        scratch_shapes=[pltpu.VMEM((hp, gs, MOBA_BLOCK), F32), pltpu.VMEM((hp, gs, MOBA_BLOCK), F32),
                        pltpu.VMEM((hp, 1, MOBA_BLOCK), F32), pltpu.VMEM((hp, V_AUG, MOBA_BLOCK), F32)],
        compiler_params=_cparams("arbitrary", "arbitrary"),
        name="mobaattn",
    )(qt, k_aug, vt_aug, bias)


def _mix_kernel(ya_ref, yb_ref, sg_ref, x_ref, wa_ref, wb_ref, wo_ref, g1_ref, n2_ref, sc_ref, sh_ref,
                rw_ref, rb_ref, x1_ref, h2_ref, eid_ref, rank_ref, wt_ref, cnt_ref, base_ref):
    i = pl.program_id(0)

    @pl.when(i == 0)
    def _():
        base_ref[...] = jnp.zeros_like(base_ref)

    d = x_ref.shape[1]
    tm = x_ref.shape[0]
    pa = jnp.dot(ya_ref[...].astype(BF16), wa_ref[...], preferred_element_type=F32)
    yb = yb_ref[...].reshape(WIDTH, tm).T
    pb = jnp.dot(yb.astype(BF16), wb_ref[...], preferred_element_type=F32)
    sg = sg_ref[...]
    mixed = sg[:, :d] * pa + sg[:, d:] * pb
    mixed = jnp.dot(mixed.astype(BF16), wo_ref[...], preferred_element_type=F32)
    x1 = x_ref[...] + g1_ref[...] * mixed
    x1_ref[...] = x1
    ms = jnp.mean(x1 * x1, axis=-1, keepdims=True)
    h2 = x1 * lax.rsqrt(ms + RMS_EPS) * n2_ref[...]
    h2 = h2 * (1.0 + sc_ref[...]) + sh_ref[...]
    h2_ref[...] = h2

    logits = jnp.dot(h2, rw_ref[...], precision=HI, preferred_element_type=F32) + rb_ref[...]
    ne = logits.shape[1]
    e_id = lax.broadcasted_iota(jnp.int32, (tm, ne), 1)
    neg_inf = jnp.float32(-jnp.inf)
    work = logits
    picks, vals = [], []
    for _ in range(EXPERT_TOP_K):
        m = jnp.max(work, axis=-1, keepdims=True)
        idx = jnp.min(jnp.where(work == m, e_id, ne), axis=-1, keepdims=True)
        pick = e_id == idx
        picks.append((idx, pick))
        vals.append(m)
        work = jnp.where(pick, neg_inf, work)
    den = sum(jnp.exp(v - vals[0]) for v in vals)
    chosen = jnp.zeros((tm, ne), F32)
    for _, pick in picks:
        chosen = jnp.where(pick, 1.0, chosen)
    r_id = lax.broadcasted_iota(jnp.int32, (tm, tm), 0)
    c_id = lax.broadcasted_iota(jnp.int32, (tm, tm), 1)
    before = jnp.where(r_id > c_id, 1.0, 0.0).astype(BF16)
    ahead = jnp.dot(before, chosen.astype(BF16), preferred_element_type=F32) + base_ref[0:1, :]
    k_id = lax.broadcasted_iota(jnp.int32, (tm, EXPERT_TOP_K), 1)
    eid = jnp.zeros((tm, EXPERT_TOP_K), jnp.int32)
    rank = jnp.zeros((tm, EXPERT_TOP_K), jnp.int32)
    wt = jnp.zeros((tm, EXPERT_TOP_K), F32)
    for kk, ((idx, pick), v) in enumerate(zip(picks, vals)):
        rk = jnp.sum(jnp.where(pick, ahead, 0.0), axis=-1, keepdims=True).astype(jnp.int32)
        eid = jnp.where(k_id == kk, idx, eid)
        rank = jnp.where(k_id == kk, rk, rank)
        wt = jnp.where(k_id == kk, jnp.exp(v - vals[0]) / den, wt)
    eid_ref[...] = eid
    rank_ref[...] = rank
    wt_ref[...] = wt
    total = base_ref[0:1, :] + jnp.sum(chosen, axis=0, keepdims=True)
    base_ref[0:1, :] = total
    cnt_ref[...] = jnp.broadcast_to(total, cnt_ref.shape)


def _mix(ya, yb, sg, x2, w_a, w_b, w_o, gate1, norm2_g, scale2, shift2, router_w, router_b):
    t, d = x2.shape
    tm = 512
    ne = router_w.shape[1]
    full = lambda a: pl.BlockSpec(a.shape, lambda i: (0, 0))
    row = lambda n: pl.BlockSpec((tm, n), lambda i: (i, 0))
    return pl.pallas_call(
        _mix_kernel,
        grid=(t // tm,),
        in_specs=[row(WIDTH), pl.BlockSpec((N_HEADS, HEAD_DIM, tm), lambda i: (0, 0, i)), row(2 * d), row(d),
                  full(w_a), full(w_b), full(w_o),
                  full(gate1), full(norm2_g), full(scale2), full(shift2), full(router_w), full(router_b)],
        out_specs=[row(d), row(d), row(EXPERT_TOP_K), row(EXPERT_TOP_K), row(EXPERT_TOP_K),
                   pl.BlockSpec((8, ne), lambda i: (0, 0))],
        out_shape=[jax.ShapeDtypeStruct((t, d), F32), jax.ShapeDtypeStruct((t, d), F32),
                   jax.ShapeDtypeStruct((t, EXPERT_TOP_K), jnp.int32),
                   jax.ShapeDtypeStruct((t, EXPERT_TOP_K), jnp.int32),
                   jax.ShapeDtypeStruct((t, EXPERT_TOP_K), F32),
                   jax.ShapeDtypeStruct((8, ne), F32)],
        scratch_shapes=[pltpu.VMEM((8, ne), F32)],
        compiler_params=_cparams("arbitrary"),
        name="mix",
    )(ya, yb, sg, x2, w_a, w_b, w_o, gate1, norm2_g, scale2, shift2, router_w, router_b)


DISPATCH_TILE = 256


def _slots_kernel(off_ref, eid_ref, rank_ref, o_ref):
    eid = eid_ref[...]
    slot = rank_ref[...]
    for e in range(N_EXPERTS):
        slot = slot + jnp.where(eid == e, off_ref[e], 0)
    o_ref[...] = slot


def _slots(off, eid, rank):
    n = eid.size
    lanes = 128
    shape = (n // lanes, lanes)
    full = pl.BlockSpec(shape, lambda i: (0, 0))
    out = pl.pallas_call(
        _slots_kernel,
        grid=(1,),
        in_specs=[pl.BlockSpec(memory_space=pltpu.SMEM), full, full],
        out_specs=full,
        out_shape=jax.ShapeDtypeStruct(shape, jnp.int32),
        compiler_params=_cparams("arbitrary"),
        name="slots",
    )(off, eid.reshape(shape), rank.reshape(shape))
    return out.reshape(n)


def _dispatch_kernel(slot_ref, h_ref, xs_in_ref, xs_ref, sem):
    del xs_in_ref
    tm = DISPATCH_TILE

    def body(t, _):
        for kk in range(EXPERT_TOP_K):
            slot = slot_ref[t * EXPERT_TOP_K + kk]
            pltpu.make_async_copy(h_ref.at[pl.ds(t, 1)], xs_ref.at[pl.ds(slot, 1)], sem).start()
        return 0

    lax.fori_loop(0, tm, body, 0, unroll=2)
    for _ in range(EXPERT_TOP_K):
        pltpu.make_async_copy(h_ref, xs_ref.at[pl.ds(0, tm)], sem).wait()


def _dispatch(slot_flat, h2, n_slots):
    t, d = h2.shape
    tm = DISPATCH_TILE
    return pl.pallas_call(
        _dispatch_kernel,
        grid=(t // tm,),
        in_specs=[pl.BlockSpec((tm * EXPERT_TOP_K,), lambda i: (i,), memory_space=pltpu.SMEM),
                  pl.BlockSpec((tm, d), lambda i: (i, 0)),
                  pl.BlockSpec(memory_space=pl.ANY)],
        out_specs=pl.BlockSpec(memory_space=pl.ANY),
        out_shape=jax.ShapeDtypeStruct((n_slots, d), F32),
        scratch_shapes=[pltpu.SemaphoreType.DMA(())],
        input_output_aliases={2: 0},
        compiler_params=pltpu.CompilerParams(dimension_semantics=("arbitrary",), has_side_effects=True),
        name="dispatch",
    )(slot_flat, h2, jnp.zeros((n_slots, d), F32))


def _experts_kernel(te_ref, nu_ref, x_ref, wg_ref, bg_ref, wu_ref, bu_ref, wd_ref, bd_ref, y_ref,
                    wg_s, wu_s, wd_s):
    i = pl.program_id(0)
    used = i < nu_ref[0]
    new_expert = jnp.logical_or(i == 0, te_ref[i] != te_ref[jnp.maximum(i - 1, 0)])

    @pl.when(jnp.logical_and(used, new_expert))
    def _():
        wg_s[...] = wg_ref[0].astype(BF16)
        wu_s[...] = wu_ref[0].astype(BF16)
        wd_s[...] = wd_ref[0].astype(BF16)

    @pl.when(used)
    def _():
        x = x_ref[...].astype(BF16)
        gt = jnp.minimum(jnp.dot(x, wg_s[...], preferred_element_type=F32) + bg_ref[0], SWIGLU_LIMIT)
        up = jnp.clip(jnp.dot(x, wu_s[...], preferred_element_type=F32) + bu_ref[0], -SWIGLU_LIMIT, SWIGLU_LIMIT)
        hid = (up + 1.0) * gt * _sigmoid(SWIGLU_ALPHA * gt)
        y_ref[...] = jnp.dot(hid.astype(BF16), wd_s[...], preferred_element_type=F32) + bd_ref[0]

    @pl.when(jnp.logical_not(used))
    def _():
        y_ref[...] = jnp.zeros_like(y_ref)


def _experts(tile_expert, n_used, xs, w_gate, b_gate, w_up, b_up, w_down, b_down):
    n_slots, d = xs.shape
    f = w_gate.shape[2]
    tm = EXPERT_TILE
    n_tiles = n_slots // tm
    row = pl.BlockSpec((tm, d), lambda i, te, nu: (i, 0))
    wspec = lambda a, b: pl.BlockSpec((1, a, b), lambda i, te, nu: (te[i], 0, 0))
    grid_spec = pltpu.PrefetchScalarGridSpec(
        num_scalar_prefetch=2,
        grid=(n_tiles,),
        in_specs=[row, wspec(d, f), wspec(1, f), wspec(d, f), wspec(1, f), wspec(f, d), wspec(1, d)],
        out_specs=row,
        scratch_shapes=[pltpu.VMEM((d, f), BF16), pltpu.VMEM((d, f), BF16), pltpu.VMEM((f, d), BF16)],
    )
    return pl.pallas_call(
        _experts_kernel,
        grid_spec=grid_spec,
        out_shape=jax.ShapeDtypeStruct((n_slots, d), F32),
        compiler_params=_cparams("arbitrary"),
        name="experts",
    )(tile_expert, n_used, xs, w_gate, b_gate, w_up, b_up, w_down, b_down)


def _combine_kernel(slot_ref, ys_ref, wt_ref, x1_ref, g2_ref, nf_ref, o_ref, buf, sem):
    tm = DISPATCH_TILE

    def body(t, _):
        for kk in range(EXPERT_TOP_K):
            slot = slot_ref[t * EXPERT_TOP_K + kk]
            pltpu.make_async_copy(ys_ref.at[pl.ds(slot, 1)], buf.at[kk, pl.ds(t, 1)], sem).start()
        return 0

    lax.fori_loop(0, tm, body, 0, unroll=2)
    for kk in range(EXPERT_TOP_K):
        pltpu.make_async_copy(ys_ref.at[pl.ds(0, tm)], buf.at[kk], sem).wait()
    wt = wt_ref[...]
    moe = wt[:, 0:1] * buf[0]
    for kk in range(1, EXPERT_TOP_K):
        moe = moe + wt[:, kk:kk + 1] * buf[kk]
    x = x1_ref[...] + g2_ref[...] * moe
    ms = jnp.mean(x * x, axis=-1, keepdims=True)
    o_ref[...] = x * lax.rsqrt(ms + RMS_EPS) * nf_ref[...]


def _combine(slot_flat, ys, wt, x1, gate2, normf_g):
    t, d = x1.shape
    tm = DISPATCH_TILE
    full = lambda a: pl.BlockSpec(a.shape, lambda i: (0, 0))
    row = lambda n: pl.BlockSpec((tm, n), lambda i: (i, 0))
    return pl.pallas_call(
        _combine_kernel,
        grid=(t // tm,),
        in_specs=[pl.BlockSpec((tm * EXPERT_TOP_K,), lambda i: (i,), memory_space=pltpu.SMEM),
                  pl.BlockSpec(memory_space=pl.ANY),
                  row(EXPERT_TOP_K), row(d), full(gate2), full(normf_g)],
        out_specs=row(d),
        out_shape=jax.ShapeDtypeStruct((t, d), F32),
        scratch_shapes=[pltpu.VMEM((EXPERT_TOP_K, tm, d), F32), pltpu.SemaphoreType.DMA(())],
        compiler_params=_cparams("arbitrary"),
        name="combine",
    )(slot_flat, ys, wt, x1, gate2, normf_g)


def kernel(x, c, w_ada, b_ada, norm1_g, w_in, rwkv_mu, rwkv_w0, rwkv_w2, rwkv_a0, rwkv_a2, rwkv_g2, rwkv_k_k, rwkv_k_a, rwkv_r_k, rwkv_ln_w, rwkv_ln_b, w_branch_a, w_branch_b, w_out, norm2_g, router_w, router_b, exp_w_gate, exp_b_gate, exp_w_up, exp_b_up, exp_w_down, exp_b_down, normf_g):
    bsz, t, d = x.shape
    assert bsz == 1 and t % (MOBA_BLOCK * KV_GROUP) == 0
    x2 = x.reshape(t, d)
    row = lambda a: a.reshape(1, -1)

    mod = _mod(c, w_ada, b_ada)
    shift1, scale1, gate1, shift2, scale2, gate2 = [mod[:, j * d:(j + 1) * d] for j in range(6)]

    qkv_end = RWKV_PROJ + 3 * WIDTH
    w_in_b = w_in.astype(BF16)
    p_rwkv, sg, kmean, q_hm, qt, k_aug, vt_aug = _inproj(
        x2, row(norm1_g), scale1, shift1, w_in_b[:, :RWKV_PROJ], w_in_b[:, RWKV_PROJ:qkv_end], w_in_b[:, qkv_end:])

    zeros_lora = jnp.zeros((DECAY_LORA, WIDTH), F32)
    w2p = jnp.concatenate([rwkv_w2, zeros_lora], axis=0)
    a2p = jnp.concatenate([zeros_lora, rwkv_a2], axis=0)
    head_of = jnp.arange(WIDTH) // HEAD_DIM
    bd = (head_of[:, None] == head_of[None, :]).astype(BF16)
    r, lw, k, v, av, bv, g = _rwkvprep(p_rwkv, row(rwkv_mu), row(rwkv_w0), w2p.astype(BF16), row(rwkv_a0),
                                       a2p.astype(BF16), rwkv_g2.astype(BF16), row(rwkv_k_k), row(rwkv_k_a), bd)
    per_head = lambda a: a.reshape(N_HEADS, 1, HEAD_DIM)
    y_a = _rwkvscan(r, lw, k, v, av, bv, g, per_head(rwkv_ln_w), per_head(rwkv_ln_b), per_head(rwkv_r_k))

    nb = t // MOBA_BLOCK
    bias = _mobasel(q_hm, kmean.reshape(nb, N_HEADS, HEAD_DIM).transpose(1, 0, 2))
    y_b = _mobaattn(qt, k_aug, vt_aug, bias)

    x1, h2, eid, rank, wt, cnt = _mix(y_a, y_b, sg, x2, w_branch_a.astype(BF16), w_branch_b.astype(BF16),
                                      w_out.astype(BF16), gate1, row(norm2_g), scale2, shift2,
                                      router_w, row(router_b))

    counts = cnt[0].astype(jnp.int32)
    tiles_per = (counts + EXPERT_TILE - 1) // EXPERT_TILE
    tile_end = jnp.cumsum(tiles_per)
    off = (tile_end - tiles_per) * EXPERT_TILE
    n_tiles = (t * EXPERT_TOP_K + N_EXPERTS * (EXPERT_TILE - 1)) // EXPERT_TILE
    n_used = tile_end[-1:]
    tile_expert = jnp.minimum(jnp.sum(tile_end[None, :] <= jnp.arange(n_tiles)[:, None], axis=1),
                              N_EXPERTS - 1).astype(jnp.int32)
    last_used = tile_expert[jnp.maximum(n_used[0] - 1, 0)]
    tile_expert = jnp.where(jnp.arange(n_tiles) < n_used[0], tile_expert, last_used)

    slot_flat = _slots(off, eid, rank)
    xs = _dispatch(slot_flat, h2, n_tiles * EXPERT_TILE)
    f = exp_w_gate.shape[2]
    ys = _experts(tile_expert, n_used, xs,
                  exp_w_gate, exp_b_gate.reshape(N_EXPERTS, 1, f),
                  exp_w_up, exp_b_up.reshape(N_EXPERTS, 1, f),
                  exp_w_down, exp_b_down.reshape(N_EXPERTS, 1, d))
    out = _combine(slot_flat, ys, wt, x1, gate2, row(normf_g))
    return out.reshape(bsz, t, d)
```

```python
import jax
import jax.numpy as jnp
from jax import lax
from jax.experimental import pallas as pl
from jax.experimental.pallas import tpu as pltpu

F32 = jnp.float32
BF16 = jnp.bfloat16
HI = lax.Precision.HIGHEST

HEAD_DIM = 64
N_HEADS = 8
WIDTH = N_HEADS * HEAD_DIM
DECAY_LORA = 64
AAA_LORA = 64
GATE_LORA = 128
RWKV_PROJ = 3 * WIDTH + DECAY_LORA + AAA_LORA + GATE_LORA
GN_EPS = 64e-5
MOBA_BLOCK = 256
MOBA_TOP_K = 3
KV_GROUP = 4
BIAS_ROWS = 16
K_AUG = 128
V_AUG = HEAD_DIM + 16
HEADS_PER_STEP = 2
LOG2E = 1.4426950408889634
N_EXPERTS = 32
EXPERT_TOP_K = 4
SWIGLU_LIMIT = 7.0
SWIGLU_ALPHA = 1.702
RMS_EPS = 1e-5
GATE_NEG = -1e30
MASK_NEG = -1e30

SCAN_CHUNK = 64
SCAN_CHUNKS = 2
EXPERT_TILE = 256
VMEM_LIMIT = 56 * 1024 * 1024
TOKEN_ROWS = 8
LANES = 128


def _cparams(*sem):
    return pltpu.CompilerParams(dimension_semantics=sem, vmem_limit_bytes=VMEM_LIMIT)


def _sigmoid(x):
    return 1.0 / (1.0 + jnp.exp(-x))


def _mod_kernel(c_ref, w_ref, b_ref, o_ref):
    c = c_ref[...]
    s = c * _sigmoid(c)
    o_ref[...] = jnp.dot(s, w_ref[...], precision=HI, preferred_element_type=F32) + b_ref[...]


def _mod(c, w_ada, b_ada):
    d = c.shape[-1]
    n = w_ada.shape[1]
    c8 = jnp.broadcast_to(c[:1], (8, d))
    tn = 1024
    out = pl.pallas_call(
        _mod_kernel,
        grid=(n // tn,),
        in_specs=[pl.BlockSpec((8, d), lambda j: (0, 0)),
                  pl.BlockSpec((d, tn), lambda j: (0, j)),
                  pl.BlockSpec((1, tn), lambda j: (0, j))],
        out_specs=pl.BlockSpec((8, tn), lambda j: (0, j)),
        out_shape=jax.ShapeDtypeStruct((8, n), F32),
        compiler_params=_cparams("arbitrary"),
        name="mod",
    )(c8, w_ada, b_ada.reshape(1, n))
    return out[:1]


def _inproj_kernel(x_ref, g_ref, sc_ref, sh_ref, wr_ref, wq_ref, wg_ref,
                   pr_ref, sg_ref, km_ref, qh_ref, qt_ref, ka_ref, va_ref):
    i = pl.program_id(0)
    x = x_ref[...]
    tm = x.shape[0]
    ms = jnp.mean(x * x, axis=-1, keepdims=True)
    h = x * lax.rsqrt(ms + RMS_EPS) * g_ref[...]
    h = h * (1.0 + sc_ref[...]) + sh_ref[...]
    hb = h.astype(BF16)
    pr_ref[...] = jnp.dot(hb, wr_ref[...], preferred_element_type=F32)
    sg_ref[...] = _sigmoid(jnp.dot(hb, wg_ref[...], preferred_element_type=F32))
    qkv = jnp.dot(hb, wq_ref[...], preferred_element_type=F32)
    km_ref[0] = jnp.mean(qkv[:, WIDTH:2 * WIDTH], axis=0, keepdims=True)
    col = lax.broadcasted_iota(jnp.int32, (tm, K_AUG - HEAD_DIM), 1)
    indicator = jnp.where(col == i % KV_GROUP, 1.0, 0.0).astype(BF16)
    row = lax.broadcasted_iota(jnp.int32, (V_AUG - HEAD_DIM, tm), 0)
    ones_row = jnp.where(row == 0, 1.0, 0.0).astype(BF16)
    for hd in range(N_HEADS):
        lo = hd * HEAD_DIM
        q = qkv[:, lo:lo + HEAD_DIM]
        kh = qkv[:, WIDTH + lo:WIDTH + lo + HEAD_DIM]
        vh = qkv[:, 2 * WIDTH + lo:2 * WIDTH + lo + HEAD_DIM]
        qh_ref[hd] = q
        qt_ref[hd] = (q * (HEAD_DIM ** -0.5 * LOG2E)).T.astype(BF16)
        ka_ref[hd, 0] = jnp.concatenate([kh.astype(BF16), indicator], axis=1)
        va_ref[hd, 0] = jnp.concatenate([vh.T.astype(BF16), ones_row], axis=0)


def _inproj(x2, norm_g, scale, shift, w_r, w_q, w_g):
    t, d = x2.shape
    tm = MOBA_BLOCK
    gs = MOBA_BLOCK * KV_GROUP
    ng = t // gs
    full = lambda a: pl.BlockSpec(a.shape, lambda i: (0, 0))
    row = lambda n: pl.BlockSpec((tm, n), lambda i: (i, 0))
    return pl.pallas_call(
        _inproj_kernel,
        grid=(t // tm,),
        in_specs=[row(d), full(norm_g), full(scale), full(shift), full(w_r), full(w_q), full(w_g)],
        out_specs=[row(w_r.shape[1]), row(w_g.shape[1]),
                   pl.BlockSpec((1, 1, WIDTH), lambda i: (i, 0, 0)),
                   pl.BlockSpec((N_HEADS, tm, HEAD_DIM), lambda i: (0, i, 0)),
                   pl.BlockSpec((N_HEADS, HEAD_DIM, tm), lambda i: (0, 0, i)),
                   pl.BlockSpec((N_HEADS, 1, tm, K_AUG), lambda i: (0, i // KV_GROUP, i % KV_GROUP, 0)),
                   pl.BlockSpec((N_HEADS, 1, V_AUG, tm), lambda i: (0, i // KV_GROUP, 0, i % KV_GROUP))],
        out_shape=[jax.ShapeDtypeStruct((t, w_r.shape[1]), F32),
                   jax.ShapeDtypeStruct((t, w_g.shape[1]), F32),
                   jax.ShapeDtypeStruct((t // tm, 1, WIDTH), F32),
                   jax.ShapeDtypeStruct((N_HEADS, t, HEAD_DIM), F32),
                   jax.ShapeDtypeStruct((N_HEADS, HEAD_DIM, t), BF16),
                   jax.ShapeDtypeStruct((N_HEADS, ng, gs, K_AUG), BF16),
                   jax.ShapeDtypeStruct((N_HEADS, ng, V_AUG, gs), BF16)],
        compiler_params=_cparams("arbitrary"),
        name="inproj",
    )(x2, norm_g, scale, shift, w_r, w_q, w_g)


def _rwkvprep_kernel(p_ref, mu_ref, w0_ref, w2_ref, a0_ref, a2_ref, g2_ref, kk_ref, ka_ref, bd_ref,
                     r_out, lw_out, k_out, v_out, a_out, b_out, g_out, prev_ref):
    i = pl.program_id(0)

    @pl.when(i == 0)
    def _():
        prev_ref[...] = jnp.zeros_like(prev_ref)

    p = p_ref[...]
    tm = p.shape[0]
    rolled = pltpu.roll(p, 1, 0)
    row0 = lax.broadcasted_iota(jnp.int32, p.shape, 0) == 0
    p_prev = jnp.where(row0, prev_ref[0:1, :], rolled)
    prev_ref[0:1, :] = p[tm - 1:tm, :]
    ps = p + (p_prev - p) * mu_ref[...]
    r = ps[:, 0:WIDTH]
    k = ps[:, WIDTH:2 * WIDTH]
    v = ps[:, 2 * WIDTH:3 * WIDTH]
    xwa = ps[:, 3 * WIDTH:3 * WIDTH + DECAY_LORA + AAA_LORA]
    xg = ps[:, 3 * WIDTH + DECAY_LORA + AAA_LORA:RWKV_PROJ]
    dot = lambda lhs, w_ref: jnp.dot(lhs.astype(BF16), w_ref[...], preferred_element_type=F32)
    z = w0_ref[...] + dot(jnp.tanh(xwa), w2_ref)
    nz = -z
    softplus = jnp.maximum(nz, 0.0) + jnp.log(1.0 + jnp.exp(-jnp.abs(nz)))
    w = -softplus - 0.5
    a = _sigmoid(a0_ref[...] + dot(xwa, a2_ref))
    g = dot(_sigmoid(xg), g2_ref)
    kk = k * kk_ref[...]
    sq = kk * kk
    sq_hi = sq.astype(BF16)
    ss = dot(sq_hi, bd_ref) + dot(sq - sq_hi.astype(F32), bd_ref)
    kk = kk / jnp.maximum(jnp.sqrt(ss), 1e-12)
    outs = ((r_out, r), (lw_out, -jnp.exp(w)),
            (k_out, k * (1.0 + (a - 1.0) * ka_ref[...])), (v_out, v), (a_out, -kk), (b_out, kk * a), (g_out, g))
    for ref, val in outs:
        for hd in range(N_HEADS):
            ref[hd] = val[:, hd * HEAD_DIM:(hd + 1) * HEAD_DIM]


def _rwkvprep(p_rwkv, mu, w0, w2p, a0, a2p, g2, k_k, k_a, bd):
    t = p_rwkv.shape[0]
    tm = 256
    full = lambda a: pl.BlockSpec(a.shape, lambda i: (0, 0))
    row = lambda n: pl.BlockSpec((tm, n), lambda i: (i, 0))
    outs = [jax.ShapeDtypeStruct((N_HEADS, t, HEAD_DIM), F32)] * 7
    return pl.pallas_call(
        _rwkvprep_kernel,
        grid=(t // tm,),
        in_specs=[row(RWKV_PROJ), full(mu), full(w0), full(w2p), full(a0), full(a2p), full(g2),
                  full(k_k), full(k_a), full(bd)],
        out_specs=[pl.BlockSpec((N_HEADS, tm, HEAD_DIM), lambda i: (0, i, 0))] * 7,
        out_shape=outs,
        scratch_shapes=[pltpu.VMEM((8, RWKV_PROJ), F32)],
        compiler_params=_cparams("arbitrary"),
        name="rwkvprep",
    )(p_rwkv, mu, w0, w2p, a0, a2p, g2, k_k, k_a, bd)


def _bmm(a, b):
    return jnp.einsum("hmk,hkn->hmn", a.astype(BF16), b.astype(BF16), preferred_element_type=F32)


def _bmm_nt(a, b):
    return jnp.einsum("hmk,hnk->hmn", a.astype(BF16), b.astype(BF16), preferred_element_type=F32)


def _bmm_tn(a, b):
    return jnp.einsum("hkm,hkn->hmn", a.astype(BF16), b.astype(BF16), preferred_element_type=F32)


def _rwkvscan_kernel(r_ref, lw_ref, k_ref, v_ref, a_ref, b_ref, g_ref, lnw_ref, lnb_ref, rk_ref,
                     y_ref, s_ref):
    i = pl.program_id(0)

    @pl.when(i == 0)
    def _():
        s_ref[...] = jnp.zeros_like(s_ref)

    c = SCAN_CHUNK
    nc = SCAN_CHUNKS
    nh, _, n = r_ref.shape
    nb = nh * nc
    split = lambda ref: ref[...].reshape(nb, c, n)
    r, lw, k, v, a, b = (split(ref) for ref in (r_ref, lw_ref, k_ref, v_ref, a_ref, b_ref))

    row = lax.broadcasted_iota(jnp.int32, (nb, c, c), 1)
    col = lax.broadcasted_iota(jnp.int32, (nb, c, c), 2)
    lower = row >= col
    strict = row > col
    cum = jnp.einsum("hmk,hkn->hmn", lower.astype(F32), lw, precision=HI,
                     preferred_element_type=F32)
    tot = cum[:, c - 1:c, :]
    g_in = jnp.exp(cum)
    g_ex = jnp.exp(cum - lw)
    g_inv = jnp.exp(-cum)
    g_rem = jnp.exp(tot - cum)
    a_t = a * g_ex
    r_t = r * g_in
    b_t = b * g_inv
    k_t = k * g_inv
    l_ab = jnp.where(strict, _bmm_nt(a_t, b_t), 0.0)
    l_ak = jnp.where(strict, _bmm_nt(a_t, k_t), 0.0)
    m_rb = jnp.where(lower, _bmm_nt(r_t, b_t), 0.0)
    m_rk = jnp.where(lower, _bmm_nt(r_t, k_t), 0.0)
    same16 = (row // 16) == (col // 16)
    same32 = (row // 32) == (col // 32)
    diag16 = jnp.where(same16, l_ab, 0.0)
    inv = jnp.where(row == col, 1.0, 0.0) + diag16
    lp = diag16
    for _ in range(3):
        lp = _bmm(lp, lp)
        inv = inv + _bmm(inv, lp)
    off32 = jnp.where(jnp.logical_and(same32, jnp.logical_not(same16)), l_ab, 0.0)
    inv = inv + _bmm(_bmm(inv, off32), inv)
    off64 = jnp.where(same32, 0.0, l_ab)
    inv = inv + _bmm(_bmm(inv, off64), inv)
    a_hat = _bmm(inv, a_t)
    w_hat = _bmm(inv, _bmm(l_ak, v))
    r_hat = r_t + _bmm(m_rb, a_hat)
    y0 = _bmm(m_rb, w_hat) + _bmm(m_rk, v)
    b_hat = b * g_rem
    k_hat = k * g_rem
    decay = jnp.exp(tot)

    chunk = lambda x, j: x.reshape(nh, nc, x.shape[1], n)[:, j]
    s = s_ref[...]
    ys = []
    for j in range(nc):
        u = _bmm_nt(chunk(a_hat, j), s) + chunk(w_hat, j)
        ys.append(_bmm_nt(chunk(r_hat, j), s) + chunk(y0, j))
        s = s * chunk(decay, j) + _bmm_tn(u, chunk(b_hat, j)) + _bmm_tn(chunk(v, j), chunk(k_hat, j))
    s_ref[...] = s
    y = jnp.concatenate(ys, axis=1)

    r = r_ref[...]
    mean = jnp.mean(y, axis=-1, keepdims=True)
    yc = y - mean
    var = jnp.mean(yc * yc, axis=-1, keepdims=True)
    yn = yc * lax.rsqrt(var + GN_EPS) * lnw_ref[...] + lnb_ref[...]
    bonus = jnp.sum(r * k_ref[...] * rk_ref[...], axis=-1, keepdims=True) * v_ref[...]
    out = (yn + bonus) * g_ref[...]
    for hd in range(nh):
        y_ref[:, hd * n:(hd + 1) * n] = out[hd]


def _rwkvscan(r, lw, k, v, a, b, g, ln_w, ln_b, r_k):
    nh, t, n = r.shape
    rows = SCAN_CHUNK * SCAN_CHUNKS
    blk = pl.BlockSpec((nh, rows, n), lambda i: (0, i, 0))
    par = pl.BlockSpec((nh, 1, n), lambda i: (0, 0, 0))
    return pl.pallas_call(
        _rwkvscan_kernel,
        grid=(t // rows,),
        in_specs=[blk] * 7 + [par] * 3,
        out_specs=pl.BlockSpec((rows, nh * n), lambda i: (i, 0)),
        out_shape=jax.ShapeDtypeStruct((t, nh * n), F32),
        scratch_shapes=[pltpu.VMEM((nh, n, n), F32)],
        compiler_params=_cparams("arbitrary"),
        name="rwkvscan",
    )(r, lw, k, v, a, b, g, ln_w, ln_b, r_k)


def _mobasel_kernel(q_ref, km_ref, o_ref):
    j = pl.program_id(1)
    q = q_ref[0]
    km = km_ref[0]
    nb = km.shape[0]
    tq = q.shape[0]
    gate = lax.dot_general(km, q, (((1,), (1,)), ((), ())), precision=HI, preferred_element_type=F32)
    n_id = lax.broadcasted_iota(jnp.int32, (nb, tq), 0)
    t_id = lax.broadcasted_iota(jnp.int32, (nb, tq), 1) + j * tq
    q_blk = t_id // MOBA_BLOCK
    cand = n_id < nb - 1
    past = n_id < q_blk
    neg_inf = jnp.float32(-jnp.inf)
    gate = jnp.where(cand, jnp.where(past, gate, GATE_NEG), neg_inf)
    sel = jnp.zeros((nb, tq), F32)
    for _ in range(min(MOBA_TOP_K, nb - 1)):
        m = jnp.max(gate, axis=0, keepdims=True)
        idx = jnp.min(jnp.where(gate == m, n_id, nb), axis=0, keepdims=True)
        pick = n_id == idx
        sel = jnp.where(pick, 1.0, sel)
        gate = jnp.where(pick, neg_inf, gate)
    bias = jnp.where(jnp.logical_or(jnp.logical_and(past, sel > 0.5), n_id == q_blk), 0.0, MASK_NEG)
    pad = jnp.zeros((BIAS_ROWS - KV_GROUP, tq), F32)
    for g in range(nb // KV_GROUP):
        o_ref[0, g] = jnp.concatenate([bias[g * KV_GROUP:(g + 1) * KV_GROUP], pad], axis=0).astype(BF16)


def _mobasel(q_hm, kmean):
    nh, t, n = q_hm.shape
    nb = kmean.shape[1]
    ng = nb // KV_GROUP
    tq = min(t, 2048)
    return pl.pallas_call(
        _mobasel_kernel,
        grid=(nh, t // tq),
        in_specs=[pl.BlockSpec((1, tq, n), lambda h, j: (h, j, 0)),
                  pl.BlockSpec((1, nb, n), lambda h, j: (h, 0, 0))],
        out_specs=pl.BlockSpec((1, ng, BIAS_ROWS, tq), lambda h, j: (h, 0, 0, j)),
        out_shape=jax.ShapeDtypeStruct((nh, ng, BIAS_ROWS, t), BF16),
        compiler_params=_cparams("arbitrary", "arbitrary"),
        name="mobasel",
    )(q_hm, kmean)


def _mobaattn_kernel(qt_ref, k_ref, vt_ref, bias_ref, o_ref, sa_ref, sb_ref, m_ref, acc_ref):
    i = pl.program_id(1)
    bs = MOBA_BLOCK
    gd = i // KV_GROUP
    q_pad = jnp.zeros((K_AUG - HEAD_DIM - BIAS_ROWS, bs), BF16)
    heads = range(HEADS_PER_STEP)

    def scores_to(dst, g):
        for hh in heads:
            q_aug = jnp.concatenate([qt_ref[hh], bias_ref[hh, g], q_pad], axis=0)
            dst[hh] = jnp.dot(k_ref[hh, g], q_aug, preferred_element_type=F32)

    def update_from(src, g, keep=None):
        for hh in heads:
            s = src[hh] if keep is None else jnp.where(keep, src[hh], MASK_NEG)
            m = m_ref[hh]
            m_new = jnp.maximum(m, jnp.max(s, axis=0, keepdims=True))
            alpha = jnp.exp2(m - m_new)
            p = jnp.exp2((s - m_new).astype(BF16))
            acc_ref[hh] = alpha * acc_ref[hh] + jnp.dot(vt_ref[hh, g], p, preferred_element_type=F32)
            m_ref[hh] = m_new

    def finish(src):
        causal = lax.broadcasted_iota(jnp.int32, (bs, bs), 0) <= lax.broadcasted_iota(jnp.int32, (bs, bs), 1)
        own = jnp.concatenate([jnp.logical_or(causal, i % KV_GROUP != b) for b in range(KV_GROUP)], axis=0)
        update_from(src, gd, own)
        for hh in heads:
            acc = acc_ref[hh]
            o_ref[hh] = acc[:HEAD_DIM] / acc[HEAD_DIM:HEAD_DIM + 1]

    m_ref[...] = jnp.full(m_ref.shape, MASK_NEG, F32)
    acc_ref[...] = jnp.zeros(acc_ref.shape, F32)
    scores_to(sa_ref, 0)

    def pair(k, _):
        g = 2 * k
        scores_to(sb_ref, g + 1)
        update_from(sa_ref, g)
        scores_to(sa_ref, g + 2)
        update_from(sb_ref, g + 1)
        return 0

    lax.fori_loop(0, gd // 2, pair, 0)

    @pl.when(gd % 2 == 1)
    def _():
        scores_to(sb_ref, gd)
        update_from(sa_ref, gd - 1)
        finish(sb_ref)

    @pl.when(gd % 2 == 0)
    def _():
        finish(sa_ref)


def _mobaattn(qt, k_aug, vt_aug, bias):
    nh, n, t = qt.shape
    gs = MOBA_BLOCK * KV_GROUP
    ng = t // gs
    hp = HEADS_PER_STEP
    return pl.pallas_call(
        _mobaattn_kernel,
        grid=(nh // hp, t // MOBA_BLOCK),
        in_specs=[pl.BlockSpec((hp, n, MOBA_BLOCK), lambda h, i: (h, 0, i)),
                  pl.BlockSpec((hp, ng, gs, K_AUG), lambda h, i: (h, 0, 0, 0)),
                  pl.BlockSpec((hp, ng, V_AUG, gs), lambda h, i: (h, 0, 0, 0)),
                  pl.BlockSpec((hp, ng, BIAS_ROWS, MOBA_BLOCK), lambda h, i: (h, 0, 0, i))],
        out_specs=pl.BlockSpec((hp, n, MOBA_BLOCK), lambda h, i: (h, 0, i)),
        out_shape=jax.ShapeDtypeStruct((nh, n, t), F32),
        scratch_shapes=[pltpu.VMEM((hp, gs, MOBA_BLOCK), F32), pltpu.VMEM((hp, gs, MOBA_BLOCK), F32),
                        pltpu.VMEM((hp, 1, MOBA_BLOCK), F32), pltpu.VMEM((hp, V_AUG, MOBA_BLOCK), F32)],
        compiler_params=_cparams("arbitrary", "arbitrary"),
        name="mobaattn",
    )(qt, k_aug, vt_aug, bias)


def _load_token_tiles(ref, n_tokens, lead=()):
    chunks = [ref[lead + (pl.ds(ch, n_tokens, stride=TOKEN_ROWS), slice(None))] for ch in range(TOKEN_ROWS)]
    return jnp.concatenate(chunks, axis=1)


def _store_token_tiles(ref, val):
    for ch in range(TOKEN_ROWS):
        ref[pl.ds(ch, val.shape[0], stride=TOKEN_ROWS), :] = val[:, ch * LANES:(ch + 1) * LANES]


def _mix_kernel(ya_ref, yb_ref, sg_ref, x_ref, wa_ref, wb_ref, wo_ref, g1_ref, n2_ref, sc_ref, sh_ref,
                rw_ref, rb_ref, x1_ref, h2_ref, eid_ref, rank_ref, wt_ref, cnt_ref, base_ref):
    i = pl.program_id(0)

    @pl.when(i == 0)
    def _():
        base_ref[...] = jnp.zeros_like(base_ref)

    d = x_ref.shape[1]
    tm = x_ref.shape[0]
    pa = jnp.dot(ya_ref[...].astype(BF16), wa_ref[...], preferred_element_type=F32)
    yb = yb_ref[...].reshape(WIDTH, tm).T
    pb = jnp.dot(yb.astype(BF16), wb_ref[...], preferred_element_type=F32)
    sg = sg_ref[...]
    mixed = sg[:, :d] * pa + sg[:, d:] * pb
    mixed = jnp.dot(mixed.astype(BF16), wo_ref[...], preferred_element_type=F32)
    x1 = x_ref[...] + g1_ref[...] * mixed
    x1_ref[...] = x1
    ms = jnp.mean(x1 * x1, axis=-1, keepdims=True)
    h2 = x1 * lax.rsqrt(ms + RMS_EPS) * n2_ref[...]
    h2 = h2 * (1.0 + sc_ref[...]) + sh_ref[...]
    _store_token_tiles(h2_ref, h2)

    logits = jnp.dot(h2, rw_ref[...], precision=HI, preferred_element_type=F32) + rb_ref[...]
    ne = logits.shape[1]
    e_id = lax.broadcasted_iota(jnp.int32, (tm, ne), 1)
    neg_inf = jnp.float32(-jnp.inf)
    work = logits
    picks, vals = [], []
    for _ in range(EXPERT_TOP_K):
        m = jnp.max(work, axis=-1, keepdims=True)
        idx = jnp.min(jnp.where(work == m, e_id, ne), axis=-1, keepdims=True)
        pick = e_id == idx
        picks.append((idx, pick))
        vals.append(m)
        work = jnp.where(pick, neg_inf, work)
    den = sum(jnp.exp(v - vals[0]) for v in vals)
    chosen = jnp.zeros((tm, ne), F32)
    for _, pick in picks:
        chosen = jnp.where(pick, 1.0, chosen)
    r_id = lax.broadcasted_iota(jnp.int32, (tm, tm), 0)
    c_id = lax.broadcasted_iota(jnp.int32, (tm, tm), 1)
    before = jnp.where(r_id > c_id, 1.0, 0.0).astype(BF16)
    ahead = jnp.dot(before, chosen.astype(BF16), preferred_element_type=F32) + base_ref[0:1, :]
    k_id = lax.broadcasted_iota(jnp.int32, (tm, EXPERT_TOP_K), 1)
    eid = jnp.zeros((tm, EXPERT_TOP_K), jnp.int32)
    rank = jnp.zeros((tm, EXPERT_TOP_K), jnp.int32)
    wt = jnp.zeros((tm, EXPERT_TOP_K), F32)
    for kk, ((idx, pick), v) in enumerate(zip(picks, vals)):
        rk = jnp.sum(jnp.where(pick, ahead, 0.0), axis=-1, keepdims=True).astype(jnp.int32)
        eid = jnp.where(k_id == kk, idx, eid)
        rank = jnp.where(k_id == kk, rk, rank)
        wt = jnp.where(k_id == kk, jnp.exp(v - vals[0]) / den, wt)
    eid_ref[...] = eid
    rank_ref[...] = rank
    wt_ref[...] = wt
    total = base_ref[0:1, :] + jnp.sum(chosen, axis=0, keepdims=True)
    base_ref[0:1, :] = total
    cnt_ref[...] = jnp.broadcast_to(total, cnt_ref.shape)


def _mix(ya, yb, sg, x2, w_a, w_b, w_o, gate1, norm2_g, scale2, shift2, router_w, router_b):
    t, d = x2.shape
    tm = 512
    ne = router_w.shape[1]
    full = lambda a: pl.BlockSpec(a.shape, lambda i: (0, 0))
    row = lambda n: pl.BlockSpec((tm, n), lambda i: (i, 0))
    return pl.pallas_call(
        _mix_kernel,
        grid=(t // tm,),
        in_specs=[row(WIDTH), pl.BlockSpec((N_HEADS, HEAD_DIM, tm), lambda i: (0, 0, i)), row(2 * d), row(d),
                  full(w_a), full(w_b), full(w_o),
                  full(gate1), full(norm2_g), full(scale2), full(shift2), full(router_w), full(router_b)],
        out_specs=[row(d), pl.BlockSpec((tm * d // LANES, LANES), lambda i: (i, 0)),
                   row(EXPERT_TOP_K), row(EXPERT_TOP_K), row(EXPERT_TOP_K),
                   pl.BlockSpec((8, ne), lambda i: (0, 0))],
        out_shape=[jax.ShapeDtypeStruct((t, d), F32), jax.ShapeDtypeStruct((t * d // LANES, LANES), F32),
                   jax.ShapeDtypeStruct((t, EXPERT_TOP_K), jnp.int32),
                   jax.ShapeDtypeStruct((t, EXPERT_TOP_K), jnp.int32),
                   jax.ShapeDtypeStruct((t, EXPERT_TOP_K), F32),
                   jax.ShapeDtypeStruct((8, ne), F32)],
        scratch_shapes=[pltpu.VMEM((8, ne), F32)],
        compiler_params=_cparams("arbitrary"),
        name="mix",
    )(ya, yb, sg, x2, w_a, w_b, w_o, gate1, norm2_g, scale2, shift2, router_w, router_b)


DISPATCH_TILE = 256


def _slots_kernel(off_ref, eid_ref, rank_ref, o_ref):
    eid = eid_ref[...]
    slot = rank_ref[...]
    for e in range(N_EXPERTS):
        slot = slot + jnp.where(eid == e, off_ref[e], 0)
    o_ref[...] = slot * TOKEN_ROWS


def _slots(off, eid, rank):
    n = eid.size
    lanes = 128
    shape = (n // lanes, lanes)
    full = pl.BlockSpec(shape, lambda i: (0, 0))
    out = pl.pallas_call(
        _slots_kernel,
        grid=(1,),
        in_specs=[pl.BlockSpec(memory_space=pltpu.SMEM), full, full],
        out_specs=full,
        out_shape=jax.ShapeDtypeStruct(shape, jnp.int32),
        compiler_params=_cparams("arbitrary"),
        name="slots",
    )(off, eid.reshape(shape), rank.reshape(shape))
    return out.reshape(n)


def _dispatch_kernel(slot_ref, h_ref, xs_in_ref, xs_ref, sem):
    del xs_in_ref
    tm = DISPATCH_TILE

    def body(t, _):
        for kk in range(EXPERT_TOP_K):
            row = pl.multiple_of(slot_ref[t * EXPERT_TOP_K + kk], TOKEN_ROWS)
            pltpu.make_async_copy(h_ref.at[pl.ds(t * TOKEN_ROWS, TOKEN_ROWS)], xs_ref.at[pl.ds(row, TOKEN_ROWS)],
                                  sem).start()
        return 0

    lax.fori_loop(0, tm, body, 0, unroll=2)
    for _ in range(EXPERT_TOP_K):
        pltpu.make_async_copy(h_ref, xs_ref.at[pl.ds(0, tm * TOKEN_ROWS)], sem).wait()


def _dispatch(slot_flat, h2, n_slots):
    rows, lanes = h2.shape
    tm = DISPATCH_TILE
    t = rows // TOKEN_ROWS
    return pl.pallas_call(
        _dispatch_kernel,
        grid=(t // tm,),
        in_specs=[pl.BlockSpec((tm * EXPERT_TOP_K,), lambda i: (i,), memory_space=pltpu.SMEM),
                  pl.BlockSpec((tm * TOKEN_ROWS, lanes), lambda i: (i, 0)),
                  pl.BlockSpec(memory_space=pl.ANY)],
        out_specs=pl.BlockSpec(memory_space=pl.ANY),
        out_shape=jax.ShapeDtypeStruct((n_slots * TOKEN_ROWS, lanes), F32),
        scratch_shapes=[pltpu.SemaphoreType.DMA(())],
        input_output_aliases={2: 0},
        compiler_params=pltpu.CompilerParams(dimension_semantics=("arbitrary",), has_side_effects=True),
        name="dispatch",
    )(slot_flat, h2, jnp.zeros((n_slots * TOKEN_ROWS, lanes), F32))


def _experts_kernel(te_ref, nu_ref, x_ref, wg_ref, bg_ref, wu_ref, bu_ref, wd_ref, bd_ref, y_ref,
                    wg_s, wu_s, wd_s):
    i = pl.program_id(0)
    used = i < nu_ref[0]
    new_expert = jnp.logical_or(i == 0, te_ref[i] != te_ref[jnp.maximum(i - 1, 0)])

    @pl.when(jnp.logical_and(used, new_expert))
    def _():
        wg_s[...] = wg_ref[0].astype(BF16)
        wu_s[...] = wu_ref[0].astype(BF16)
        wd_s[...] = wd_ref[0].astype(BF16)

    @pl.when(used)
    def _():
        x = _load_token_tiles(x_ref, EXPERT_TILE).astype(BF16)
        gt = jnp.minimum(jnp.dot(x, wg_s[...], preferred_element_type=F32) + bg_ref[0], SWIGLU_LIMIT)
        up = jnp.clip(jnp.dot(x, wu_s[...], preferred_element_type=F32) + bu_ref[0], -SWIGLU_LIMIT, SWIGLU_LIMIT)
        hid = (up + 1.0) * gt * _sigmoid(SWIGLU_ALPHA * gt)
        _store_token_tiles(y_ref, jnp.dot(hid.astype(BF16), wd_s[...], preferred_element_type=F32) + bd_ref[0])

    @pl.when(jnp.logical_not(used))
    def _():
        y_ref[...] = jnp.zeros_like(y_ref)


def _experts(tile_expert, n_used, xs, w_gate, b_gate, w_up, b_up, w_down, b_down):
    rows, lanes = xs.shape
    d = TOKEN_ROWS * lanes
    f = w_gate.shape[2]
    tm = EXPERT_TILE
    n_tiles = rows // (tm * TOKEN_ROWS)
    row = pl.BlockSpec((tm * TOKEN_ROWS, lanes), lambda i, te, nu: (i, 0))
    wspec = lambda a, b: pl.BlockSpec((1, a, b), lambda i, te, nu: (te[i], 0, 0))
    grid_spec = pltpu.PrefetchScalarGridSpec(
        num_scalar_prefetch=2,
        grid=(n_tiles,),
        in_specs=[row, wspec(d, f), wspec(1, f), wspec(d, f), wspec(1, f), wspec(f, d), wspec(1, d)],
        out_specs=row,
        scratch_shapes=[pltpu.VMEM((d, f), BF16), pltpu.VMEM((d, f), BF16), pltpu.VMEM((f, d), BF16)],
    )
    return pl.pallas_call(
        _experts_kernel,
        grid_spec=grid_spec,
        out_shape=jax.ShapeDtypeStruct((rows, lanes), F32),
        compiler_params=_cparams("arbitrary"),
        name="experts",
    )(tile_expert, n_used, xs, w_gate, b_gate, w_up, b_up, w_down, b_down)


def _combine_kernel(slot_ref, ys_ref, wt_ref, x1_ref, g2_ref, nf_ref, o_ref, buf, sem):
    tm = DISPATCH_TILE

    def body(t, _):
        for kk in range(EXPERT_TOP_K):
            row = pl.multiple_of(slot_ref[t * EXPERT_TOP_K + kk], TOKEN_ROWS)
            pltpu.make_async_copy(ys_ref.at[pl.ds(row, TOKEN_ROWS)], buf.at[kk, pl.ds(t * TOKEN_ROWS, TOKEN_ROWS)],
                                  sem).start()
        return 0

    lax.fori_loop(0, tm, body, 0, unroll=2)
    for kk in range(EXPERT_TOP_K):
        pltpu.make_async_copy(ys_ref.at[pl.ds(0, tm * TOKEN_ROWS)], buf.at[kk], sem).wait()
    wt = wt_ref[...]
    moe = wt[:, 0:1] * _load_token_tiles(buf, tm, (0,))
    for kk in range(1, EXPERT_TOP_K):
        moe = moe + wt[:, kk:kk + 1] * _load_token_tiles(buf, tm, (kk,))
    x = x1_ref[...] + g2_ref[...] * moe
    ms = jnp.mean(x * x, axis=-1, keepdims=True)
    o_ref[...] = x * lax.rsqrt(ms + RMS_EPS) * nf_ref[...]


def _combine(slot_flat, ys, wt, x1, gate2, normf_g):
    t, d = x1.shape
    tm = DISPATCH_TILE
    full = lambda a: pl.BlockSpec(a.shape, lambda i: (0, 0))
    row = lambda n: pl.BlockSpec((tm, n), lambda i: (i, 0))
    return pl.pallas_call(
        _combine_kernel,
        grid=(t // tm,),
        in_specs=[pl.BlockSpec((tm * EXPERT_TOP_K,), lambda i: (i,), memory_space=pltpu.SMEM),
                  pl.BlockSpec(memory_space=pl.ANY),
                  row(EXPERT_TOP_K), row(d), full(gate2), full(normf_g)],
        out_specs=row(d),
        out_shape=jax.ShapeDtypeStruct((t, d), F32),
        scratch_shapes=[pltpu.VMEM((EXPERT_TOP_K, tm * TOKEN_ROWS, LANES), F32), pltpu.SemaphoreType.DMA(())],
        compiler_params=_cparams("arbitrary"),
        name="combine",
    )(slot_flat, ys, wt, x1, gate2, normf_g)


def kernel(x, c, w_ada, b_ada, norm1_g, w_in, rwkv_mu, rwkv_w0, rwkv_w2, rwkv_a0, rwkv_a2, rwkv_g2, rwkv_k_k, rwkv_k_a, rwkv_r_k, rwkv_ln_w, rwkv_ln_b, w_branch_a, w_branch_b, w_out, norm2_g, router_w, router_b, exp_w_gate, exp_b_gate, exp_w_up, exp_b_up, exp_w_down, exp_b_down, normf_g):
    bsz, t, d = x.shape
    assert bsz == 1 and t % (MOBA_BLOCK * KV_GROUP) == 0
    x2 = x.reshape(t, d)
    row = lambda a: a.reshape(1, -1)

    mod = _mod(c, w_ada, b_ada)
    shift1, scale1, gate1, shift2, scale2, gate2 = [mod[:, j * d:(j + 1) * d] for j in range(6)]

    qkv_end = RWKV_PROJ + 3 * WIDTH
    w_in_b = w_in.astype(BF16)
    p_rwkv, sg, kmean, q_hm, qt, k_aug, vt_aug = _inproj(
        x2, row(norm1_g), scale1, shift1, w_in_b[:, :RWKV_PROJ], w_in_b[:, RWKV_PROJ:qkv_end], w_in_b[:, qkv_end:])

    zeros_lora = jnp.zeros((DECAY_LORA, WIDTH), F32)
    w2p = jnp.concatenate([rwkv_w2, zeros_lora], axis=0)
    a2p = jnp.concatenate([zeros_lora, rwkv_a2], axis=0)
    head_of = jnp.arange(WIDTH) // HEAD_DIM
    bd = (head_of[:, None] == head_of[None, :]).astype(BF16)
    r, lw, k, v, av, bv, g = _rwkvprep(p_rwkv, row(rwkv_mu), row(rwkv_w0), w2p.astype(BF16), row(rwkv_a0),
                                       a2p.astype(BF16), rwkv_g2.astype(BF16), row(rwkv_k_k), row(rwkv_k_a), bd)
    per_head = lambda a: a.reshape(N_HEADS, 1, HEAD_DIM)
    y_a = _rwkvscan(r, lw, k, v, av, bv, g, per_head(rwkv_ln_w), per_head(rwkv_ln_b), per_head(rwkv_r_k))

    nb = t // MOBA_BLOCK
    bias = _mobasel(q_hm, kmean.reshape(nb, N_HEADS, HEAD_DIM).transpose(1, 0, 2))
    y_b = _mobaattn(qt, k_aug, vt_aug, bias)

    x1, h2, eid, rank, wt, cnt = _mix(y_a, y_b, sg, x2, w_branch_a.astype(BF16), w_branch_b.astype(BF16),
                                      w_out.astype(BF16), gate1, row(norm2_g), scale2, shift2,
                                      router_w, row(router_b))

    counts = cnt[0].astype(jnp.int32)
    tiles_per = (counts + EXPERT_TILE - 1) // EXPERT_TILE
    tile_end = jnp.cumsum(tiles_per)
    off = (tile_end - tiles_per) * EXPERT_TILE
    n_tiles = (t * EXPERT_TOP_K + N_EXPERTS * (EXPERT_TILE - 1)) // EXPERT_TILE
    n_used = tile_end[-1:]
    tile_expert = jnp.minimum(jnp.sum(tile_end[None, :] <= jnp.arange(n_tiles)[:, None], axis=1),
                              N_EXPERTS - 1).astype(jnp.int32)
    last_used = tile_expert[jnp.maximum(n_used[0] - 1, 0)]
    tile_expert = jnp.where(jnp.arange(n_tiles) < n_used[0], tile_expert, last_used)

    slot_flat = _slots(off, eid, rank)
    xs = _dispatch(slot_flat, h2, n_tiles * EXPERT_TILE)
    f = exp_w_gate.shape[2]
    ys = _experts(tile_expert, n_used, xs,
                  exp_w_gate, exp_b_gate.reshape(N_EXPERTS, 1, f),
                  exp_w_up, exp_b_up.reshape(N_EXPERTS, 1, f),
                  exp_w_down, exp_b_down.reshape(N_EXPERTS, 1, d))
    out = _combine(slot_flat, ys, wt, x1, gate2, row(normf_g))
    return out.reshape(bsz, t, d)
```

```python
import jax
import jax.numpy as jnp
from jax import lax
from jax.experimental import pallas as pl
from jax.experimental.pallas import tpu as pltpu

F32 = jnp.float32
BF16 = jnp.bfloat16
HI = lax.Precision.HIGHEST

HEAD_DIM = 64
N_HEADS = 8
WIDTH = N_HEADS * HEAD_DIM
DECAY_LORA = 64
AAA_LORA = 64
GATE_LORA = 128
RWKV_PROJ = 3 * WIDTH + DECAY_LORA + AAA_LORA + GATE_LORA
GN_EPS = 64e-5
MOBA_BLOCK = 256
MOBA_TOP_K = 3
KV_GROUP = 4
BIAS_ROWS = 16
K_AUG = 128
V_AUG = HEAD_DIM + 16
HEADS_PER_STEP = 2
LOG2E = 1.4426950408889634
N_EXPERTS = 32
EXPERT_TOP_K = 4
SWIGLU_LIMIT = 7.0
SWIGLU_ALPHA = 1.702
RMS_EPS = 1e-5
GATE_NEG = -1e30
MASK_NEG = -1e30

SCAN_CHUNK = 64
SCAN_CHUNKS = 2
EXPERT_TILE = 256
VMEM_LIMIT = 56 * 1024 * 1024
TOKEN_ROWS = 8
PACKED_ROWS = 4
LANES = 128


def _cparams(*sem):
    return pltpu.CompilerParams(dimension_semantics=sem, vmem_limit_bytes=VMEM_LIMIT)


def _sigmoid(x):
    return 1.0 / (1.0 + jnp.exp(-x))


def _mod_kernel(c_ref, w_ref, b_ref, o_ref):
    c = c_ref[...]
    s = c * _sigmoid(c)
    o_ref[...] = jnp.dot(s, w_ref[...], precision=HI, preferred_element_type=F32) + b_ref[...]


def _mod(c, w_ada, b_ada):
    d = c.shape[-1]
    n = w_ada.shape[1]
    c8 = jnp.broadcast_to(c[:1], (8, d))
    tn = 1024
    out = pl.pallas_call(
        _mod_kernel,
        grid=(n // tn,),
        in_specs=[pl.BlockSpec((8, d), lambda j: (0, 0)),
                  pl.BlockSpec((d, tn), lambda j: (0, j)),
                  pl.BlockSpec((1, tn), lambda j: (0, j))],
        out_specs=pl.BlockSpec((8, tn), lambda j: (0, j)),
        out_shape=jax.ShapeDtypeStruct((8, n), F32),
        compiler_params=_cparams("arbitrary"),
        name="mod",
    )(c8, w_ada, b_ada.reshape(1, n))
    return out[:1]


def _inproj_kernel(x_ref, g_ref, sc_ref, sh_ref, wr_ref, wq_ref, wg_ref,
                   pr_ref, sg_ref, km_ref, qh_ref, qt_ref, ka_ref, va_ref):
    i = pl.program_id(0)
    x = x_ref[...]
    tm = x.shape[0]
    ms = jnp.mean(x * x, axis=-1, keepdims=True)
    h = x * lax.rsqrt(ms + RMS_EPS) * g_ref[...]
    h = h * (1.0 + sc_ref[...]) + sh_ref[...]
    hb = h.astype(BF16)
    pr_ref[...] = jnp.dot(hb, wr_ref[...], preferred_element_type=F32)
    sg_ref[...] = _sigmoid(jnp.dot(hb, wg_ref[...], preferred_element_type=F32))
    qkv = jnp.dot(hb, wq_ref[...], preferred_element_type=F32)
    km_ref[0] = jnp.mean(qkv[:, WIDTH:2 * WIDTH], axis=0, keepdims=True)
    col = lax.broadcasted_iota(jnp.int32, (tm, K_AUG - HEAD_DIM), 1)
    indicator = jnp.where(col == i % KV_GROUP, 1.0, 0.0).astype(BF16)
    row = lax.broadcasted_iota(jnp.int32, (V_AUG - HEAD_DIM, tm), 0)
    ones_row = jnp.where(row == 0, 1.0, 0.0).astype(BF16)
    for hd in range(N_HEADS):
        lo = hd * HEAD_DIM
        q = qkv[:, lo:lo + HEAD_DIM]
        kh = qkv[:, WIDTH + lo:WIDTH + lo + HEAD_DIM]
        vh = qkv[:, 2 * WIDTH + lo:2 * WIDTH + lo + HEAD_DIM]
        qh_ref[hd] = q
        qt_ref[hd] = (q * (HEAD_DIM ** -0.5 * LOG2E)).T.astype(BF16)
        ka_ref[hd, 0] = jnp.concatenate([kh.astype(BF16), indicator], axis=1)
        va_ref[hd, 0] = jnp.concatenate([vh.T.astype(BF16), ones_row], axis=0)


def _inproj(x2, norm_g, scale, shift, w_r, w_q, w_g):
    t, d = x2.shape
    tm = MOBA_BLOCK
    gs = MOBA_BLOCK * KV_GROUP
    ng = t // gs
    full = lambda a: pl.BlockSpec(a.shape, lambda i: (0, 0))
    row = lambda n: pl.BlockSpec((tm, n), lambda i: (i, 0))
    return pl.pallas_call(
        _inproj_kernel,
        grid=(t // tm,),
        in_specs=[row(d), full(norm_g), full(scale), full(shift), full(w_r), full(w_q), full(w_g)],
        out_specs=[row(w_r.shape[1]), row(w_g.shape[1]),
                   pl.BlockSpec((1, 1, WIDTH), lambda i: (i, 0, 0)),
                   pl.BlockSpec((N_HEADS, tm, HEAD_DIM), lambda i: (0, i, 0)),
                   pl.BlockSpec((N_HEADS, HEAD_DIM, tm), lambda i: (0, 0, i)),
                   pl.BlockSpec((N_HEADS, 1, tm, K_AUG), lambda i: (0, i // KV_GROUP, i % KV_GROUP, 0)),
                   pl.BlockSpec((N_HEADS, 1, V_AUG, tm), lambda i: (0, i // KV_GROUP, 0, i % KV_GROUP))],
        out_shape=[jax.ShapeDtypeStruct((t, w_r.shape[1]), F32),
                   jax.ShapeDtypeStruct((t, w_g.shape[1]), F32),
                   jax.ShapeDtypeStruct((t // tm, 1, WIDTH), F32),
                   jax.ShapeDtypeStruct((N_HEADS, t, HEAD_DIM), F32),
                   jax.ShapeDtypeStruct((N_HEADS, HEAD_DIM, t), BF16),
                   jax.ShapeDtypeStruct((N_HEADS, ng, gs, K_AUG), BF16),
                   jax.ShapeDtypeStruct((N_HEADS, ng, V_AUG, gs), BF16)],
        compiler_params=_cparams("arbitrary"),
        name="inproj",
    )(x2, norm_g, scale, shift, w_r, w_q, w_g)


def _rwkvprep_kernel(p_ref, mu_ref, w0_ref, w2_ref, a0_ref, a2_ref, g2_ref, kk_ref, ka_ref, bd_ref,
                     r_out, lw_out, k_out, v_out, a_out, b_out, g_out, prev_ref):
    i = pl.program_id(0)

    @pl.when(i == 0)
    def _():
        prev_ref[...] = jnp.zeros_like(prev_ref)

    p = p_ref[...]
    tm = p.shape[0]
    rolled = pltpu.roll(p, 1, 0)
    row0 = lax.broadcasted_iota(jnp.int32, p.shape, 0) == 0
    p_prev = jnp.where(row0, prev_ref[0:1, :], rolled)
    prev_ref[0:1, :] = p[tm - 1:tm, :]
    ps = p + (p_prev - p) * mu_ref[...]
    r = ps[:, 0:WIDTH]
    k = ps[:, WIDTH:2 * WIDTH]
    v = ps[:, 2 * WIDTH:3 * WIDTH]
    xwa = ps[:, 3 * WIDTH:3 * WIDTH + DECAY_LORA + AAA_LORA]
    xg = ps[:, 3 * WIDTH + DECAY_LORA + AAA_LORA:RWKV_PROJ]
    dot = lambda lhs, w_ref: jnp.dot(lhs.astype(BF16), w_ref[...], preferred_element_type=F32)
    z = w0_ref[...] + dot(jnp.tanh(xwa), w2_ref)
    nz = -z
    softplus = jnp.maximum(nz, 0.0) + jnp.log(1.0 + jnp.exp(-jnp.abs(nz)))
    w = -softplus - 0.5
    a = _sigmoid(a0_ref[...] + dot(xwa, a2_ref))
    g = dot(_sigmoid(xg), g2_ref)
    kk = k * kk_ref[...]
    sq = kk * kk
    sq_hi = sq.astype(BF16)
    ss = dot(sq_hi, bd_ref) + dot(sq - sq_hi.astype(F32), bd_ref)
    kk = kk / jnp.maximum(jnp.sqrt(ss), 1e-12)
    outs = ((r_out, r), (lw_out, -jnp.exp(w)),
            (k_out, k * (1.0 + (a - 1.0) * ka_ref[...])), (v_out, v), (a_out, -kk), (b_out, kk * a), (g_out, g))
    for ref, val in outs:
        for hd in range(N_HEADS):
            ref[hd] = val[:, hd * HEAD_DIM:(hd + 1) * HEAD_DIM]


def _rwkvprep(p_rwkv, mu, w0, w2p, a0, a2p, g2, k_k, k_a, bd):
    t = p_rwkv.shape[0]
    tm = 256
    full = lambda a: pl.BlockSpec(a.shape, lambda i: (0, 0))
    row = lambda n: pl.BlockSpec((tm, n), lambda i: (i, 0))
    outs = [jax.ShapeDtypeStruct((N_HEADS, t, HEAD_DIM), F32)] * 7
    return pl.pallas_call(
        _rwkvprep_kernel,
        grid=(t // tm,),
        in_specs=[row(RWKV_PROJ), full(mu), full(w0), full(w2p), full(a0), full(a2p), full(g2),
                  full(k_k), full(k_a), full(bd)],
        out_specs=[pl.BlockSpec((N_HEADS, tm, HEAD_DIM), lambda i: (0, i, 0))] * 7,
        out_shape=outs,
        scratch_shapes=[pltpu.VMEM((8, RWKV_PROJ), F32)],
        compiler_params=_cparams("arbitrary"),
        name="rwkvprep",
    )(p_rwkv, mu, w0, w2p, a0, a2p, g2, k_k, k_a, bd)


def _bmm(a, b):
    return jnp.einsum("hmk,hkn->hmn", a.astype(BF16), b.astype(BF16), preferred_element_type=F32)


def _bmm_nt(a, b):
    return jnp.einsum("hmk,hnk->hmn", a.astype(BF16), b.astype(BF16), preferred_element_type=F32)


def _bmm_tn(a, b):
    return jnp.einsum("hkm,hkn->hmn", a.astype(BF16), b.astype(BF16), preferred_element_type=F32)


def _rwkvscan_kernel(r_ref, lw_ref, k_ref, v_ref, a_ref, b_ref, g_ref, lnw_ref, lnb_ref, rk_ref,
                     y_ref, s_ref):
    i = pl.program_id(0)

    @pl.when(i == 0)
    def _():
        s_ref[...] = jnp.zeros_like(s_ref)

    c = SCAN_CHUNK
    nc = SCAN_CHUNKS
    nh, _, n = r_ref.shape
    nb = nh * nc
    split = lambda ref: ref[...].reshape(nb, c, n)
    r, lw, k, v, a, b = (split(ref) for ref in (r_ref, lw_ref, k_ref, v_ref, a_ref, b_ref))

    row = lax.broadcasted_iota(jnp.int32, (nb, c, c), 1)
    col = lax.broadcasted_iota(jnp.int32, (nb, c, c), 2)
    lower = row >= col
    strict = row > col
    cum = jnp.einsum("hmk,hkn->hmn", lower.astype(F32), lw, precision=HI,
                     preferred_element_type=F32)
    tot = cum[:, c - 1:c, :]
    g_in = jnp.exp(cum)
    g_ex = jnp.exp(cum - lw)
    g_inv = jnp.exp(-cum)
    g_rem = jnp.exp(tot - cum)
    a_t = a * g_ex
    r_t = r * g_in
    b_t = b * g_inv
    k_t = k * g_inv
    l_ab = jnp.where(strict, _bmm_nt(a_t, b_t), 0.0)
    l_ak = jnp.where(strict, _bmm_nt(a_t, k_t), 0.0)
    m_rb = jnp.where(lower, _bmm_nt(r_t, b_t), 0.0)
    m_rk = jnp.where(lower, _bmm_nt(r_t, k_t), 0.0)
    same16 = (row // 16) == (col // 16)
    same32 = (row // 32) == (col // 32)
    diag16 = jnp.where(same16, l_ab, 0.0)
    inv = jnp.where(row == col, 1.0, 0.0) + diag16
    lp = diag16
    for _ in range(3):
        lp = _bmm(lp, lp)
        inv = inv + _bmm(inv, lp)
    off32 = jnp.where(jnp.logical_and(same32, jnp.logical_not(same16)), l_ab, 0.0)
    inv = inv + _bmm(_bmm(inv, off32), inv)
    off64 = jnp.where(same32, 0.0, l_ab)
    inv = inv + _bmm(_bmm(inv, off64), inv)
    a_hat = _bmm(inv, a_t)
    w_hat = _bmm(inv, _bmm(l_ak, v))
    r_hat = r_t + _bmm(m_rb, a_hat)
    y0 = _bmm(m_rb, w_hat) + _bmm(m_rk, v)
    b_hat = b * g_rem
    k_hat = k * g_rem
    decay = jnp.exp(tot)

    chunk = lambda x, j: x.reshape(nh, nc, x.shape[1], n)[:, j]
    s = s_ref[...]
    ys = []
    for j in range(nc):
        u = _bmm_nt(chunk(a_hat, j), s) + chunk(w_hat, j)
        ys.append(_bmm_nt(chunk(r_hat, j), s) + chunk(y0, j))
        s = s * chunk(decay, j) + _bmm_tn(u, chunk(b_hat, j)) + _bmm_tn(chunk(v, j), chunk(k_hat, j))
    s_ref[...] = s
    y = jnp.concatenate(ys, axis=1)

    r = r_ref[...]
    mean = jnp.mean(y, axis=-1, keepdims=True)
    yc = y - mean
    var = jnp.mean(yc * yc, axis=-1, keepdims=True)
    yn = yc * lax.rsqrt(var + GN_EPS) * lnw_ref[...] + lnb_ref[...]
    bonus = jnp.sum(r * k_ref[...] * rk_ref[...], axis=-1, keepdims=True) * v_ref[...]
    out = (yn + bonus) * g_ref[...]
    for hd in range(nh):
        y_ref[:, hd * n:(hd + 1) * n] = out[hd]


def _rwkvscan(r, lw, k, v, a, b, g, ln_w, ln_b, r_k):
    nh, t, n = r.shape
    rows = SCAN_CHUNK * SCAN_CHUNKS
    blk = pl.BlockSpec((nh, rows, n), lambda i: (0, i, 0))
    par = pl.BlockSpec((nh, 1, n), lambda i: (0, 0, 0))
    return pl.pallas_call(
        _rwkvscan_kernel,
        grid=(t // rows,),
        in_specs=[blk] * 7 + [par] * 3,
        out_specs=pl.BlockSpec((rows, nh * n), lambda i: (i, 0)),
        out_shape=jax.ShapeDtypeStruct((t, nh * n), F32),
        scratch_shapes=[pltpu.VMEM((nh, n, n), F32)],
        compiler_params=_cparams("arbitrary"),
        name="rwkvscan",
    )(r, lw, k, v, a, b, g, ln_w, ln_b, r_k)


def _mobasel_kernel(q_ref, km_ref, o_ref):
    j = pl.program_id(1)
    q = q_ref[0]
    km = km_ref[0]
    nb = km.shape[0]
    tq = q.shape[0]
    gate = lax.dot_general(km, q, (((1,), (1,)), ((), ())), precision=HI, preferred_element_type=F32)
    n_id = lax.broadcasted_iota(jnp.int32, (nb, tq), 0)
    t_id = lax.broadcasted_iota(jnp.int32, (nb, tq), 1) + j * tq
    q_blk = t_id // MOBA_BLOCK
    cand = n_id < nb - 1
    past = n_id < q_blk
    neg_inf = jnp.float32(-jnp.inf)
    gate = jnp.where(cand, jnp.where(past, gate, GATE_NEG), neg_inf)
    sel = jnp.zeros((nb, tq), F32)
    for _ in range(min(MOBA_TOP_K, nb - 1)):
        m = jnp.max(gate, axis=0, keepdims=True)
        idx = jnp.min(jnp.where(gate == m, n_id, nb), axis=0, keepdims=True)
        pick = n_id == idx
        sel = jnp.where(pick, 1.0, sel)
        gate = jnp.where(pick, neg_inf, gate)
    bias = jnp.where(jnp.logical_or(jnp.logical_and(past, sel > 0.5), n_id == q_blk), 0.0, MASK_NEG)
    pad = jnp.zeros((BIAS_ROWS - KV_GROUP, tq), F32)
    for g in range(nb // KV_GROUP):
        o_ref[0, g] = jnp.concatenate([bias[g * KV_GROUP:(g + 1) * KV_GROUP], pad], axis=0).astype(BF16)


def _mobasel(q_hm, kmean):
    nh, t, n = q_hm.shape
    nb = kmean.shape[1]
    ng = nb // KV_GROUP
    tq = min(t, 2048)
    return pl.pallas_call(
        _mobasel_kernel,
        grid=(nh, t // tq),
        in_specs=[pl.BlockSpec((1, tq, n), lambda h, j: (h, j, 0)),
                  pl.BlockSpec((1, nb, n), lambda h, j: (h, 0, 0))],
        out_specs=pl.BlockSpec((1, ng, BIAS_ROWS, tq), lambda h, j: (h, 0, 0, j)),
        out_shape=jax.ShapeDtypeStruct((nh, ng, BIAS_ROWS, t), BF16),
        compiler_params=_cparams("arbitrary", "arbitrary"),
        name="mobasel",
    )(q_hm, kmean)


def _mobaattn_kernel(qt_ref, k_ref, vt_ref, bias_ref, o_ref, sa_ref, sb_ref, m_ref, acc_ref):
    i = pl.program_id(1)
    bs = MOBA_BLOCK
    gd = i // KV_GROUP
    q_pad = jnp.zeros((K_AUG - HEAD_DIM - BIAS_ROWS, bs), BF16)
    heads = range(HEADS_PER_STEP)

    def scores_to(dst, g):
        for hh in heads:
            q_aug = jnp.concatenate([qt_ref[hh], bias_ref[hh, g], q_pad], axis=0)
            dst[hh] = jnp.dot(k_ref[hh, g], q_aug, preferred_element_type=F32)

    def update_from(src, g, keep=None):
        for hh in heads:
            s = src[hh] if keep is None else jnp.where(keep, src[hh], MASK_NEG)
            m = m_ref[hh]
            m_new = jnp.maximum(m, jnp.max(s, axis=0, keepdims=True))
            acc = jnp.exp2(m - m_new) * acc_ref[hh]
            for b in range(KV_GROUP):
                p = jnp.exp2((s[b * bs:(b + 1) * bs] - m_new).astype(BF16))
                acc = acc + jnp.dot(vt_ref[hh, g, :, pl.ds(b * bs, bs)], p, preferred_element_type=F32)
            acc_ref[hh] = acc
            m_ref[hh] = m_new

    def finish(src):
        causal = lax.broadcasted_iota(jnp.int32, (bs, bs), 0) <= lax.broadcasted_iota(jnp.int32, (bs, bs), 1)
        own = jnp.concatenate([jnp.logical_or(causal, i % KV_GROUP != b) for b in range(KV_GROUP)], axis=0)
        update_from(src, gd, own)
        for hh in heads:
            acc = acc_ref[hh]
            o_ref[hh] = acc[:HEAD_DIM] / acc[HEAD_DIM:HEAD_DIM + 1]

    m_ref[...] = jnp.full(m_ref.shape, MASK_NEG, F32)
    acc_ref[...] = jnp.zeros(acc_ref.shape, F32)
    scores_to(sa_ref, 0)

    def pair(k, _):
        g = 2 * k
        scores_to(sb_ref, g + 1)
        update_from(sa_ref, g)
        scores_to(sa_ref, g + 2)
        update_from(sb_ref, g + 1)
        return 0

    lax.fori_loop(0, gd // 2, pair, 0)

    @pl.when(gd % 2 == 1)
    def _():
        scores_to(sb_ref, gd)
        update_from(sa_ref, gd - 1)
        finish(sb_ref)

    @pl.when(gd % 2 == 0)
    def _():
        finish(sa_ref)


def _mobaattn(qt, k_aug, vt_aug, bias):
    nh, n, t = qt.shape
    gs = MOBA_BLOCK * KV_GROUP
    ng = t // gs
    hp = HEADS_PER_STEP
    return pl.pallas_call(
        _mobaattn_kernel,
        grid=(nh // hp, t // MOBA_BLOCK),
        in_specs=[pl.BlockSpec((hp, n, MOBA_BLOCK), lambda h, i: (h, 0, i)),
                  pl.BlockSpec((hp, ng, gs, K_AUG), lambda h, i: (h, 0, 0, 0)),
                  pl.BlockSpec((hp, ng, V_AUG, gs), lambda h, i: (h, 0, 0, 0)),
                  pl.BlockSpec((hp, ng, BIAS_ROWS, MOBA_BLOCK), lambda h, i: (h, 0, 0, i))],
        out_specs=pl.BlockSpec((hp, n, MOBA_BLOCK), lambda h, i: (h, 0, i)),
        out_shape=jax.ShapeDtypeStruct((nh, n, t), F32),
        scratch_shapes=[pltpu.VMEM((hp, gs, MOBA_BLOCK), F32), pltpu.VMEM((hp, gs, MOBA_BLOCK), F32),
                        pltpu.VMEM((hp, 1, MOBA_BLOCK), F32), pltpu.VMEM((hp, V_AUG, MOBA_BLOCK), F32)],
        compiler_params=_cparams("arbitrary", "arbitrary"),
        name="mobaattn",
    )(qt, k_aug, vt_aug, bias)


def _load_token_tiles(ref, n_tokens, lead=(), first=0):
    chunks = [ref[lead + (pl.ds(first * TOKEN_ROWS + ch, n_tokens, stride=TOKEN_ROWS), slice(None))]
              for ch in range(TOKEN_ROWS)]
    return jnp.concatenate(chunks, axis=1)


def _store_token_tiles(ref, val, first=0):
    for ch in range(TOKEN_ROWS):
        ref[pl.ds(first * TOKEN_ROWS + ch, val.shape[0], stride=TOKEN_ROWS), :] = val[:, ch * LANES:(ch + 1) * LANES]


def _bf16_bits(x):
    return pltpu.bitcast(x.astype(BF16).astype(F32), jnp.uint32)


def _store_packed_tokens(ref, val):
    for ch in range(PACKED_ROWS):
        hi = _bf16_bits(val[:, ch * LANES:(ch + 1) * LANES]) & jnp.uint32(0xFFFF0000)
        lo = _bf16_bits(val[:, (ch + PACKED_ROWS) * LANES:(ch + PACKED_ROWS + 1) * LANES]) >> 16
        ref[pl.ds(ch, val.shape[0], stride=PACKED_ROWS), :] = hi | lo


def _load_packed_tokens(ref, n_tokens):
    words = [ref[pl.ds(ch, n_tokens, stride=PACKED_ROWS), :] for ch in range(PACKED_ROWS)]
    his = [pltpu.bitcast(w & jnp.uint32(0xFFFF0000), F32) for w in words]
    los = [pltpu.bitcast(w << 16, F32) for w in words]
    return jnp.concatenate(his + los, axis=1).astype(BF16)


def _mix_kernel(ya_ref, yb_ref, sg_ref, x_ref, wa_ref, wb_ref, wo_ref, g1_ref, n2_ref, sc_ref, sh_ref,
                rw_ref, rb_ref, x1_ref, h2_ref, eid_ref, rank_ref, wt_ref, cnt_ref, base_ref):
    i = pl.program_id(0)

    @pl.when(i == 0)
    def _():
        base_ref[...] = jnp.zeros_like(base_ref)

    d = x_ref.shape[1]
    tm = x_ref.shape[0]
    pa = jnp.dot(ya_ref[...].astype(BF16), wa_ref[...], preferred_element_type=F32)
    yb = yb_ref[...].reshape(WIDTH, tm).T
    pb = jnp.dot(yb.astype(BF16), wb_ref[...], preferred_element_type=F32)
    sg = sg_ref[...]
    mixed = sg[:, :d] * pa + sg[:, d:] * pb
    mixed = jnp.dot(mixed.astype(BF16), wo_ref[...], preferred_element_type=F32)
    x1 = x_ref[...] + g1_ref[...] * mixed
    x1_ref[...] = x1
    ms = jnp.mean(x1 * x1, axis=-1, keepdims=True)
    h2 = x1 * lax.rsqrt(ms + RMS_EPS) * n2_ref[...]
    h2 = h2 * (1.0 + sc_ref[...]) + sh_ref[...]
    _store_packed_tokens(h2_ref, h2)

    logits = jnp.dot(h2, rw_ref[...], precision=HI, preferred_element_type=F32) + rb_ref[...]
    ne = logits.shape[1]
    e_id = lax.broadcasted_iota(jnp.int32, (tm, ne), 1)
    neg_inf = jnp.float32(-jnp.inf)
    work = logits
    picks, vals = [], []
    for _ in range(EXPERT_TOP_K):
        m = jnp.max(work, axis=-1, keepdims=True)
        idx = jnp.min(jnp.where(work == m, e_id, ne), axis=-1, keepdims=True)
        pick = e_id == idx
        picks.append((idx, pick))
        vals.append(m)
        work = jnp.where(pick, neg_inf, work)
    den = sum(jnp.exp(v - vals[0]) for v in vals)
    chosen = jnp.zeros((tm, ne), F32)
    for _, pick in picks:
        chosen = jnp.where(pick, 1.0, chosen)
    r_id = lax.broadcasted_iota(jnp.int32, (tm, tm), 0)
    c_id = lax.broadcasted_iota(jnp.int32, (tm, tm), 1)
    before = jnp.where(r_id > c_id, 1.0, 0.0).astype(BF16)
    ahead = jnp.dot(before, chosen.astype(BF16), preferred_element_type=F32) + base_ref[0:1, :]
    k_id = lax.broadcasted_iota(jnp.int32, (tm, EXPERT_TOP_K), 1)
    eid = jnp.zeros((tm, EXPERT_TOP_K), jnp.int32)
    rank = jnp.zeros((tm, EXPERT_TOP_K), jnp.int32)
    wt = jnp.zeros((tm, EXPERT_TOP_K), F32)
    for kk, ((idx, pick), v) in enumerate(zip(picks, vals)):
        rk = jnp.sum(jnp.where(pick, ahead, 0.0), axis=-1, keepdims=True).astype(jnp.int32)
        eid = jnp.where(k_id == kk, idx, eid)
        rank = jnp.where(k_id == kk, rk, rank)
        wt = jnp.where(k_id == kk, jnp.exp(v - vals[0]) / den, wt)
    eid_ref[...] = eid
    rank_ref[...] = rank
    wt_ref[...] = wt
    total = base_ref[0:1, :] + jnp.sum(chosen, axis=0, keepdims=True)
    base_ref[0:1, :] = total
    cnt_ref[...] = jnp.broadcast_to(total, cnt_ref.shape)


def _mix(ya, yb, sg, x2, w_a, w_b, w_o, gate1, norm2_g, scale2, shift2, router_w, router_b):
    t, d = x2.shape
    tm = 512
    ne = router_w.shape[1]
    full = lambda a: pl.BlockSpec(a.shape, lambda i: (0, 0))
    row = lambda n: pl.BlockSpec((tm, n), lambda i: (i, 0))
    return pl.pallas_call(
        _mix_kernel,
        grid=(t // tm,),
        in_specs=[row(WIDTH), pl.BlockSpec((N_HEADS, HEAD_DIM, tm), lambda i: (0, 0, i)), row(2 * d), row(d),
                  full(w_a), full(w_b), full(w_o),
                  full(gate1), full(norm2_g), full(scale2), full(shift2), full(router_w), full(router_b)],
        out_specs=[row(d), pl.BlockSpec((tm * PACKED_ROWS, LANES), lambda i: (i, 0)),
                   row(EXPERT_TOP_K), row(EXPERT_TOP_K), row(EXPERT_TOP_K),
                   pl.BlockSpec((8, ne), lambda i: (0, 0))],
        out_shape=[jax.ShapeDtypeStruct((t, d), F32), jax.ShapeDtypeStruct((t * PACKED_ROWS, LANES), jnp.uint32),
                   jax.ShapeDtypeStruct((t, EXPERT_TOP_K), jnp.int32),
                   jax.ShapeDtypeStruct((t, EXPERT_TOP_K), jnp.int32),
                   jax.ShapeDtypeStruct((t, EXPERT_TOP_K), F32),
                   jax.ShapeDtypeStruct((8, ne), F32)],
        scratch_shapes=[pltpu.VMEM((8, ne), F32)],
        compiler_params=_cparams("arbitrary"),
        name="mix",
    )(ya, yb, sg, x2, w_a, w_b, w_o, gate1, norm2_g, scale2, shift2, router_w, router_b)


DISPATCH_TILE = 256


def _slots_kernel(off_ref, eid_ref, rank_ref, o_ref):
    eid = eid_ref[...]
    slot = rank_ref[...]
    for e in range(N_EXPERTS):
        slot = slot + jnp.where(eid == e, off_ref[e], 0)
    o_ref[...] = slot * PACKED_ROWS


def _slots(off, eid, rank):
    n = eid.size
    lanes = 128
    shape = (n // lanes, lanes)
    full = pl.BlockSpec(shape, lambda i: (0, 0))
    out = pl.pallas_call(
        _slots_kernel,
        grid=(1,),
        in_specs=[pl.BlockSpec(memory_space=pltpu.SMEM), full, full],
        out_specs=full,
        out_shape=jax.ShapeDtypeStruct(shape, jnp.int32),
        compiler_params=_cparams("arbitrary"),
        name="slots",
    )(off, eid.reshape(shape), rank.reshape(shape))
    return out.reshape(n)


def _dispatch_kernel(slot_ref, h_ref, xs_in_ref, xs_ref, sem):
    del xs_in_ref
    tm = DISPATCH_TILE

    def body(t, _):
        for kk in range(EXPERT_TOP_K):
            row = pl.multiple_of(slot_ref[t * EXPERT_TOP_K + kk], PACKED_ROWS)
            pltpu.make_async_copy(h_ref.at[pl.ds(t * PACKED_ROWS, PACKED_ROWS)], xs_ref.at[pl.ds(row, PACKED_ROWS)],
                                  sem).start(priority=kk % 2)
        return 0

    lax.fori_loop(0, tm, body, 0, unroll=2)
    for _ in range(EXPERT_TOP_K):
        pltpu.make_async_copy(h_ref, xs_ref.at[pl.ds(0, tm * PACKED_ROWS)], sem).wait()


def _dispatch(slot_flat, h2, n_slots):
    rows, lanes = h2.shape
    tm = DISPATCH_TILE
    t = rows // PACKED_ROWS
    return pl.pallas_call(
        _dispatch_kernel,
        grid=(t // tm,),
        in_specs=[pl.BlockSpec((tm * EXPERT_TOP_K,), lambda i: (i,), memory_space=pltpu.SMEM),
                  pl.BlockSpec((tm * PACKED_ROWS, lanes), lambda i: (i, 0)),
                  pl.BlockSpec(memory_space=pl.ANY)],
        out_specs=pl.BlockSpec(memory_space=pl.ANY),
        out_shape=jax.ShapeDtypeStruct((n_slots * PACKED_ROWS, lanes), jnp.uint32),
        scratch_shapes=[pltpu.SemaphoreType.DMA(())],
        input_output_aliases={2: 0},
        compiler_params=pltpu.CompilerParams(dimension_semantics=("arbitrary",), has_side_effects=True),
        name="dispatch",
    )(slot_flat, h2, jnp.zeros((n_slots * PACKED_ROWS, lanes), jnp.uint32))


def _experts_kernel(te_ref, nu_ref, x_ref, wg_ref, bg_ref, wu_ref, bu_ref, wd_ref, bd_ref, y_ref,
                    wg_s, wu_s, wd_s):
    i = pl.program_id(0)
    used = i < nu_ref[0]
    new_expert = jnp.logical_or(i == 0, te_ref[i] != te_ref[jnp.maximum(i - 1, 0)])

    @pl.when(jnp.logical_and(used, new_expert))
    def _():
        wg_s[...] = wg_ref[0].astype(BF16)
        wu_s[...] = wu_ref[0].astype(BF16)
        wd_s[...] = wd_ref[0].astype(BF16)

    @pl.when(used)
    def _():
        x = _load_packed_tokens(x_ref, EXPERT_TILE)
        gt = jnp.minimum(jnp.dot(x, wg_s[...], preferred_element_type=F32) + bg_ref[0], SWIGLU_LIMIT)
        up = jnp.clip(jnp.dot(x, wu_s[...], preferred_element_type=F32) + bu_ref[0], -SWIGLU_LIMIT, SWIGLU_LIMIT)
        hid = (up + 1.0) * gt * _sigmoid(SWIGLU_ALPHA * gt)
        _store_token_tiles(y_ref, jnp.dot(hid.astype(BF16), wd_s[...], preferred_element_type=F32) + bd_ref[0])

    @pl.when(jnp.logical_not(used))
    def _():
        y_ref[...] = jnp.zeros_like(y_ref)


def _experts(tile_expert, n_used, xs, w_gate, b_gate, w_up, b_up, w_down, b_down):
    rows, lanes = xs.shape
    d = TOKEN_ROWS * lanes
    f = w_gate.shape[2]
    tm = EXPERT_TILE
    n_tiles = rows // (tm * PACKED_ROWS)
    row = lambda per_token: pl.BlockSpec((tm * per_token, lanes), lambda i, te, nu: (i, 0))
    wspec = lambda a, b: pl.BlockSpec((1, a, b), lambda i, te, nu: (te[i], 0, 0))
    grid_spec = pltpu.PrefetchScalarGridSpec(
        num_scalar_prefetch=2,
        grid=(n_tiles,),
        in_specs=[row(PACKED_ROWS), wspec(d, f), wspec(1, f), wspec(d, f), wspec(1, f), wspec(f, d), wspec(1, d)],
        out_specs=row(TOKEN_ROWS),
        scratch_shapes=[pltpu.VMEM((d, f), BF16), pltpu.VMEM((d, f), BF16), pltpu.VMEM((f, d), BF16)],
    )
    return pl.pallas_call(
        _experts_kernel,
        grid_spec=grid_spec,
        out_shape=jax.ShapeDtypeStruct((n_tiles * tm * TOKEN_ROWS, lanes), F32),
        compiler_params=_cparams("arbitrary"),
        name="experts",
    )(tile_expert, n_used, xs, w_gate, b_gate, w_up, b_up, w_down, b_down)


def _combine_kernel(slot_ref, ys_ref, wt_ref, x1_ref, g2_ref, nf_ref, o_ref, buf, sem):
    tm = DISPATCH_TILE

    def body(t, _):
        for kk in range(EXPERT_TOP_K):
            row = pl.multiple_of(slot_ref[t * EXPERT_TOP_K + kk] * (TOKEN_ROWS // PACKED_ROWS), TOKEN_ROWS)
            pltpu.make_async_copy(ys_ref.at[pl.ds(row, TOKEN_ROWS)], buf.at[kk, pl.ds(t * TOKEN_ROWS, TOKEN_ROWS)],
                                  sem).start(priority=kk % 2)
        return 0

    lax.fori_loop(0, tm, body, 0, unroll=2)
    for kk in range(EXPERT_TOP_K):
        pltpu.make_async_copy(ys_ref.at[pl.ds(0, tm * TOKEN_ROWS)], buf.at[kk], sem).wait()
    wt = wt_ref[...]
    moe = wt[:, 0:1] * _load_token_tiles(buf, tm, (0,))
    for kk in range(1, EXPERT_TOP_K):
        moe = moe + wt[:, kk:kk + 1] * _load_token_tiles(buf, tm, (kk,))
    x = x1_ref[...] + g2_ref[...] * moe
    ms = jnp.mean(x * x, axis=-1, keepdims=True)
    o_ref[...] = x * lax.rsqrt(ms + RMS_EPS) * nf_ref[...]


def _combine(slot_flat, ys, wt, x1, gate2, normf_g):
    t, d = x1.shape
    tm = DISPATCH_TILE
    full = lambda a: pl.BlockSpec(a.shape, lambda i: (0, 0))
    row = lambda n: pl.BlockSpec((tm, n), lambda i: (i, 0))
    return pl.pallas_call(
        _combine_kernel,
        grid=(t // tm,),
        in_specs=[pl.BlockSpec((tm * EXPERT_TOP_K,), lambda i: (i,), memory_space=pltpu.SMEM),
                  pl.BlockSpec(memory_space=pl.ANY),
                  row(EXPERT_TOP_K), row(d), full(gate2), full(normf_g)],
        out_specs=row(d),
        out_shape=jax.ShapeDtypeStruct((t, d), F32),
        scratch_shapes=[pltpu.VMEM((EXPERT_TOP_K, tm * TOKEN_ROWS, LANES), F32), pltpu.SemaphoreType.DMA(())],
        compiler_params=_cparams("arbitrary"),
        name="combine",
    )(slot_flat, ys, wt, x1, gate2, normf_g)


def kernel(x, c, w_ada, b_ada, norm1_g, w_in, rwkv_mu, rwkv_w0, rwkv_w2, rwkv_a0, rwkv_a2, rwkv_g2, rwkv_k_k, rwkv_k_a, rwkv_r_k, rwkv_ln_w, rwkv_ln_b, w_branch_a, w_branch_b, w_out, norm2_g, router_w, router_b, exp_w_gate, exp_b_gate, exp_w_up, exp_b_up, exp_w_down, exp_b_down, normf_g):
    bsz, t, d = x.shape
    assert bsz == 1 and t % (MOBA_BLOCK * KV_GROUP) == 0
    x2 = x.reshape(t, d)
    row = lambda a: a.reshape(1, -1)

    mod = _mod(c, w_ada, b_ada)
    shift1, scale1, gate1, shift2, scale2, gate2 = [mod[:, j * d:(j + 1) * d] for j in range(6)]

    qkv_end = RWKV_PROJ + 3 * WIDTH
    w_in_b = w_in.astype(BF16)
    p_rwkv, sg, kmean, q_hm, qt, k_aug, vt_aug = _inproj(
        x2, row(norm1_g), scale1, shift1, w_in_b[:, :RWKV_PROJ], w_in_b[:, RWKV_PROJ:qkv_end], w_in_b[:, qkv_end:])

    zeros_lora = jnp.zeros((DECAY_LORA, WIDTH), F32)
    w2p = jnp.concatenate([rwkv_w2, zeros_lora], axis=0)
    a2p = jnp.concatenate([zeros_lora, rwkv_a2], axis=0)
    head_of = jnp.arange(WIDTH) // HEAD_DIM
    bd = (head_of[:, None] == head_of[None, :]).astype(BF16)
    r, lw, k, v, av, bv, g = _rwkvprep(p_rwkv, row(rwkv_mu), row(rwkv_w0), w2p.astype(BF16), row(rwkv_a0),
                                       a2p.astype(BF16), rwkv_g2.astype(BF16), row(rwkv_k_k), row(rwkv_k_a), bd)
    per_head = lambda a: a.reshape(N_HEADS, 1, HEAD_DIM)
    y_a = _rwkvscan(r, lw, k, v, av, bv, g, per_head(rwkv_ln_w), per_head(rwkv_ln_b), per_head(rwkv_r_k))

    nb = t // MOBA_BLOCK
    bias = _mobasel(q_hm, kmean.reshape(nb, N_HEADS, HEAD_DIM).transpose(1, 0, 2))
    y_b = _mobaattn(qt, k_aug, vt_aug, bias)

    x1, h2, eid, rank, wt, cnt = _mix(y_a, y_b, sg, x2, w_branch_a.astype(BF16), w_branch_b.astype(BF16),
                                      w_out.astype(BF16), gate1, row(norm2_g), scale2, shift2,
                                      router_w, row(router_b))

    counts = cnt[0].astype(jnp.int32)
    tiles_per = (counts + EXPERT_TILE - 1) // EXPERT_TILE
    tile_end = jnp.cumsum(tiles_per)
    off = (tile_end - tiles_per) * EXPERT_TILE
    n_tiles = (t * EXPERT_TOP_K + N_EXPERTS * (EXPERT_TILE - 1)) // EXPERT_TILE
    n_used = tile_end[-1:]
    tile_expert = jnp.minimum(jnp.sum(tile_end[None, :] <= jnp.arange(n_tiles)[:, None], axis=1),
                              N_EXPERTS - 1).astype(jnp.int32)
    last_used = tile_expert[jnp.maximum(n_used[0] - 1, 0)]
    tile_expert = jnp.where(jnp.arange(n_tiles) < n_used[0], tile_expert, last_used)

    slot_flat = _slots(off, eid, rank)
    xs = _dispatch(slot_flat, h2, n_tiles * EXPERT_TILE)
    f = exp_w_gate.shape[2]
    ys = _experts(tile_expert, n_used, xs,
                  exp_w_gate, exp_b_gate.reshape(N_EXPERTS, 1, f),
                  exp_w_up, exp_b_up.reshape(N_EXPERTS, 1, f),
                  exp_w_down, exp_b_down.reshape(N_EXPERTS, 1, d))
    out = _combine(slot_flat, ys, wt, x1, gate2, row(normf_g))
    return out.reshape(bsz, t, d)
```

```python
import jax
import jax.numpy as jnp
from jax import lax
from jax.experimental import pallas as pl
from jax.experimental.pallas import tpu as pltpu

F32 = jnp.float32
BF16 = jnp.bfloat16
HI = lax.Precision.HIGHEST

HEAD_DIM = 64
N_HEADS = 8
WIDTH = N_HEADS * HEAD_DIM
DECAY_LORA = 64
AAA_LORA = 64
GATE_LORA = 128
RWKV_PROJ = 3 * WIDTH + DECAY_LORA + AAA_LORA + GATE_LORA
GN_EPS = 64e-5
MOBA_BLOCK = 256
MOBA_TOP_K = 3
KV_GROUP = 4
BIAS_ROWS = 16
K_AUG = 128
V_AUG = HEAD_DIM + 16
HEADS_PER_STEP = 2
LOG2E = 1.4426950408889634
N_EXPERTS = 32
EXPERT_TOP_K = 4
SWIGLU_LIMIT = 7.0
SWIGLU_ALPHA = 1.702
RMS_EPS = 1e-5
GATE_NEG = -1e30
MASK_NEG = -1e30

SCAN_CHUNK = 64
SCAN_CHUNKS = 2
EXPERT_TILE = 256
VMEM_LIMIT = 56 * 1024 * 1024
TOKEN_ROWS = 8
LANES = 128


def _cparams(*sem):
    return pltpu.CompilerParams(dimension_semantics=sem, vmem_limit_bytes=VMEM_LIMIT)


def _sigmoid(x):
    return 1.0 / (1.0 + jnp.exp(-x))


def _mod_kernel(c_ref, w_ref, b_ref, o_ref):
    c = c_ref[...]
    s = c * _sigmoid(c)
    o_ref[...] = jnp.dot(s, w_ref[...], precision=HI, preferred_element_type=F32) + b_ref[...]


def _mod(c, w_ada, b_ada):
    d = c.shape[-1]
    n = w_ada.shape[1]
    c8 = jnp.broadcast_to(c[:1], (8, d))
    tn = 1024
    out = pl.pallas_call(
        _mod_kernel,
        grid=(n // tn,),
        in_specs=[pl.BlockSpec((8, d), lambda j: (0, 0)),
                  pl.BlockSpec((d, tn), lambda j: (0, j)),
                  pl.BlockSpec((1, tn), lambda j: (0, j))],
        out_specs=pl.BlockSpec((8, tn), lambda j: (0, j)),
        out_shape=jax.ShapeDtypeStruct((8, n), F32),
        compiler_params=_cparams("arbitrary"),
        name="mod",
    )(c8, w_ada, b_ada.reshape(1, n))
    return out[:1]


def _inproj_kernel(x_ref, g_ref, sc_ref, sh_ref, wr_ref, wq_ref, wg_ref,
                   pr_ref, sg_ref, km_ref, qh_ref, qt_ref, ka_ref, va_ref):
    i = pl.program_id(0)
    x = x_ref[...]
    tm = x.shape[0]
    ms = jnp.mean(x * x, axis=-1, keepdims=True)
    h = x * lax.rsqrt(ms + RMS_EPS) * g_ref[...]
    h = h * (1.0 + sc_ref[...]) + sh_ref[...]
    hb = h.astype(BF16)
    pr_ref[...] = jnp.dot(hb, wr_ref[...], preferred_element_type=F32)
    sg_ref[...] = _sigmoid(jnp.dot(hb, wg_ref[...], preferred_element_type=F32))
    qkv = jnp.dot(hb, wq_ref[...], preferred_element_type=F32)
    km_ref[0] = jnp.mean(qkv[:, WIDTH:2 * WIDTH], axis=0, keepdims=True)
    col = lax.broadcasted_iota(jnp.int32, (tm, K_AUG - HEAD_DIM), 1)
    indicator = jnp.where(col == i % KV_GROUP, 1.0, 0.0).astype(BF16)
    row = lax.broadcasted_iota(jnp.int32, (V_AUG - HEAD_DIM, tm), 0)
    ones_row = jnp.where(row == 0, 1.0, 0.0).astype(BF16)
    for hd in range(N_HEADS):
        lo = hd * HEAD_DIM
        q = qkv[:, lo:lo + HEAD_DIM]
        kh = qkv[:, WIDTH + lo:WIDTH + lo + HEAD_DIM]
        vh = qkv[:, 2 * WIDTH + lo:2 * WIDTH + lo + HEAD_DIM]
        qh_ref[hd] = q
        qt_ref[hd] = (q * (HEAD_DIM ** -0.5 * LOG2E)).T.astype(BF16)
        ka_ref[hd, 0] = jnp.concatenate([kh.astype(BF16), indicator], axis=1)
        va_ref[hd, 0] = jnp.concatenate([vh.T.astype(BF16), ones_row], axis=0)


def _inproj(x2, norm_g, scale, shift, w_r, w_q, w_g):
    t, d = x2.shape
    tm = MOBA_BLOCK
    gs = MOBA_BLOCK * KV_GROUP
    ng = t // gs
    full = lambda a: pl.BlockSpec(a.shape, lambda i: (0, 0))
    row = lambda n: pl.BlockSpec((tm, n), lambda i: (i, 0))
    return pl.pallas_call(
        _inproj_kernel,
        grid=(t // tm,),
        in_specs=[row(d), full(norm_g), full(scale), full(shift), full(w_r), full(w_q), full(w_g)],
        out_specs=[row(w_r.shape[1]), row(w_g.shape[1]),
                   pl.BlockSpec((1, 1, WIDTH), lambda i: (i, 0, 0)),
                   pl.BlockSpec((N_HEADS, tm, HEAD_DIM), lambda i: (0, i, 0)),
                   pl.BlockSpec((N_HEADS, HEAD_DIM, tm), lambda i: (0, 0, i)),
                   pl.BlockSpec((N_HEADS, 1, tm, K_AUG), lambda i: (0, i // KV_GROUP, i % KV_GROUP, 0)),
                   pl.BlockSpec((N_HEADS, 1, V_AUG, tm), lambda i: (0, i // KV_GROUP, 0, i % KV_GROUP))],
        out_shape=[jax.ShapeDtypeStruct((t, w_r.shape[1]), F32),
                   jax.ShapeDtypeStruct((t, w_g.shape[1]), F32),
                   jax.ShapeDtypeStruct((t // tm, 1, WIDTH), F32),
                   jax.ShapeDtypeStruct((N_HEADS, t, HEAD_DIM), F32),
                   jax.ShapeDtypeStruct((N_HEADS, HEAD_DIM, t), BF16),
                   jax.ShapeDtypeStruct((N_HEADS, ng, gs, K_AUG), BF16),
                   jax.ShapeDtypeStruct((N_HEADS, ng, V_AUG, gs), BF16)],
        compiler_params=_cparams("arbitrary"),
        name="inproj",
    )(x2, norm_g, scale, shift, w_r, w_q, w_g)


def _rwkvprep_kernel(p_ref, mu_ref, w0_ref, w2_ref, a0_ref, a2_ref, g2_ref, kk_ref, ka_ref, bd_ref,
                     r_out, lw_out, k_out, v_out, a_out, b_out, g_out, prev_ref):
    i = pl.program_id(0)

    @pl.when(i == 0)
    def _():
        prev_ref[...] = jnp.zeros_like(prev_ref)

    p = p_ref[...]
    tm = p.shape[0]
    rolled = pltpu.roll(p, 1, 0)
    row0 = lax.broadcasted_iota(jnp.int32, p.shape, 0) == 0
    p_prev = jnp.where(row0, prev_ref[0:1, :], rolled)
    prev_ref[0:1, :] = p[tm - 1:tm, :]
    ps = p + (p_prev - p) * mu_ref[...]
    r = ps[:, 0:WIDTH]
    k = ps[:, WIDTH:2 * WIDTH]
    v = ps[:, 2 * WIDTH:3 * WIDTH]
    xwa = ps[:, 3 * WIDTH:3 * WIDTH + DECAY_LORA + AAA_LORA]
    xg = ps[:, 3 * WIDTH + DECAY_LORA + AAA_LORA:RWKV_PROJ]
    dot = lambda lhs, w_ref: jnp.dot(lhs.astype(BF16), w_ref[...], preferred_element_type=F32)
    z = w0_ref[...] + dot(jnp.tanh(xwa), w2_ref)
    nz = -z
    softplus = jnp.maximum(nz, 0.0) + jnp.log(1.0 + jnp.exp(-jnp.abs(nz)))
    w = -softplus - 0.5
    a = _sigmoid(a0_ref[...] + dot(xwa, a2_ref))
    g = dot(_sigmoid(xg), g2_ref)
    kk = k * kk_ref[...]
    sq = kk * kk
    sq_hi = sq.astype(BF16)
    ss = dot(sq_hi, bd_ref) + dot(sq - sq_hi.astype(F32), bd_ref)
    kk = kk / jnp.maximum(jnp.sqrt(ss), 1e-12)
    outs = ((r_out, r), (lw_out, -jnp.exp(w)),
            (k_out, k * (1.0 + (a - 1.0) * ka_ref[...])), (v_out, v), (a_out, -kk), (b_out, kk * a), (g_out, g))
    for ref, val in outs:
        for hd in range(N_HEADS):
            ref[hd] = val[:, hd * HEAD_DIM:(hd + 1) * HEAD_DIM]


def _rwkvprep(p_rwkv, mu, w0, w2p, a0, a2p, g2, k_k, k_a, bd):
    t = p_rwkv.shape[0]
    tm = 256
    full = lambda a: pl.BlockSpec(a.shape, lambda i: (0, 0))
    row = lambda n: pl.BlockSpec((tm, n), lambda i: (i, 0))
    outs = [jax.ShapeDtypeStruct((N_HEADS, t, HEAD_DIM), F32)] * 7
    return pl.pallas_call(
        _rwkvprep_kernel,
        grid=(t // tm,),
        in_specs=[row(RWKV_PROJ), full(mu), full(w0), full(w2p), full(a0), full(a2p), full(g2),
                  full(k_k), full(k_a), full(bd)],
        out_specs=[pl.BlockSpec((N_HEADS, tm, HEAD_DIM), lambda i: (0, i, 0))] * 7,
        out_shape=outs,
        scratch_shapes=[pltpu.VMEM((8, RWKV_PROJ), F32)],
        compiler_params=_cparams("arbitrary"),
        name="rwkvprep",
    )(p_rwkv, mu, w0, w2p, a0, a2p, g2, k_k, k_a, bd)


def _bmm(a, b):
    return jnp.einsum("hmk,hkn->hmn", a.astype(BF16), b.astype(BF16), preferred_element_type=F32)


def _bmm_nt(a, b):
    return jnp.einsum("hmk,hnk->hmn", a.astype(BF16), b.astype(BF16), preferred_element_type=F32)


def _bmm_tn(a, b):
    return jnp.einsum("hkm,hkn->hmn", a.astype(BF16), b.astype(BF16), preferred_element_type=F32)


def _rwkvscan_kernel(r_ref, lw_ref, k_ref, v_ref, a_ref, b_ref, g_ref, lnw_ref, lnb_ref, rk_ref,
                     y_ref, s_ref):
    i = pl.program_id(0)

    @pl.when(i == 0)
    def _():
        s_ref[...] = jnp.zeros_like(s_ref)

    c = SCAN_CHUNK
    nc = SCAN_CHUNKS
    nh, _, n = r_ref.shape
    nb = nh * nc
    split = lambda ref: ref[...].reshape(nb, c, n)
    r, lw, k, v, a, b = (split(ref) for ref in (r_ref, lw_ref, k_ref, v_ref, a_ref, b_ref))

    row = lax.broadcasted_iota(jnp.int32, (nb, c, c), 1)
    col = lax.broadcasted_iota(jnp.int32, (nb, c, c), 2)
    lower = row >= col
    strict = row > col
    cum = jnp.einsum("hmk,hkn->hmn", lower.astype(F32), lw, precision=HI,
                     preferred_element_type=F32)
    tot = cum[:, c - 1:c, :]
    g_in = jnp.exp(cum)
    g_ex = jnp.exp(cum - lw)
    g_inv = jnp.exp(-cum)
    g_rem = jnp.exp(tot - cum)
    a_t = a * g_ex
    r_t = r * g_in
    b_t = b * g_inv
    k_t = k * g_inv
    l_ab = jnp.where(strict, _bmm_nt(a_t, b_t), 0.0)
    l_ak = jnp.where(strict, _bmm_nt(a_t, k_t), 0.0)
    m_rb = jnp.where(lower, _bmm_nt(r_t, b_t), 0.0)
    m_rk = jnp.where(lower, _bmm_nt(r_t, k_t), 0.0)
    same16 = (row // 16) == (col // 16)
    same32 = (row // 32) == (col // 32)
    diag16 = jnp.where(same16, l_ab, 0.0)
    inv = jnp.where(row == col, 1.0, 0.0) + diag16
    lp = diag16
    for _ in range(3):
        lp = _bmm(lp, lp)
        inv = inv + _bmm(inv, lp)
    off32 = jnp.where(jnp.logical_and(same32, jnp.logical_not(same16)), l_ab, 0.0)
    inv = inv + _bmm(_bmm(inv, off32), inv)
    off64 = jnp.where(same32, 0.0, l_ab)
    inv = inv + _bmm(_bmm(inv, off64), inv)
    a_hat = _bmm(inv, a_t)
    w_hat = _bmm(inv, _bmm(l_ak, v))
    r_hat = r_t + _bmm(m_rb, a_hat)
    y0 = _bmm(m_rb, w_hat) + _bmm(m_rk, v)
    b_hat = b * g_rem
    k_hat = k * g_rem
    decay = jnp.exp(tot)

    chunk = lambda x, j: x.reshape(nh, nc, x.shape[1], n)[:, j]
    s = s_ref[...]
    ys = []
    for j in range(nc):
        u = _bmm_nt(chunk(a_hat, j), s) + chunk(w_hat, j)
        ys.append(_bmm_nt(chunk(r_hat, j), s) + chunk(y0, j))
        s = s * chunk(decay, j) + _bmm_tn(u, chunk(b_hat, j)) + _bmm_tn(chunk(v, j), chunk(k_hat, j))
    s_ref[...] = s
    y = jnp.concatenate(ys, axis=1)

    r = r_ref[...]
    mean = jnp.mean(y, axis=-1, keepdims=True)
    yc = y - mean
    var = jnp.mean(yc * yc, axis=-1, keepdims=True)
    yn = yc * lax.rsqrt(var + GN_EPS) * lnw_ref[...] + lnb_ref[...]
    bonus = jnp.sum(r * k_ref[...] * rk_ref[...], axis=-1, keepdims=True) * v_ref[...]
    out = (yn + bonus) * g_ref[...]
    for hd in range(nh):
        y_ref[:, hd * n:(hd + 1) * n] = out[hd]


def _rwkvscan(r, lw, k, v, a, b, g, ln_w, ln_b, r_k):
    nh, t, n = r.shape
    rows = SCAN_CHUNK * SCAN_CHUNKS
    blk = pl.BlockSpec((nh, rows, n), lambda i: (0, i, 0))
    par = pl.BlockSpec((nh, 1, n), lambda i: (0, 0, 0))
    return pl.pallas_call(
        _rwkvscan_kernel,
        grid=(t // rows,),
        in_specs=[blk] * 7 + [par] * 3,
        out_specs=pl.BlockSpec((rows, nh * n), lambda i: (i, 0)),
        out_shape=jax.ShapeDtypeStruct((t, nh * n), F32),
        scratch_shapes=[pltpu.VMEM((nh, n, n), F32)],
        compiler_params=_cparams("arbitrary"),
        name="rwkvscan",
    )(r, lw, k, v, a, b, g, ln_w, ln_b, r_k)


def _mobasel_kernel(q_ref, km_ref, o_ref):
    j = pl.program_id(1)
    q = q_ref[0]
    km = km_ref[0]
    nb = km.shape[0]
    tq = q.shape[0]
    gate = lax.dot_general(km, q, (((1,), (1,)), ((), ())), precision=HI, preferred_element_type=F32)
    n_id = lax.broadcasted_iota(jnp.int32, (nb, tq), 0)
    t_id = lax.broadcasted_iota(jnp.int32, (nb, tq), 1) + j * tq
    q_blk = t_id // MOBA_BLOCK
    cand = n_id < nb - 1
    past = n_id < q_blk
    neg_inf = jnp.float32(-jnp.inf)
    gate = jnp.where(cand, jnp.where(past, gate, GATE_NEG), neg_inf)
    sel = jnp.zeros((nb, tq), F32)
    for _ in range(min(MOBA_TOP_K, nb - 1)):
        m = jnp.max(gate, axis=0, keepdims=True)
        idx = jnp.min(jnp.where(gate == m, n_id, nb), axis=0, keepdims=True)
        pick = n_id == idx
        sel = jnp.where(pick, 1.0, sel)
        gate = jnp.where(pick, neg_inf, gate)
    bias = jnp.where(jnp.logical_or(jnp.logical_and(past, sel > 0.5), n_id == q_blk), 0.0, MASK_NEG)
    pad = jnp.zeros((BIAS_ROWS - KV_GROUP, tq), F32)
    for g in range(nb // KV_GROUP):
        o_ref[0, g] = jnp.concatenate([bias[g * KV_GROUP:(g + 1) * KV_GROUP], pad], axis=0).astype(BF16)


def _mobasel(q_hm, kmean):
    nh, t, n = q_hm.shape
    nb = kmean.shape[1]
    ng = nb // KV_GROUP
    tq = min(t, 2048)
    return pl.pallas_call(
        _mobasel_kernel,
        grid=(nh, t // tq),
        in_specs=[pl.BlockSpec((1, tq, n), lambda h, j: (h, j, 0)),
                  pl.BlockSpec((1, nb, n), lambda h, j: (h, 0, 0))],
        out_specs=pl.BlockSpec((1, ng, BIAS_ROWS, tq), lambda h, j: (h, 0, 0, j)),
        out_shape=jax.ShapeDtypeStruct((nh, ng, BIAS_ROWS, t), BF16),
        compiler_params=_cparams("arbitrary", "arbitrary"),
        name="mobasel",
    )(q_hm, kmean)


def _mobaattn_kernel(qt_ref, k_ref, vt_ref, bias_ref, o_ref, sa_ref, sb_ref, m_ref, acc_ref):
    i = pl.program_id(1)
    bs = MOBA_BLOCK
    gd = i // KV_GROUP
    q_pad = jnp.zeros((K_AUG - HEAD_DIM - BIAS_ROWS, bs), BF16)
    heads = range(HEADS_PER_STEP)

    def scores_to(dst, g):
        for hh in heads:
            q_aug = jnp.concatenate([qt_ref[hh], bias_ref[hh, g], q_pad], axis=0)
            dst[hh] = jnp.dot(k_ref[hh, g], q_aug, preferred_element_type=F32)

    def update_from(src, g, keep=None):
        for hh in heads:
            s = src[hh] if keep is None else jnp.where(keep, src[hh], MASK_NEG)
            m = m_ref[hh]
            m_new = jnp.maximum(m, jnp.max(s, axis=0, keepdims=True))
            acc = jnp.exp2(m - m_new) * acc_ref[hh]
            for b in range(KV_GROUP):
                p = jnp.exp2((s[b * bs:(b + 1) * bs] - m_new).astype(BF16))
                acc = acc + jnp.dot(vt_ref[hh, g, :, pl.ds(b * bs, bs)], p, preferred_element_type=F32)
            acc_ref[hh] = acc
            m_ref[hh] = m_new

    def finish(src):
        causal = lax.broadcasted_iota(jnp.int32, (bs, bs), 0) <= lax.broadcasted_iota(jnp.int32, (bs, bs), 1)
        own = jnp.concatenate([jnp.logical_or(causal, i % KV_GROUP != b) for b in range(KV_GROUP)], axis=0)
        update_from(src, gd, own)
        for hh in heads:
            acc = acc_ref[hh]
            o_ref[hh] = acc[:HEAD_DIM] / acc[HEAD_DIM:HEAD_DIM + 1]

    m_ref[...] = jnp.full(m_ref.shape, MASK_NEG, F32)
    acc_ref[...] = jnp.zeros(acc_ref.shape, F32)
    scores_to(sa_ref, 0)

    def pair(k, _):
        g = 2 * k
        scores_to(sb_ref, g + 1)
        update_from(sa_ref, g)
        scores_to(sa_ref, g + 2)
        update_from(sb_ref, g + 1)
        return 0

    lax.fori_loop(0, gd // 2, pair, 0)

    @pl.when(gd % 2 == 1)
    def _():
        scores_to(sb_ref, gd)
        update_from(sa_ref, gd - 1)
        finish(sb_ref)

    @pl.when(gd % 2 == 0)
    def _():
        finish(sa_ref)


def _mobaattn(qt, k_aug, vt_aug, bias):
    nh, n, t = qt.shape
    gs = MOBA_BLOCK * KV_GROUP
    ng = t // gs
    hp = HEADS_PER_STEP
    return pl.pallas_call(
        _mobaattn_kernel,
        grid=(nh // hp, t // MOBA_BLOCK),
        in_specs=[pl.BlockSpec((hp, n, MOBA_BLOCK), lambda h, i: (h, 0, i)),
                  pl.BlockSpec((hp, ng, gs, K_AUG), lambda h, i: (h, 0, 0, 0)),
                  pl.BlockSpec((hp, ng, V_AUG, gs), lambda h, i: (h, 0, 0, 0)),
                  pl.BlockSpec((hp, ng, BIAS_ROWS, MOBA_BLOCK), lambda h, i: (h, 0, 0, i))],
        out_specs=pl.BlockSpec((hp, n, MOBA_BLOCK), lambda h, i: (h, 0, i)),
        out_shape=jax.ShapeDtypeStruct((nh, n, t), F32),
        scratch_shapes=[pltpu.VMEM((hp, gs, MOBA_BLOCK), F32), pltpu.VMEM((hp, gs, MOBA_BLOCK), F32),
                        pltpu.VMEM((hp, 1, MOBA_BLOCK), F32), pltpu.VMEM((hp, V_AUG, MOBA_BLOCK), F32)],
        compiler_params=_cparams("arbitrary", "arbitrary"),
        name="mobaattn",
    )(qt, k_aug, vt_aug, bias)


def _load_token_tiles(ref, n_tokens, lead=(), first=0):
    chunks = [ref[lead + (pl.ds(first * TOKEN_ROWS + ch, n_tokens, stride=TOKEN_ROWS), slice(None))]
              for ch in range(TOKEN_ROWS)]
    return jnp.concatenate(chunks, axis=1)


def _store_token_tiles(ref, val, first=0):
    for ch in range(TOKEN_ROWS):
        ref[pl.ds(first * TOKEN_ROWS + ch, val.shape[0], stride=TOKEN_ROWS), :] = val[:, ch * LANES:(ch + 1) * LANES]


def _mix_kernel(ya_ref, yb_ref, sg_ref, x_ref, wa_ref, wb_ref, wo_ref, g1_ref, n2_ref, sc_ref, sh_ref,
                rw_ref, rb_ref, x1_ref, h2_ref, eid_ref, rank_ref, wt_ref, cnt_ref, base_ref):
    i = pl.program_id(0)

    @pl.when(i == 0)
    def _():
        base_ref[...] = jnp.zeros_like(base_ref)

    d = x_ref.shape[1]
    tm = x_ref.shape[0]
    pa = jnp.dot(ya_ref[...].astype(BF16), wa_ref[...], preferred_element_type=F32)
    yb = yb_ref[...].reshape(WIDTH, tm).T
    pb = jnp.dot(yb.astype(BF16), wb_ref[...], preferred_element_type=F32)
    sg = sg_ref[...]
    mixed = sg[:, :d] * pa + sg[:, d:] * pb
    mixed = jnp.dot(mixed.astype(BF16), wo_ref[...], preferred_element_type=F32)
    x1 = x_ref[...] + g1_ref[...] * mixed
    x1_ref[...] = x1
    ms = jnp.mean(x1 * x1, axis=-1, keepdims=True)
    h2 = x1 * lax.rsqrt(ms + RMS_EPS) * n2_ref[...]
    h2 = h2 * (1.0 + sc_ref[...]) + sh_ref[...]
    _store_token_tiles(h2_ref, h2)

    logits = jnp.dot(h2, rw_ref[...], precision=HI, preferred_element_type=F32) + rb_ref[...]
    ne = logits.shape[1]
    e_id = lax.broadcasted_iota(jnp.int32, (tm, ne), 1)
    neg_inf = jnp.float32(-jnp.inf)
    work = logits
    picks, vals = [], []
    for _ in range(EXPERT_TOP_K):
        m = jnp.max(work, axis=-1, keepdims=True)
        idx = jnp.min(jnp.where(work == m, e_id, ne), axis=-1, keepdims=True)
        pick = e_id == idx
        picks.append((idx, pick))
        vals.append(m)
        work = jnp.where(pick, neg_inf, work)
    den = sum(jnp.exp(v - vals[0]) for v in vals)
    chosen = jnp.zeros((tm, ne), F32)
    for _, pick in picks:
        chosen = jnp.where(pick, 1.0, chosen)
    r_id = lax.broadcasted_iota(jnp.int32, (tm, tm), 0)
    c_id = lax.broadcasted_iota(jnp.int32, (tm, tm), 1)
    before = jnp.where(r_id > c_id, 1.0, 0.0).astype(BF16)
    ahead = jnp.dot(before, chosen.astype(BF16), preferred_element_type=F32) + base_ref[0:1, :]
    k_id = lax.broadcasted_iota(jnp.int32, (tm, EXPERT_TOP_K), 1)
    eid = jnp.zeros((tm, EXPERT_TOP_K), jnp.int32)
    rank = jnp.zeros((tm, EXPERT_TOP_K), jnp.int32)
    wt = jnp.zeros((tm, EXPERT_TOP_K), F32)
    for kk, ((idx, pick), v) in enumerate(zip(picks, vals)):
        rk = jnp.sum(jnp.where(pick, ahead, 0.0), axis=-1, keepdims=True).astype(jnp.int32)
        eid = jnp.where(k_id == kk, idx, eid)
        rank = jnp.where(k_id == kk, rk, rank)
        wt = jnp.where(k_id == kk, jnp.exp(v - vals[0]) / den, wt)
    eid_ref[...] = eid
    rank_ref[...] = rank
    wt_ref[...] = wt
    total = base_ref[0:1, :] + jnp.sum(chosen, axis=0, keepdims=True)
    base_ref[0:1, :] = total
    cnt_ref[...] = jnp.broadcast_to(total, cnt_ref.shape)


def _mix(ya, yb, sg, x2, w_a, w_b, w_o, gate1, norm2_g, scale2, shift2, router_w, router_b):
    t, d = x2.shape
    tm = 512
    ne = router_w.shape[1]
    full = lambda a: pl.BlockSpec(a.shape, lambda i: (0, 0))
    row = lambda n: pl.BlockSpec((tm, n), lambda i: (i, 0))
    return pl.pallas_call(
        _mix_kernel,
        grid=(t // tm,),
        in_specs=[row(WIDTH), pl.BlockSpec((N_HEADS, HEAD_DIM, tm), lambda i: (0, 0, i)), row(2 * d), row(d),
                  full(w_a), full(w_b), full(w_o),
                  full(gate1), full(norm2_g), full(scale2), full(shift2), full(router_w), full(router_b)],
        out_specs=[row(d), pl.BlockSpec((tm * TOKEN_ROWS, LANES), lambda i: (i, 0)),
                   row(EXPERT_TOP_K), row(EXPERT_TOP_K), row(EXPERT_TOP_K),
                   pl.BlockSpec((8, ne), lambda i: (0, 0))],
        out_shape=[jax.ShapeDtypeStruct((t, d), F32), jax.ShapeDtypeStruct((t * TOKEN_ROWS, LANES), F32),
                   jax.ShapeDtypeStruct((t, EXPERT_TOP_K), jnp.int32),
                   jax.ShapeDtypeStruct((t, EXPERT_TOP_K), jnp.int32),
                   jax.ShapeDtypeStruct((t, EXPERT_TOP_K), F32),
                   jax.ShapeDtypeStruct((8, ne), F32)],
        scratch_shapes=[pltpu.VMEM((8, ne), F32)],
        compiler_params=_cparams("arbitrary"),
        name="mix",
    )(ya, yb, sg, x2, w_a, w_b, w_o, gate1, norm2_g, scale2, shift2, router_w, router_b)


DISPATCH_TILE = 256


def _slots_kernel(off_ref, eid_ref, rank_ref, o_ref):
    eid = eid_ref[...]
    slot = rank_ref[...]
    for e in range(N_EXPERTS):
        slot = slot + jnp.where(eid == e, off_ref[e], 0)
    o_ref[...] = slot * TOKEN_ROWS


def _slots(off, eid, rank):
    n = eid.size
    lanes = 128
    shape = (n // lanes, lanes)
    full = pl.BlockSpec(shape, lambda i: (0, 0))
    out = pl.pallas_call(
        _slots_kernel,
        grid=(1,),
        in_specs=[pl.BlockSpec(memory_space=pltpu.SMEM), full, full],
        out_specs=full,
        out_shape=jax.ShapeDtypeStruct(shape, jnp.int32),
        compiler_params=_cparams("arbitrary"),
        name="slots",
    )(off, eid.reshape(shape), rank.reshape(shape))
    return out.reshape(n)


def _dispatch_kernel(slot_ref, last_ref, nu_ref, h_ref, xs_ref, zeros_ref, sem, zsem):
    tm = DISPATCH_TILE
    tile_rows = EXPERT_TILE * TOKEN_ROWS
    n_tiles = xs_ref.shape[0] // tile_rows

    @pl.when(pl.program_id(0) == 0)
    def _():
        zeros_ref[...] = jnp.zeros_like(zeros_ref)
        zero_tile = lambda j: pltpu.make_async_copy(
            zeros_ref, xs_ref.at[pl.ds(pl.multiple_of(j * tile_rows, tile_rows), tile_rows)], zsem)

        def each_padded_tile(fn):
            def expert(e, c):
                @pl.when(last_ref[e] >= 0)
                def _():
                    fn(last_ref[e])
                return c
            lax.fori_loop(0, N_EXPERTS, expert, 0)
            lax.fori_loop(nu_ref[0], n_tiles, lambda j, c: (fn(j), c)[1], 0)

        each_padded_tile(lambda j: zero_tile(j).start())
        each_padded_tile(lambda j: zero_tile(j).wait())

    def body(t, _):
        for kk in range(EXPERT_TOP_K):
            row = pl.multiple_of(slot_ref[t * EXPERT_TOP_K + kk], TOKEN_ROWS)
            pltpu.make_async_copy(h_ref.at[pl.ds(t * TOKEN_ROWS, TOKEN_ROWS)], xs_ref.at[pl.ds(row, TOKEN_ROWS)],
                                  sem).start(priority=kk % 2)
        return 0

    lax.fori_loop(0, tm, body, 0, unroll=2)
    for _ in range(EXPERT_TOP_K):
        pltpu.make_async_copy(h_ref, xs_ref.at[pl.ds(0, tm * TOKEN_ROWS)], sem).wait()


def _dispatch(slot_flat, last_tile, n_used, h2, n_slots):
    rows, lanes = h2.shape
    tm = DISPATCH_TILE
    t = rows // TOKEN_ROWS
    smem = pl.BlockSpec(memory_space=pltpu.SMEM)
    return pl.pallas_call(
        _dispatch_kernel,
        grid=(t // tm,),
        in_specs=[pl.BlockSpec((tm * EXPERT_TOP_K,), lambda i: (i,), memory_space=pltpu.SMEM), smem, smem,
                  pl.BlockSpec((tm * TOKEN_ROWS, lanes), lambda i: (i, 0))],
        out_specs=pl.BlockSpec(memory_space=pl.ANY),
        out_shape=jax.ShapeDtypeStruct((n_slots * TOKEN_ROWS, lanes), F32),
        scratch_shapes=[pltpu.VMEM((EXPERT_TILE * TOKEN_ROWS, lanes), F32),
                        pltpu.SemaphoreType.DMA(()), pltpu.SemaphoreType.DMA(())],
        compiler_params=pltpu.CompilerParams(dimension_semantics=("arbitrary",), has_side_effects=True),
        name="dispatch",
    )(slot_flat, last_tile, n_used, h2)


def _experts_kernel(te_ref, nu_ref, x_ref, wg_ref, bg_ref, wu_ref, bu_ref, wd_ref, bd_ref, y_ref,
                    wg_s, wu_s, wd_s):
    i = pl.program_id(0)
    used = i < nu_ref[0]
    new_expert = jnp.logical_or(i == 0, te_ref[i] != te_ref[jnp.maximum(i - 1, 0)])

    @pl.when(jnp.logical_and(used, new_expert))
    def _():
        wg_s[...] = wg_ref[0].astype(BF16)
        wu_s[...] = wu_ref[0].astype(BF16)
        wd_s[...] = wd_ref[0].astype(BF16)

    @pl.when(used)
    def _():
        x = _load_token_tiles(x_ref, EXPERT_TILE).astype(BF16)
        gt = jnp.minimum(jnp.dot(x, wg_s[...], preferred_element_type=F32) + bg_ref[0], SWIGLU_LIMIT)
        up = jnp.clip(jnp.dot(x, wu_s[...], preferred_element_type=F32) + bu_ref[0], -SWIGLU_LIMIT, SWIGLU_LIMIT)
        hid = (up + 1.0) * gt * _sigmoid(SWIGLU_ALPHA * gt)
        _store_token_tiles(y_ref, jnp.dot(hid.astype(BF16), wd_s[...], preferred_element_type=F32) + bd_ref[0])

    @pl.when(jnp.logical_not(used))
    def _():
        y_ref[...] = jnp.zeros_like(y_ref)


def _experts(tile_expert, n_used, xs, w_gate, b_gate, w_up, b_up, w_down, b_down):
    rows, lanes = xs.shape
    d = TOKEN_ROWS * lanes
    f = w_gate.shape[2]
    tm = EXPERT_TILE
    n_tiles = rows // (tm * TOKEN_ROWS)
    row = lambda per_token: pl.BlockSpec((tm * per_token, lanes), lambda i, te, nu: (i, 0))
    wspec = lambda a, b: pl.BlockSpec((1, a, b), lambda i, te, nu: (te[i], 0, 0))
    grid_spec = pltpu.PrefetchScalarGridSpec(
        num_scalar_prefetch=2,
        grid=(n_tiles,),
        in_specs=[row(TOKEN_ROWS), wspec(d, f), wspec(1, f), wspec(d, f), wspec(1, f), wspec(f, d), wspec(1, d)],
        out_specs=row(TOKEN_ROWS),
        scratch_shapes=[pltpu.VMEM((d, f), BF16), pltpu.VMEM((d, f), BF16), pltpu.VMEM((f, d), BF16)],
    )
    return pl.pallas_call(
        _experts_kernel,
        grid_spec=grid_spec,
        out_shape=jax.ShapeDtypeStruct((n_tiles * tm * TOKEN_ROWS, lanes), F32),
        compiler_params=_cparams("arbitrary"),
        name="experts",
    )(tile_expert, n_used, xs, w_gate, b_gate, w_up, b_up, w_down, b_down)


def _combine_kernel(slot_ref, next_ref, ys_ref, wt_ref, x1_ref, g2_ref, nf_ref, o_ref, buf, sems):
    i = pl.program_id(0)
    n = pl.num_programs(0)
    tm = DISPATCH_TILE

    def issue(slots, p):
        def body(t, _):
            for kk in range(EXPERT_TOP_K):
                row = pl.multiple_of(slots[t * EXPERT_TOP_K + kk], TOKEN_ROWS)
                pltpu.make_async_copy(ys_ref.at[pl.ds(row, TOKEN_ROWS)],
                                      buf.at[p, kk, pl.ds(t * TOKEN_ROWS, TOKEN_ROWS)],
                                      sems.at[p]).start(priority=kk % 2)
            return 0
        lax.fori_loop(0, tm, body, 0, unroll=2)

    p = i % 2

    @pl.when(i == 0)
    def _():
        issue(slot_ref, 0)

    @pl.when(i + 1 < n)
    def _():
        issue(next_ref, 1 - p)

    for kk in range(EXPERT_TOP_K):
        pltpu.make_async_copy(ys_ref.at[pl.ds(0, tm * TOKEN_ROWS)], buf.at[p, kk], sems.at[p]).wait()
    wt = wt_ref[...]
    moe = wt[:, 0:1] * _load_token_tiles(buf, tm, (p, 0))
    for kk in range(1, EXPERT_TOP_K):
        moe = moe + wt[:, kk:kk + 1] * _load_token_tiles(buf, tm, (p, kk))
    x = x1_ref[...] + g2_ref[...] * moe
    ms = jnp.mean(x * x, axis=-1, keepdims=True)
    o_ref[...] = x * lax.rsqrt(ms + RMS_EPS) * nf_ref[...]


def _combine(slot_flat, ys, wt, x1, gate2, normf_g):
    t, d = x1.shape
    tm = DISPATCH_TILE
    n = t // tm
    full = lambda a: pl.BlockSpec(a.shape, lambda i: (0, 0))
    row = lambda w: pl.BlockSpec((tm, w), lambda i: (i, 0))
    slots = lambda index: pl.BlockSpec((tm * EXPERT_TOP_K,), index, memory_space=pltpu.SMEM)
    return pl.pallas_call(
        _combine_kernel,
        grid=(n,),
        in_specs=[slots(lambda i: (i,)), slots(lambda i: (jnp.minimum(i + 1, n - 1),)),
                  pl.BlockSpec(memory_space=pl.ANY),
                  row(EXPERT_TOP_K), row(d), full(gate2), full(normf_g)],
        out_specs=row(d),
        out_shape=jax.ShapeDtypeStruct((t, d), F32),
        scratch_shapes=[pltpu.VMEM((2, EXPERT_TOP_K, tm * TOKEN_ROWS, LANES), F32), pltpu.SemaphoreType.DMA((2,))],
        compiler_params=_cparams("arbitrary"),
        name="combine",
    )(slot_flat, slot_flat, ys, wt, x1, gate2, normf_g)


def kernel(x, c, w_ada, b_ada, norm1_g, w_in, rwkv_mu, rwkv_w0, rwkv_w2, rwkv_a0, rwkv_a2, rwkv_g2, rwkv_k_k, rwkv_k_a, rwkv_r_k, rwkv_ln_w, rwkv_ln_b, w_branch_a, w_branch_b, w_out, norm2_g, router_w, router_b, exp_w_gate, exp_b_gate, exp_w_up, exp_b_up, exp_w_down, exp_b_down, normf_g):
    bsz, t, d = x.shape
    assert bsz == 1 and t % (MOBA_BLOCK * KV_GROUP) == 0
    x2 = x.reshape(t, d)
    row = lambda a: a.reshape(1, -1)

    mod = _mod(c, w_ada, b_ada)
    shift1, scale1, gate1, shift2, scale2, gate2 = [mod[:, j * d:(j + 1) * d] for j in range(6)]

    qkv_end = RWKV_PROJ + 3 * WIDTH
    w_in_b = w_in.astype(BF16)
    p_rwkv, sg, kmean, q_hm, qt, k_aug, vt_aug = _inproj(
        x2, row(norm1_g), scale1, shift1, w_in_b[:, :RWKV_PROJ], w_in_b[:, RWKV_PROJ:qkv_end], w_in_b[:, qkv_end:])

    zeros_lora = jnp.zeros((DECAY_LORA, WIDTH), F32)
    w2p = jnp.concatenate([rwkv_w2, zeros_lora], axis=0)
    a2p = jnp.concatenate([zeros_lora, rwkv_a2], axis=0)
    head_of = jnp.arange(WIDTH) // HEAD_DIM
    bd = (head_of[:, None] == head_of[None, :]).astype(BF16)
    r, lw, k, v, av, bv, g = _rwkvprep(p_rwkv, row(rwkv_mu), row(rwkv_w0), w2p.astype(BF16), row(rwkv_a0),
                                       a2p.astype(BF16), rwkv_g2.astype(BF16), row(rwkv_k_k), row(rwkv_k_a), bd)
    per_head = lambda a: a.reshape(N_HEADS, 1, HEAD_DIM)
    y_a = _rwkvscan(r, lw, k, v, av, bv, g, per_head(rwkv_ln_w), per_head(rwkv_ln_b), per_head(rwkv_r_k))

    nb = t // MOBA_BLOCK
    bias = _mobasel(q_hm, kmean.reshape(nb, N_HEADS, HEAD_DIM).transpose(1, 0, 2))
    y_b = _mobaattn(qt, k_aug, vt_aug, bias)

    x1, h2, eid, rank, wt, cnt = _mix(y_a, y_b, sg, x2, w_branch_a.astype(BF16), w_branch_b.astype(BF16),
                                      w_out.astype(BF16), gate1, row(norm2_g), scale2, shift2,
                                      router_w, row(router_b))

    counts = cnt[0].astype(jnp.int32)
    tiles_per = (counts + EXPERT_TILE - 1) // EXPERT_TILE
    tile_end = jnp.cumsum(tiles_per)
    off = (tile_end - tiles_per) * EXPERT_TILE
    n_tiles = (t * EXPERT_TOP_K + N_EXPERTS * (EXPERT_TILE - 1)) // EXPERT_TILE
    n_used = tile_end[-1:]
    tile_expert = jnp.minimum(jnp.sum(tile_end[None, :] <= jnp.arange(n_tiles)[:, None], axis=1),
                              N_EXPERTS - 1).astype(jnp.int32)
    last_used = tile_expert[jnp.maximum(n_used[0] - 1, 0)]
    tile_expert = jnp.where(jnp.arange(n_tiles) < n_used[0], tile_expert, last_used)

    slot_flat = _slots(off, eid, rank)
    last_tile = jnp.where(tiles_per > 0, tile_end - 1, -1).astype(jnp.int32)
    xs = _dispatch(slot_flat, last_tile, n_used, h2, n_tiles * EXPERT_TILE)
    f = exp_w_gate.shape[2]
    ys = _experts(tile_expert, n_used, xs,
                  exp_w_gate, exp_b_gate.reshape(N_EXPERTS, 1, f),
                  exp_w_up, exp_b_up.reshape(N_EXPERTS, 1, f),
                  exp_w_down, exp_b_down.reshape(N_EXPERTS, 1, d))
    out = _combine(slot_flat, ys, wt, x1, gate2, row(normf_g))
    return out.reshape(bsz, t, d)
```

```python
import jax
import jax.numpy as jnp
from jax import lax
from jax.experimental import pallas as pl
from jax.experimental.pallas import tpu as pltpu

F32 = jnp.float32
BF16 = jnp.bfloat16
HI = lax.Precision.HIGHEST

HEAD_DIM = 64
N_HEADS = 8
WIDTH = N_HEADS * HEAD_DIM
DECAY_LORA = 64
AAA_LORA = 64
GATE_LORA = 128
RWKV_PROJ = 3 * WIDTH + DECAY_LORA + AAA_LORA + GATE_LORA
GN_EPS = 64e-5
MOBA_BLOCK = 256
MOBA_TOP_K = 3
KV_GROUP = 4
BIAS_ROWS = 16
K_AUG = 128
V_AUG = HEAD_DIM + 16
HEADS_PER_STEP = 2
LOG2E = 1.4426950408889634
N_EXPERTS = 32
EXPERT_TOP_K = 4
SWIGLU_LIMIT = 7.0
SWIGLU_ALPHA = 1.702
RMS_EPS = 1e-5
GATE_NEG = -1e30
MASK_NEG = -1e30

SCAN_CHUNK = 64
SCAN_CHUNKS = 4
EXPERT_TILE = 256
VMEM_LIMIT = 56 * 1024 * 1024
TOKEN_ROWS = 8
LANES = 128


def _cparams(*sem):
    return pltpu.CompilerParams(dimension_semantics=sem, vmem_limit_bytes=VMEM_LIMIT)


def _sigmoid(x):
    return 1.0 / (1.0 + jnp.exp(-x))


def _mod_kernel(c_ref, w_ref, b_ref, o_ref):
    c = c_ref[...]
    s = c * _sigmoid(c)
    o_ref[...] = jnp.dot(s, w_ref[...], precision=HI, preferred_element_type=F32) + b_ref[...]


def _mod(c, w_ada, b_ada):
    d = c.shape[-1]
    n = w_ada.shape[1]
    c8 = jnp.broadcast_to(c[:1], (8, d))
    tn = 1024
    out = pl.pallas_call(
        _mod_kernel,
        grid=(n // tn,),
        in_specs=[pl.BlockSpec((8, d), lambda j: (0, 0)),
                  pl.BlockSpec((d, tn), lambda j: (0, j)),
                  pl.BlockSpec((1, tn), lambda j: (0, j))],
        out_specs=pl.BlockSpec((8, tn), lambda j: (0, j)),
        out_shape=jax.ShapeDtypeStruct((8, n), F32),
        compiler_params=_cparams("arbitrary"),
        name="mod",
    )(c8, w_ada, b_ada.reshape(1, n))
    return out[:1]


def _inproj_kernel(x_ref, g_ref, sc_ref, sh_ref, wr_ref, wq_ref, wg_ref,
                   pr_ref, sg_ref, km_ref, qh_ref, qt_ref, ka_ref, va_ref):
    i = pl.program_id(0)
    x = x_ref[...]
    tm = x.shape[0]
    ms = jnp.mean(x * x, axis=-1, keepdims=True)
    h = x * lax.rsqrt(ms + RMS_EPS) * g_ref[...]
    h = h * (1.0 + sc_ref[...]) + sh_ref[...]
    hb = h.astype(BF16)
    pr_ref[...] = jnp.dot(hb, wr_ref[...], preferred_element_type=F32)
    sg_ref[...] = _sigmoid(jnp.dot(hb, wg_ref[...], preferred_element_type=F32))
    qkv = jnp.dot(hb, wq_ref[...], preferred_element_type=F32)
    km_ref[0] = jnp.mean(qkv[:, WIDTH:2 * WIDTH], axis=0, keepdims=True)
    col = lax.broadcasted_iota(jnp.int32, (tm, K_AUG - HEAD_DIM), 1)
    indicator = jnp.where(col == i % KV_GROUP, 1.0, 0.0).astype(BF16)
    row = lax.broadcasted_iota(jnp.int32, (V_AUG - HEAD_DIM, tm), 0)
    ones_row = jnp.where(row == 0, 1.0, 0.0).astype(BF16)
    for hd in range(N_HEADS):
        lo = hd * HEAD_DIM
        q = qkv[:, lo:lo + HEAD_DIM]
        kh = qkv[:, WIDTH + lo:WIDTH + lo + HEAD_DIM]
        vh = qkv[:, 2 * WIDTH + lo:2 * WIDTH + lo + HEAD_DIM]
        qh_ref[hd] = q
        qt_ref[hd] = (q * (HEAD_DIM ** -0.5 * LOG2E)).T.astype(BF16)
        ka_ref[hd, 0] = jnp.concatenate([kh.astype(BF16), indicator], axis=1)
        va_ref[hd, 0] = jnp.concatenate([vh.T.astype(BF16), ones_row], axis=0)


def _inproj(x2, norm_g, scale, shift, w_r, w_q, w_g):
    t, d = x2.shape
    tm = MOBA_BLOCK
    gs = MOBA_BLOCK * KV_GROUP
    ng = t // gs
    full = lambda a: pl.BlockSpec(a.shape, lambda i: (0, 0))
    row = lambda n: pl.BlockSpec((tm, n), lambda i: (i, 0))
    return pl.pallas_call(
        _inproj_kernel,
        grid=(t // tm,),
        in_specs=[row(d), full(norm_g), full(scale), full(shift), full(w_r), full(w_q), full(w_g)],
        out_specs=[row(w_r.shape[1]), row(w_g.shape[1]),
                   pl.BlockSpec((1, 1, WIDTH), lambda i: (i, 0, 0)),
                   pl.BlockSpec((N_HEADS, tm, HEAD_DIM), lambda i: (0, i, 0)),
                   pl.BlockSpec((N_HEADS, HEAD_DIM, tm), lambda i: (0, 0, i)),
                   pl.BlockSpec((N_HEADS, 1, tm, K_AUG), lambda i: (0, i // KV_GROUP, i % KV_GROUP, 0)),
                   pl.BlockSpec((N_HEADS, 1, V_AUG, tm), lambda i: (0, i // KV_GROUP, 0, i % KV_GROUP))],
        out_shape=[jax.ShapeDtypeStruct((t, w_r.shape[1]), F32),
                   jax.ShapeDtypeStruct((t, w_g.shape[1]), F32),
                   jax.ShapeDtypeStruct((t // tm, 1, WIDTH), F32),
                   jax.ShapeDtypeStruct((N_HEADS, t, HEAD_DIM), F32),
                   jax.ShapeDtypeStruct((N_HEADS, HEAD_DIM, t), BF16),
                   jax.ShapeDtypeStruct((N_HEADS, ng, gs, K_AUG), BF16),
                   jax.ShapeDtypeStruct((N_HEADS, ng, V_AUG, gs), BF16)],
        compiler_params=_cparams("arbitrary"),
        name="inproj",
    )(x2, norm_g, scale, shift, w_r, w_q, w_g)


def _rwkvprep_kernel(p_ref, mu_ref, w0_ref, w2_ref, a0_ref, a2_ref, g2_ref, kk_ref, ka_ref, bd_ref,
                     r_out, lw_out, k_out, v_out, a_out, b_out, g_out, prev_ref):
    i = pl.program_id(0)

    @pl.when(i == 0)
    def _():
        prev_ref[...] = jnp.zeros_like(prev_ref)

    p = p_ref[...]
    tm = p.shape[0]
    rolled = pltpu.roll(p, 1, 0)
    row0 = lax.broadcasted_iota(jnp.int32, p.shape, 0) == 0
    p_prev = jnp.where(row0, prev_ref[0:1, :], rolled)
    prev_ref[0:1, :] = p[tm - 1:tm, :]
    ps = p + (p_prev - p) * mu_ref[...]
    r = ps[:, 0:WIDTH]
    k = ps[:, WIDTH:2 * WIDTH]
    v = ps[:, 2 * WIDTH:3 * WIDTH]
    xwa = ps[:, 3 * WIDTH:3 * WIDTH + DECAY_LORA + AAA_LORA]
    xg = ps[:, 3 * WIDTH + DECAY_LORA + AAA_LORA:RWKV_PROJ]
    dot = lambda lhs, w_ref: jnp.dot(lhs.astype(BF16), w_ref[...], preferred_element_type=F32)
    z = w0_ref[...] + dot(jnp.tanh(xwa), w2_ref)
    nz = -z
    softplus = jnp.maximum(nz, 0.0) + jnp.log(1.0 + jnp.exp(-jnp.abs(nz)))
    w = -softplus - 0.5
    a = _sigmoid(a0_ref[...] + dot(xwa, a2_ref))
    g = dot(_sigmoid(xg), g2_ref)
    kk = k * kk_ref[...]
    sq = kk * kk
    sq_hi = sq.astype(BF16)
    ss = dot(sq_hi, bd_ref) + dot(sq - sq_hi.astype(F32), bd_ref)
    kk = kk / jnp.maximum(jnp.sqrt(ss), 1e-12)
    outs = ((r_out, r), (lw_out, -jnp.exp(w)),
            (k_out, k * (1.0 + (a - 1.0) * ka_ref[...])), (v_out, v), (a_out, -kk), (b_out, kk * a), (g_out, g))
    for ref, val in outs:
        for hd in range(N_HEADS):
            ref[hd] = val[:, hd * HEAD_DIM:(hd + 1) * HEAD_DIM]


def _rwkvprep(p_rwkv, mu, w0, w2p, a0, a2p, g2, k_k, k_a, bd):
    t = p_rwkv.shape[0]
    tm = 256
    full = lambda a: pl.BlockSpec(a.shape, lambda i: (0, 0))
    row = lambda n: pl.BlockSpec((tm, n), lambda i: (i, 0))
    outs = [jax.ShapeDtypeStruct((N_HEADS, t, HEAD_DIM), F32)] * 7
    return pl.pallas_call(
        _rwkvprep_kernel,
        grid=(t // tm,),
        in_specs=[row(RWKV_PROJ), full(mu), full(w0), full(w2p), full(a0), full(a2p), full(g2),
                  full(k_k), full(k_a), full(bd)],
        out_specs=[pl.BlockSpec((N_HEADS, tm, HEAD_DIM), lambda i: (0, i, 0))] * 7,
        out_shape=outs,
        scratch_shapes=[pltpu.VMEM((8, RWKV_PROJ), F32)],
        compiler_params=_cparams("arbitrary"),
        name="rwkvprep",
    )(p_rwkv, mu, w0, w2p, a0, a2p, g2, k_k, k_a, bd)


def _bmm(a, b):
    return jnp.einsum("hmk,hkn->hmn", a.astype(BF16), b.astype(BF16), preferred_element_type=F32)


def _bmm_nt(a, b):
    return jnp.einsum("hmk,hnk->hmn", a.astype(BF16), b.astype(BF16), preferred_element_type=F32)


def _bmm_tn(a, b):
    return jnp.einsum("hkm,hkn->hmn", a.astype(BF16), b.astype(BF16), preferred_element_type=F32)


def _rwkvscan_kernel(r_ref, lw_ref, k_ref, v_ref, a_ref, b_ref, g_ref, lnw_ref, lnb_ref, rk_ref,
                     y_ref, s_ref):
    i = pl.program_id(0)

    @pl.when(i == 0)
    def _():
        s_ref[...] = jnp.zeros_like(s_ref)

    c = SCAN_CHUNK
    nc = SCAN_CHUNKS
    nh, _, n = r_ref.shape
    nb = nh * nc
    split = lambda ref: ref[...].reshape(nb, c, n)
    r, lw, k, v, a, b = (split(ref) for ref in (r_ref, lw_ref, k_ref, v_ref, a_ref, b_ref))

    row = lax.broadcasted_iota(jnp.int32, (nb, c, c), 1)
    col = lax.broadcasted_iota(jnp.int32, (nb, c, c), 2)
    lower = row >= col
    strict = row > col
    lw_hi = lw.astype(BF16).astype(F32)
    lw_mid = (lw - lw_hi).astype(BF16).astype(F32)
    cum = _bmm(lower, lw_hi) + _bmm(lower, lw_mid) + _bmm(lower, lw - lw_hi - lw_mid)
    tot = cum[:, c - 1:c, :]
    g_in = jnp.exp(cum)
    g_ex = jnp.exp(cum - lw)
    g_inv = jnp.exp(-cum)
    g_rem = jnp.exp(tot - cum)
    a_t = a * g_ex
    r_t = r * g_in
    b_t = b * g_inv
    k_t = k * g_inv
    l_ab = jnp.where(strict, _bmm_nt(a_t, b_t), 0.0)
    l_ak = jnp.where(strict, _bmm_nt(a_t, k_t), 0.0)
    m_rb = jnp.where(lower, _bmm_nt(r_t, b_t), 0.0)
    m_rk = jnp.where(lower, _bmm_nt(r_t, k_t), 0.0)
    same16 = (row // 16) == (col // 16)
    same32 = (row // 32) == (col // 32)
    diag16 = jnp.where(same16, l_ab, 0.0)
    inv = jnp.where(row == col, 1.0, 0.0) + diag16
    lp = diag16
    for _ in range(3):
        lp = _bmm(lp, lp)
        inv = inv + _bmm(inv, lp)
    off32 = jnp.where(jnp.logical_and(same32, jnp.logical_not(same16)), l_ab, 0.0)
    inv = inv + _bmm(_bmm(inv, off32), inv)
    off64 = jnp.where(same32, 0.0, l_ab)
    inv = inv + _bmm(_bmm(inv, off64), inv)
    a_hat = _bmm(inv, a_t)
    w_hat = _bmm(inv, _bmm(l_ak, v))
    r_hat = r_t + _bmm(m_rb, a_hat)
    y0 = _bmm(m_rb, w_hat) + _bmm(m_rk, v)
    b_hat = b * g_rem
    k_hat = k * g_rem
    decay = jnp.exp(tot)

    chunk = lambda x, j: x.reshape(nh, nc, x.shape[1], n)[:, j]
    s = s_ref[...]
    ys = []
    for j in range(nc):
        u = _bmm_nt(chunk(a_hat, j), s) + chunk(w_hat, j)
        ys.append(_bmm_nt(chunk(r_hat, j), s) + chunk(y0, j))
        s = s * chunk(decay, j) + _bmm_tn(u, chunk(b_hat, j)) + _bmm_tn(chunk(v, j), chunk(k_hat, j))
    s_ref[...] = s
    y = jnp.concatenate(ys, axis=1)

    r = r_ref[...]
    mean = jnp.mean(y, axis=-1, keepdims=True)
    yc = y - mean
    var = jnp.mean(yc * yc, axis=-1, keepdims=True)
    yn = yc * lax.rsqrt(var + GN_EPS) * lnw_ref[...] + lnb_ref[...]
    bonus = jnp.sum(r * k_ref[...] * rk_ref[...], axis=-1, keepdims=True) * v_ref[...]
    out = (yn + bonus) * g_ref[...]
    for hd in range(nh):
        y_ref[:, hd * n:(hd + 1) * n] = out[hd]


def _rwkvscan(r, lw, k, v, a, b, g, ln_w, ln_b, r_k):
    nh, t, n = r.shape
    rows = SCAN_CHUNK * SCAN_CHUNKS
    blk = pl.BlockSpec((nh, rows, n), lambda i: (0, i, 0))
    par = pl.BlockSpec((nh, 1, n), lambda i: (0, 0, 0))
    return pl.pallas_call(
        _rwkvscan_kernel,
        grid=(t // rows,),
        in_specs=[blk] * 7 + [par] * 3,
        out_specs=pl.BlockSpec((rows, nh * n), lambda i: (i, 0)),
        out_shape=jax.ShapeDtypeStruct((t, nh * n), F32),
        scratch_shapes=[pltpu.VMEM((nh, n, n), F32)],
        compiler_params=_cparams("arbitrary"),
        name="rwkvscan",
    )(r, lw, k, v, a, b, g, ln_w, ln_b, r_k)


def _mobasel_kernel(q_ref, km_ref, o_ref):
    j = pl.program_id(1)
    q = q_ref[0]
    km = km_ref[0]
    nb = km.shape[0]
    tq = q.shape[0]
    gate = lax.dot_general(km, q, (((1,), (1,)), ((), ())), precision=HI, preferred_element_type=F32)
    n_id = lax.broadcasted_iota(jnp.int32, (nb, tq), 0)
    t_id = lax.broadcasted_iota(jnp.int32, (nb, tq), 1) + j * tq
    q_blk = t_id // MOBA_BLOCK
    cand = n_id < nb - 1
    past = n_id < q_blk
    neg_inf = jnp.float32(-jnp.inf)
    gate = jnp.where(cand, jnp.where(past, gate, GATE_NEG), neg_inf)
    sel = jnp.zeros((nb, tq), F32)
    for _ in range(min(MOBA_TOP_K, nb - 1)):
        m = jnp.max(gate, axis=0, keepdims=True)
        idx = jnp.min(jnp.where(gate == m, n_id, nb), axis=0, keepdims=True)
        pick = n_id == idx
        sel = jnp.where(pick, 1.0, sel)
        gate = jnp.where(pick, neg_inf, gate)
    bias = jnp.where(jnp.logical_or(jnp.logical_and(past, sel > 0.5), n_id == q_blk), 0.0, MASK_NEG)
    pad = jnp.zeros((BIAS_ROWS - KV_GROUP, tq), F32)
    for g in range(nb // KV_GROUP):
        o_ref[0, g] = jnp.concatenate([bias[g * KV_GROUP:(g + 1) * KV_GROUP], pad], axis=0).astype(BF16)


def _mobasel(q_hm, kmean):
    nh, t, n = q_hm.shape
    nb = kmean.shape[1]
    ng = nb // KV_GROUP
    tq = min(t, 2048)
    return pl.pallas_call(
        _mobasel_kernel,
        grid=(nh, t // tq),
        in_specs=[pl.BlockSpec((1, tq, n), lambda h, j: (h, j, 0)),
                  pl.BlockSpec((1, nb, n), lambda h, j: (h, 0, 0))],
        out_specs=pl.BlockSpec((1, ng, BIAS_ROWS, tq), lambda h, j: (h, 0, 0, j)),
        out_shape=jax.ShapeDtypeStruct((nh, ng, BIAS_ROWS, t), BF16),
        compiler_params=_cparams("arbitrary", "arbitrary"),
        name="mobasel",
    )(q_hm, kmean)


def _mobaattn_kernel(qt_ref, k_ref, vt_ref, bias_ref, o_ref, sa_ref, sb_ref, m_ref, acc_ref):
    i = pl.program_id(1)
    bs = MOBA_BLOCK
    gd = i // KV_GROUP
    q_pad = jnp.zeros((K_AUG - HEAD_DIM - BIAS_ROWS, bs), BF16)
    heads = range(HEADS_PER_STEP)

    def scores_to(dst, g):
        for hh in heads:
            q_aug = jnp.concatenate([qt_ref[hh], bias_ref[hh, g], q_pad], axis=0)
            dst[hh] = jnp.dot(k_ref[hh, g], q_aug, preferred_element_type=F32)

    def update_from(src, g, keep=None):
        for hh in heads:
            s = src[hh] if keep is None else jnp.where(keep, src[hh], MASK_NEG)
            m = m_ref[hh]
            m_new = jnp.maximum(m, jnp.max(s, axis=0, keepdims=True))
            acc = jnp.exp2(m - m_new) * acc_ref[hh]
            for b in range(KV_GROUP):
                p = jnp.exp2((s[b * bs:(b + 1) * bs] - m_new).astype(BF16))
                acc = acc + jnp.dot(vt_ref[hh, g, :, pl.ds(b * bs, bs)], p, preferred_element_type=F32)
            acc_ref[hh] = acc
            m_ref[hh] = m_new

    def finish(src):
        causal = lax.broadcasted_iota(jnp.int32, (bs, bs), 0) <= lax.broadcasted_iota(jnp.int32, (bs, bs), 1)
        own = jnp.concatenate([jnp.logical_or(causal, i % KV_GROUP != b) for b in range(KV_GROUP)], axis=0)
        update_from(src, gd, own)
        for hh in heads:
            acc = acc_ref[hh]
            o_ref[hh] = acc[:HEAD_DIM] / acc[HEAD_DIM:HEAD_DIM + 1]

    m_ref[...] = jnp.full(m_ref.shape, MASK_NEG, F32)
    acc_ref[...] = jnp.zeros(acc_ref.shape, F32)
    scores_to(sa_ref, 0)

    def pair(k, _):
        g = 2 * k
        scores_to(sb_ref, g + 1)
        update_from(sa_ref, g)
        scores_to(sa_ref, g + 2)
        update_from(sb_ref, g + 1)
        return 0

    lax.fori_loop(0, gd // 2, pair, 0)

    @pl.when(gd % 2 == 1)
    def _():
        scores_to(sb_ref, gd)
        update_from(sa_ref, gd - 1)
        finish(sb_ref)

    @pl.when(gd % 2 == 0)
    def _():
        finish(sa_ref)


def _mobaattn(qt, k_aug, vt_aug, bias):
    nh, n, t = qt.shape
    gs = MOBA_BLOCK * KV_GROUP
    ng = t // gs
    hp = HEADS_PER_STEP
    return pl.pallas_call(
        _mobaattn_kernel,
        grid=(nh // hp, t // MOBA_BLOCK),
        in_specs=[pl.BlockSpec((hp, n, MOBA_BLOCK), lambda h, i: (h, 0, i)),
                  pl.BlockSpec((hp, ng, gs, K_AUG), lambda h, i: (h, 0, 0, 0)),
                  pl.BlockSpec((hp, ng, V_AUG, gs), lambda h, i: (h, 0, 0, 0)),
                  pl.BlockSpec((hp, ng, BIAS_ROWS, MOBA_BLOCK), lambda h, i: (h, 0, 0, i))],
        out_specs=pl.BlockSpec((hp, n, MOBA_BLOCK), lambda h, i: (h, 0, i)),
        out_shape=jax.ShapeDtypeStruct((nh, n, t), F32),
        scratch_shapes=[pltpu.VMEM((hp, gs, MOBA_BLOCK), F32), pltpu.VMEM((hp, gs, MOBA_BLOCK), F32),
                        pltpu.VMEM((hp, 1, MOBA_BLOCK), F32), pltpu.VMEM((hp, V_AUG, MOBA_BLOCK), F32)],
        compiler_params=_cparams("arbitrary", "arbitrary"),
        name="mobaattn",
    )(qt, k_aug, vt_aug, bias)


def _load_token_tiles(ref, n_tokens, lead=(), first=0):
    chunks = [ref[lead + (pl.ds(first * TOKEN_ROWS + ch, n_tokens, stride=TOKEN_ROWS), slice(None))]
              for ch in range(TOKEN_ROWS)]
    return jnp.concatenate(chunks, axis=1)


def _store_token_tiles(ref, val, first=0):
    for ch in range(TOKEN_ROWS):
        ref[pl.ds(first * TOKEN_ROWS + ch, val.shape[0], stride=TOKEN_ROWS), :] = val[:, ch * LANES:(ch + 1) * LANES]


def _mix_kernel(ya_ref, yb_ref, sg_ref, x_ref, wa_ref, wb_ref, wo_ref, g1_ref, n2_ref, sc_ref, sh_ref,
                rw_ref, rb_ref, x1_ref, h2_ref, eid_ref, rank_ref, wt_ref, cnt_ref, base_ref):
    i = pl.program_id(0)

    @pl.when(i == 0)
    def _():
        base_ref[...] = jnp.zeros_like(base_ref)

    d = x_ref.shape[1]
    tm = x_ref.shape[0]
    pa = jnp.dot(ya_ref[...].astype(BF16), wa_ref[...], preferred_element_type=F32)
    yb = yb_ref[...].reshape(WIDTH, tm).T
    pb = jnp.dot(yb.astype(BF16), wb_ref[...], preferred_element_type=F32)
    sg = sg_ref[...]
    mixed = sg[:, :d] * pa + sg[:, d:] * pb
    mixed = jnp.dot(mixed.astype(BF16), wo_ref[...], preferred_element_type=F32)
    x1 = x_ref[...] + g1_ref[...] * mixed
    x1_ref[...] = x1
    ms = jnp.mean(x1 * x1, axis=-1, keepdims=True)
    h2 = x1 * lax.rsqrt(ms + RMS_EPS) * n2_ref[...]
    h2 = h2 * (1.0 + sc_ref[...]) + sh_ref[...]
    _store_token_tiles(h2_ref, h2)

    ne = rw_ref.shape[1] // 2
    h2_hi = h2.astype(BF16)
    h2_lo = (h2 - h2_hi.astype(F32)).astype(BF16)
    by_hi = jnp.dot(h2_hi, rw_ref[...], preferred_element_type=F32)
    by_lo = jnp.dot(h2_lo, rw_ref[:, :ne], preferred_element_type=F32)
    logits = by_hi[:, :ne] + by_hi[:, ne:] + by_lo + rb_ref[...]
    e_id = lax.broadcasted_iota(jnp.int32, (tm, ne), 1)
    neg_inf = jnp.float32(-jnp.inf)
    work = logits
    picks, vals = [], []
    for _ in range(EXPERT_TOP_K):
        m = jnp.max(work, axis=-1, keepdims=True)
        idx = jnp.min(jnp.where(work == m, e_id, ne), axis=-1, keepdims=True)
        pick = e_id == idx
        picks.append((idx, pick))
        vals.append(m)
        work = jnp.where(pick, neg_inf, work)
    den = sum(jnp.exp(v - vals[0]) for v in vals)
    chosen = jnp.zeros((tm, ne), F32)
    for _, pick in picks:
        chosen = jnp.where(pick, 1.0, chosen)
    r_id = lax.broadcasted_iota(jnp.int32, (tm, tm), 0)
    c_id = lax.broadcasted_iota(jnp.int32, (tm, tm), 1)
    before = jnp.where(r_id > c_id, 1.0, 0.0).astype(BF16)
    ahead = jnp.dot(before, chosen.astype(BF16), preferred_element_type=F32) + base_ref[0:1, :]
    k_id = lax.broadcasted_iota(jnp.int32, (tm, EXPERT_TOP_K), 1)
    eid = jnp.zeros((tm, EXPERT_TOP_K), jnp.int32)
    rank = jnp.zeros((tm, EXPERT_TOP_K), jnp.int32)
    wt = jnp.zeros((tm, EXPERT_TOP_K), F32)
    for kk, ((idx, pick), v) in enumerate(zip(picks, vals)):
        rk = jnp.sum(jnp.where(pick, ahead, 0.0), axis=-1, keepdims=True).astype(jnp.int32)
        eid = jnp.where(k_id == kk, idx, eid)
        rank = jnp.where(k_id == kk, rk, rank)
        wt = jnp.where(k_id == kk, jnp.exp(v - vals[0]) / den, wt)
    eid_ref[...] = eid
    rank_ref[...] = rank
    wt_ref[...] = wt
    total = base_ref[0:1, :] + jnp.sum(chosen, axis=0, keepdims=True)
    base_ref[0:1, :] = total
    cnt_ref[...] = jnp.broadcast_to(total, cnt_ref.shape)


def _mix(ya, yb, sg, x2, w_a, w_b, w_o, gate1, norm2_g, scale2, shift2, router_w, router_b):
    t, d = x2.shape
    tm = 512
    ne = router_w.shape[1]
    rw_hi = router_w.astype(BF16)
    router_w = jnp.concatenate([rw_hi, (router_w - rw_hi.astype(F32)).astype(BF16)], axis=1)
    full = lambda a: pl.BlockSpec(a.shape, lambda i: (0, 0))
    row = lambda n: pl.BlockSpec((tm, n), lambda i: (i, 0))
    return pl.pallas_call(
        _mix_kernel,
        grid=(t // tm,),
        in_specs=[row(WIDTH), pl.BlockSpec((N_HEADS, HEAD_DIM, tm), lambda i: (0, 0, i)), row(2 * d), row(d),
                  full(w_a), full(w_b), full(w_o),
                  full(gate1), full(norm2_g), full(scale2), full(shift2), full(router_w), full(router_b)],
        out_specs=[row(d), pl.BlockSpec((tm * TOKEN_ROWS, LANES), lambda i: (i, 0)),
                   row(EXPERT_TOP_K), row(EXPERT_TOP_K), row(EXPERT_TOP_K),
                   pl.BlockSpec((8, ne), lambda i: (0, 0))],
        out_shape=[jax.ShapeDtypeStruct((t, d), F32), jax.ShapeDtypeStruct((t * TOKEN_ROWS, LANES), F32),
                   jax.ShapeDtypeStruct((t, EXPERT_TOP_K), jnp.int32),
                   jax.ShapeDtypeStruct((t, EXPERT_TOP_K), jnp.int32),
                   jax.ShapeDtypeStruct((t, EXPERT_TOP_K), F32),
                   jax.ShapeDtypeStruct((8, ne), F32)],
        scratch_shapes=[pltpu.VMEM((8, ne), F32)],
        compiler_params=_cparams("arbitrary"),
        name="mix",
    )(ya, yb, sg, x2, w_a, w_b, w_o, gate1, norm2_g, scale2, shift2, router_w, router_b)


DISPATCH_TILE = 256


def _slots_kernel(off_ref, eid_ref, rank_ref, o_ref):
    eid = eid_ref[...]
    slot = rank_ref[...]
    for e in range(N_EXPERTS):
        slot = slot + jnp.where(eid == e, off_ref[e], 0)
    o_ref[...] = slot * TOKEN_ROWS


def _slots(off, eid, rank):
    n = eid.size
    lanes = 128
    shape = (n // lanes, lanes)
    full = pl.BlockSpec(shape, lambda i: (0, 0))
    out = pl.pallas_call(
        _slots_kernel,
        grid=(1,),
        in_specs=[pl.BlockSpec(memory_space=pltpu.SMEM), full, full],
        out_specs=full,
        out_shape=jax.ShapeDtypeStruct(shape, jnp.int32),
        compiler_params=_cparams("arbitrary"),
        name="slots",
    )(off, eid.reshape(shape), rank.reshape(shape))
    return out.reshape(n)


def _dispatch_kernel(slot_ref, last_ref, nu_ref, h_ref, xs_ref, zeros_ref, sem, zsem):
    tm = DISPATCH_TILE
    tile_rows = EXPERT_TILE * TOKEN_ROWS
    n_tiles = xs_ref.shape[0] // tile_rows

    @pl.when(pl.program_id(0) == 0)
    def _():
        zeros_ref[...] = jnp.zeros_like(zeros_ref)
        zero_tile = lambda j: pltpu.make_async_copy(
            zeros_ref, xs_ref.at[pl.ds(pl.multiple_of(j * tile_rows, tile_rows), tile_rows)], zsem)

        def each_padded_tile(fn):
            def expert(e, c):
                @pl.when(last_ref[e] >= 0)
                def _():
                    fn(last_ref[e])
                return c
            lax.fori_loop(0, N_EXPERTS, expert, 0)
            lax.fori_loop(nu_ref[0], n_tiles, lambda j, c: (fn(j), c)[1], 0)

        each_padded_tile(lambda j: zero_tile(j).start())
        each_padded_tile(lambda j: zero_tile(j).wait())

    def body(t, _):
        for kk in range(EXPERT_TOP_K):
            row = pl.multiple_of(slot_ref[t * EXPERT_TOP_K + kk], TOKEN_ROWS)
            pltpu.make_async_copy(h_ref.at[pl.ds(t * TOKEN_ROWS, TOKEN_ROWS)], xs_ref.at[pl.ds(row, TOKEN_ROWS)],
                                  sem).start(priority=kk % 2)
        return 0

    lax.fori_loop(0, tm, body, 0, unroll=2)
    for _ in range(EXPERT_TOP_K):
        pltpu.make_async_copy(h_ref, xs_ref.at[pl.ds(0, tm * TOKEN_ROWS)], sem).wait()


def _dispatch(slot_flat, last_tile, n_used, h2, n_slots):
    rows, lanes = h2.shape
    tm = DISPATCH_TILE
    t = rows // TOKEN_ROWS
    smem = pl.BlockSpec(memory_space=pltpu.SMEM)
    return pl.pallas_call(
        _dispatch_kernel,
        grid=(t // tm,),
        in_specs=[pl.BlockSpec((tm * EXPERT_TOP_K,), lambda i: (i,), memory_space=pltpu.SMEM), smem, smem,
                  pl.BlockSpec((tm * TOKEN_ROWS, lanes), lambda i: (i, 0))],
        out_specs=pl.BlockSpec(memory_space=pl.ANY),
        out_shape=jax.ShapeDtypeStruct((n_slots * TOKEN_ROWS, lanes), F32),
        scratch_shapes=[pltpu.VMEM((EXPERT_TILE * TOKEN_ROWS, lanes), F32),
                        pltpu.SemaphoreType.DMA(()), pltpu.SemaphoreType.DMA(())],
        compiler_params=pltpu.CompilerParams(dimension_semantics=("arbitrary",), has_side_effects=True),
        name="dispatch",
    )(slot_flat, last_tile, n_used, h2)


def _experts_kernel(te_ref, nu_ref, x_ref, wg_ref, bg_ref, wu_ref, bu_ref, wd_ref, bd_ref, y_ref,
                    wg_s, wu_s, wd_s):
    i = pl.program_id(0)
    used = i < nu_ref[0]
    new_expert = jnp.logical_or(i == 0, te_ref[i] != te_ref[jnp.maximum(i - 1, 0)])

    @pl.when(jnp.logical_and(used, new_expert))
    def _():
        wg_s[...] = wg_ref[0].astype(BF16)
        wu_s[...] = wu_ref[0].astype(BF16)
        wd_s[...] = wd_ref[0].astype(BF16)

    @pl.when(used)
    def _():
        x = _load_token_tiles(x_ref, EXPERT_TILE).astype(BF16)
        gt = jnp.minimum(jnp.dot(x, wg_s[...], preferred_element_type=F32) + bg_ref[0], SWIGLU_LIMIT)
        up = jnp.clip(jnp.dot(x, wu_s[...], preferred_element_type=F32) + bu_ref[0], -SWIGLU_LIMIT, SWIGLU_LIMIT)
        hid = (up + 1.0) * gt * _sigmoid(SWIGLU_ALPHA * gt)
        _store_token_tiles(y_ref, jnp.dot(hid.astype(BF16), wd_s[...], preferred_element_type=F32) + bd_ref[0])

    @pl.when(jnp.logical_not(used))
    def _():
        y_ref[...] = jnp.zeros_like(y_ref)


def _experts(tile_expert, n_used, xs, w_gate, b_gate, w_up, b_up, w_down, b_down):
    rows, lanes = xs.shape
    d = TOKEN_ROWS * lanes
    f = w_gate.shape[2]
    tm = EXPERT_TILE
    n_tiles = rows // (tm * TOKEN_ROWS)
    row = lambda per_token: pl.BlockSpec((tm * per_token, lanes), lambda i, te, nu: (i, 0))
    wspec = lambda a, b: pl.BlockSpec((1, a, b), lambda i, te, nu: (te[i], 0, 0))
    grid_spec = pltpu.PrefetchScalarGridSpec(
        num_scalar_prefetch=2,
        grid=(n_tiles,),
        in_specs=[row(TOKEN_ROWS), wspec(d, f), wspec(1, f), wspec(d, f), wspec(1, f), wspec(f, d), wspec(1, d)],
        out_specs=row(TOKEN_ROWS),
        scratch_shapes=[pltpu.VMEM((d, f), BF16), pltpu.VMEM((d, f), BF16), pltpu.VMEM((f, d), BF16)],
    )
    return pl.pallas_call(
        _experts_kernel,
        grid_spec=grid_spec,
        out_shape=jax.ShapeDtypeStruct((n_tiles * tm * TOKEN_ROWS, lanes), F32),
        compiler_params=_cparams("arbitrary"),
        name="experts",
    )(tile_expert, n_used, xs, w_gate, b_gate, w_up, b_up, w_down, b_down)


def _combine_kernel(slot_ref, next_ref, ys_ref, wt_ref, x1_ref, g2_ref, nf_ref, o_ref, buf, sems):
    i = pl.program_id(0)
    n = pl.num_programs(0)
    tm = DISPATCH_TILE

    def issue(slots, p):
        def body(t, _):
            for kk in range(EXPERT_TOP_K):
                row = pl.multiple_of(slots[t * EXPERT_TOP_K + kk], TOKEN_ROWS)
                pltpu.make_async_copy(ys_ref.at[pl.ds(row, TOKEN_ROWS)],
                                      buf.at[p, kk, pl.ds(t * TOKEN_ROWS, TOKEN_ROWS)],
                                      sems.at[p]).start(priority=kk % 2)
            return 0
        lax.fori_loop(0, tm, body, 0, unroll=2)

    p = i % 2

    @pl.when(i == 0)
    def _():
        issue(slot_ref, 0)

    @pl.when(i + 1 < n)
    def _():
        issue(next_ref, 1 - p)

    for kk in range(EXPERT_TOP_K):
        pltpu.make_async_copy(ys_ref.at[pl.ds(0, tm * TOKEN_ROWS)], buf.at[p, kk], sems.at[p]).wait()
    wt = wt_ref[...]
    moe = wt[:, 0:1] * _load_token_tiles(buf, tm, (p, 0))
    for kk in range(1, EXPERT_TOP_K):
        moe = moe + wt[:, kk:kk + 1] * _load_token_tiles(buf, tm, (p, kk))
    x = x1_ref[...] + g2_ref[...] * moe
    ms = jnp.mean(x * x, axis=-1, keepdims=True)
    o_ref[...] = x * lax.rsqrt(ms + RMS_EPS) * nf_ref[...]


def _combine(slot_flat, ys, wt, x1, gate2, normf_g):
    t, d = x1.shape
    tm = DISPATCH_TILE
    n = t // tm
    full = lambda a: pl.BlockSpec(a.shape, lambda i: (0, 0))
    row = lambda w: pl.BlockSpec((tm, w), lambda i: (i, 0))
    slots = lambda index: pl.BlockSpec((tm * EXPERT_TOP_K,), index, memory_space=pltpu.SMEM)
    return pl.pallas_call(
        _combine_kernel,
        grid=(n,),
        in_specs=[slots(lambda i: (i,)), slots(lambda i: (jnp.minimum(i + 1, n - 1),)),
                  pl.BlockSpec(memory_space=pl.ANY),
                  row(EXPERT_TOP_K), row(d), full(gate2), full(normf_g)],
        out_specs=row(d),
        out_shape=jax.ShapeDtypeStruct((t, d), F32),
        scratch_shapes=[pltpu.VMEM((2, EXPERT_TOP_K, tm * TOKEN_ROWS, LANES), F32), pltpu.SemaphoreType.DMA((2,))],
        compiler_params=_cparams("arbitrary"),
        name="combine",
    )(slot_flat, slot_flat, ys, wt, x1, gate2, normf_g)


def kernel(x, c, w_ada, b_ada, norm1_g, w_in, rwkv_mu, rwkv_w0, rwkv_w2, rwkv_a0, rwkv_a2, rwkv_g2, rwkv_k_k, rwkv_k_a, rwkv_r_k, rwkv_ln_w, rwkv_ln_b, w_branch_a, w_branch_b, w_out, norm2_g, router_w, router_b, exp_w_gate, exp_b_gate, exp_w_up, exp_b_up, exp_w_down, exp_b_down, normf_g):
    bsz, t, d = x.shape
    assert bsz == 1 and t % (MOBA_BLOCK * KV_GROUP) == 0
    x2 = x.reshape(t, d)
    row = lambda a: a.reshape(1, -1)

    mod = _mod(c, w_ada, b_ada)
    shift1, scale1, gate1, shift2, scale2, gate2 = [mod[:, j * d:(j + 1) * d] for j in range(6)]

    qkv_end = RWKV_PROJ + 3 * WIDTH
    w_in_b = w_in.astype(BF16)
    p_rwkv, sg, kmean, q_hm, qt, k_aug, vt_aug = _inproj(
        x2, row(norm1_g), scale1, shift1, w_in_b[:, :RWKV_PROJ], w_in_b[:, RWKV_PROJ:qkv_end], w_in_b[:, qkv_end:])

    zeros_lora = jnp.zeros((DECAY_LORA, WIDTH), F32)
    w2p = jnp.concatenate([rwkv_w2, zeros_lora], axis=0)
    a2p = jnp.concatenate([zeros_lora, rwkv_a2], axis=0)
    head_of = jnp.arange(WIDTH) // HEAD_DIM
    bd = (head_of[:, None] == head_of[None, :]).astype(BF16)
    r, lw, k, v, av, bv, g = _rwkvprep(p_rwkv, row(rwkv_mu), row(rwkv_w0), w2p.astype(BF16), row(rwkv_a0),
                                       a2p.astype(BF16), rwkv_g2.astype(BF16), row(rwkv_k_k), row(rwkv_k_a), bd)
    per_head = lambda a: a.reshape(N_HEADS, 1, HEAD_DIM)
    y_a = _rwkvscan(r, lw, k, v, av, bv, g, per_head(rwkv_ln_w), per_head(rwkv_ln_b), per_head(rwkv_r_k))

    nb = t // MOBA_BLOCK
    bias = _mobasel(q_hm, kmean.reshape(nb, N_HEADS, HEAD_DIM).transpose(1, 0, 2))
    y_b = _mobaattn(qt, k_aug, vt_aug, bias)

    x1, h2, eid, rank, wt, cnt = _mix(y_a, y_b, sg, x2, w_branch_a.astype(BF16), w_branch_b.astype(BF16),
                                      w_out.astype(BF16), gate1, row(norm2_g), scale2, shift2,
                                      router_w, row(router_b))

    counts = cnt[0].astype(jnp.int32)
    tiles_per = (counts + EXPERT_TILE - 1) // EXPERT_TILE
    tile_end = jnp.cumsum(tiles_per)
    off = (tile_end - tiles_per) * EXPERT_TILE
    n_tiles = (t * EXPERT_TOP_K + N_EXPERTS * (EXPERT_TILE - 1)) // EXPERT_TILE
    n_used = tile_end[-1:]
    tile_expert = jnp.minimum(jnp.sum(tile_end[None, :] <= jnp.arange(n_tiles)[:, None], axis=1),
                              N_EXPERTS - 1).astype(jnp.int32)
    last_used = tile_expert[jnp.maximum(n_used[0] - 1, 0)]
    tile_expert = jnp.where(jnp.arange(n_tiles) < n_used[0], tile_expert, last_used)

    slot_flat = _slots(off, eid, rank)
    last_tile = jnp.where(tiles_per > 0, tile_end - 1, -1).astype(jnp.int32)
    xs = _dispatch(slot_flat, last_tile, n_used, h2, n_tiles * EXPERT_TILE)
    f = exp_w_gate.shape[2]
    ys = _experts(tile_expert, n_used, xs,
                  exp_w_gate, exp_b_gate.reshape(N_EXPERTS, 1, f),
                  exp_w_up, exp_b_up.reshape(N_EXPERTS, 1, f),
                  exp_w_down, exp_b_down.reshape(N_EXPERTS, 1, d))
    out = _combine(slot_flat, ys, wt, x1, gate2, row(normf_g))
    return out.reshape(bsz, t, d)
```

```python
import jax
import jax.numpy as jnp
from jax import lax
from jax.experimental import pallas as pl
from jax.experimental.pallas import tpu as pltpu

F32 = jnp.float32
BF16 = jnp.bfloat16
HI = lax.Precision.HIGHEST

HEAD_DIM = 64
N_HEADS = 8
WIDTH = N_HEADS * HEAD_DIM
DECAY_LORA = 64
AAA_LORA = 64
GATE_LORA = 128
RWKV_PROJ = 3 * WIDTH + DECAY_LORA + AAA_LORA + GATE_LORA
GN_EPS = 64e-5
MOBA_BLOCK = 256
MOBA_TOP_K = 3
KV_GROUP = 4
BIAS_ROWS = 16
K_AUG = 128
V_AUG = HEAD_DIM + 16
HEADS_PER_STEP = 2
LOG2E = 1.4426950408889634
N_EXPERTS = 32
EXPERT_TOP_K = 4
SWIGLU_LIMIT = 7.0
SWIGLU_ALPHA = 1.702
RMS_EPS = 1e-5
GATE_NEG = -1e30
MASK_NEG = -1e30

SCAN_CHUNK = 64
SCAN_CHUNKS = 4
EXPERT_TILE = 256
VMEM_LIMIT = 56 * 1024 * 1024
TOKEN_ROWS = 8
LANES = 128


def _cparams(*sem):
    return pltpu.CompilerParams(dimension_semantics=sem, vmem_limit_bytes=VMEM_LIMIT)


def _sigmoid(x):
    return 1.0 / (1.0 + jnp.exp(-x))


def _mod_kernel(c_ref, w_ref, b_ref, o_ref):
    c = c_ref[...]
    s = c * _sigmoid(c)
    o_ref[...] = jnp.dot(s, w_ref[...], precision=HI, preferred_element_type=F32) + b_ref[...]


def _mod(c, w_ada, b_ada):
    d = c.shape[-1]
    n = w_ada.shape[1]
    c8 = jnp.broadcast_to(c[:1], (8, d))
    tn = 1024
    out = pl.pallas_call(
        _mod_kernel,
        grid=(n // tn,),
        in_specs=[pl.BlockSpec((8, d), lambda j: (0, 0)),
                  pl.BlockSpec((d, tn), lambda j: (0, j)),
                  pl.BlockSpec((1, tn), lambda j: (0, j))],
        out_specs=pl.BlockSpec((8, tn), lambda j: (0, j)),
        out_shape=jax.ShapeDtypeStruct((8, n), F32),
        compiler_params=_cparams("arbitrary"),
        name="mod",
    )(c8, w_ada, b_ada.reshape(1, n))
    return out[:1]


def _inproj_kernel(x_ref, g_ref, sc_ref, sh_ref, wr_ref, wq_ref, wg_ref,
                   pr_ref, sg_ref, km_ref, qh_ref, qt_ref, ka_ref, va_ref):
    i = pl.program_id(0)
    x = x_ref[...]
    tm = x.shape[0]
    ms = jnp.mean(x * x, axis=-1, keepdims=True)
    h = x * lax.rsqrt(ms + RMS_EPS) * g_ref[...]
    h = h * (1.0 + sc_ref[...]) + sh_ref[...]
    hb = h.astype(BF16)
    pr_ref[...] = jnp.dot(hb, wr_ref[...], preferred_element_type=F32)
    sg_ref[...] = _sigmoid(jnp.dot(hb, wg_ref[...], preferred_element_type=F32))
    qkv = jnp.dot(hb, wq_ref[...], preferred_element_type=F32)
    km_ref[0] = jnp.mean(qkv[:, WIDTH:2 * WIDTH], axis=0, keepdims=True)
    col = lax.broadcasted_iota(jnp.int32, (tm, K_AUG - HEAD_DIM), 1)
    indicator = jnp.where(col == i % KV_GROUP, 1.0, 0.0).astype(BF16)
    row = lax.broadcasted_iota(jnp.int32, (V_AUG - HEAD_DIM, tm), 0)
    ones_row = jnp.where(row == 0, 1.0, 0.0).astype(BF16)
    for hd in range(N_HEADS):
        lo = hd * HEAD_DIM
        q = qkv[:, lo:lo + HEAD_DIM]
        kh = qkv[:, WIDTH + lo:WIDTH + lo + HEAD_DIM]
        vh = qkv[:, 2 * WIDTH + lo:2 * WIDTH + lo + HEAD_DIM]
        qh_ref[hd] = q
        qt_ref[hd] = (q * (HEAD_DIM ** -0.5 * LOG2E)).T.astype(BF16)
        ka_ref[hd, 0] = jnp.concatenate([kh.astype(BF16), indicator], axis=1)
        va_ref[hd, 0] = jnp.concatenate([vh.T.astype(BF16), ones_row], axis=0)


def _inproj(x2, norm_g, scale, shift, w_r, w_q, w_g):
    t, d = x2.shape
    tm = MOBA_BLOCK
    gs = MOBA_BLOCK * KV_GROUP
    ng = t // gs
    full = lambda a: pl.BlockSpec(a.shape, lambda i: (0, 0))
    row = lambda n: pl.BlockSpec((tm, n), lambda i: (i, 0))
    return pl.pallas_call(
        _inproj_kernel,
        grid=(t // tm,),
        in_specs=[row(d), full(norm_g), full(scale), full(shift), full(w_r), full(w_q), full(w_g)],
        out_specs=[row(w_r.shape[1]), row(w_g.shape[1]),
                   pl.BlockSpec((1, 1, WIDTH), lambda i: (i, 0, 0)),
                   pl.BlockSpec((N_HEADS, tm, HEAD_DIM), lambda i: (0, i, 0)),
                   pl.BlockSpec((N_HEADS, HEAD_DIM, tm), lambda i: (0, 0, i)),
                   pl.BlockSpec((N_HEADS, 1, tm, K_AUG), lambda i: (0, i // KV_GROUP, i % KV_GROUP, 0)),
                   pl.BlockSpec((N_HEADS, 1, V_AUG, tm), lambda i: (0, i // KV_GROUP, 0, i % KV_GROUP))],
        out_shape=[jax.ShapeDtypeStruct((t, w_r.shape[1]), F32),
                   jax.ShapeDtypeStruct((t, w_g.shape[1]), F32),
                   jax.ShapeDtypeStruct((t // tm, 1, WIDTH), F32),
                   jax.ShapeDtypeStruct((N_HEADS, t, HEAD_DIM), F32),
                   jax.ShapeDtypeStruct((N_HEADS, HEAD_DIM, t), BF16),
                   jax.ShapeDtypeStruct((N_HEADS, ng, gs, K_AUG), BF16),
                   jax.ShapeDtypeStruct((N_HEADS, ng, V_AUG, gs), BF16)],
        compiler_params=_cparams("arbitrary"),
        name="inproj",
    )(x2, norm_g, scale, shift, w_r, w_q, w_g)


def _rwkvprep_kernel(p_ref, mu_ref, w0_ref, w2_ref, a0_ref, a2_ref, g2_ref, kk_ref, ka_ref, bd_ref,
                     r_out, lw_out, k_out, v_out, a_out, b_out, g_out, prev_ref):
    i = pl.program_id(0)

    @pl.when(i == 0)
    def _():
        prev_ref[...] = jnp.zeros_like(prev_ref)

    p = p_ref[...]
    tm = p.shape[0]
    rolled = pltpu.roll(p, 1, 0)
    row0 = lax.broadcasted_iota(jnp.int32, p.shape, 0) == 0
    p_prev = jnp.where(row0, prev_ref[0:1, :], rolled)
    prev_ref[0:1, :] = p[tm - 1:tm, :]
    ps = p + (p_prev - p) * mu_ref[...]
    r = ps[:, 0:WIDTH]
    k = ps[:, WIDTH:2 * WIDTH]
    v = ps[:, 2 * WIDTH:3 * WIDTH]
    xwa = ps[:, 3 * WIDTH:3 * WIDTH + DECAY_LORA + AAA_LORA]
    xg = ps[:, 3 * WIDTH + DECAY_LORA + AAA_LORA:RWKV_PROJ]
    dot = lambda lhs, w_ref: jnp.dot(lhs.astype(BF16), w_ref[...], preferred_element_type=F32)
    z = w0_ref[...] + dot(jnp.tanh(xwa), w2_ref)
    nz = -z
    softplus = jnp.maximum(nz, 0.0) + jnp.log(1.0 + jnp.exp(-jnp.abs(nz)))
    w = -softplus - 0.5
    a = _sigmoid(a0_ref[...] + dot(xwa, a2_ref))
    g = dot(_sigmoid(xg), g2_ref)
    kk = k * kk_ref[...]
    sq = kk * kk
    sq_hi = sq.astype(BF16)
    ss = dot(sq_hi, bd_ref) + dot(sq - sq_hi.astype(F32), bd_ref)
    kk = kk / jnp.maximum(jnp.sqrt(ss), 1e-12)
    outs = ((r_out, r), (lw_out, -jnp.exp(w)),
            (k_out, k * (1.0 + (a - 1.0) * ka_ref[...])), (v_out, v), (a_out, -kk), (b_out, kk * a), (g_out, g))
    for ref, val in outs:
        for hd in range(N_HEADS):
            ref[hd] = val[:, hd * HEAD_DIM:(hd + 1) * HEAD_DIM]


def _rwkvprep(p_rwkv, mu, w0, w2p, a0, a2p, g2, k_k, k_a, bd):
    t = p_rwkv.shape[0]
    tm = 256
    full = lambda a: pl.BlockSpec(a.shape, lambda i: (0, 0))
    row = lambda n: pl.BlockSpec((tm, n), lambda i: (i, 0))
    outs = [jax.ShapeDtypeStruct((N_HEADS, t, HEAD_DIM), F32)] * 7
    return pl.pallas_call(
        _rwkvprep_kernel,
        grid=(t // tm,),
        in_specs=[row(RWKV_PROJ), full(mu), full(w0), full(w2p), full(a0), full(a2p), full(g2),
                  full(k_k), full(k_a), full(bd)],
        out_specs=[pl.BlockSpec((N_HEADS, tm, HEAD_DIM), lambda i: (0, i, 0))] * 7,
        out_shape=outs,
        scratch_shapes=[pltpu.VMEM((8, RWKV_PROJ), F32)],
        compiler_params=_cparams("arbitrary"),
        name="rwkvprep",
    )(p_rwkv, mu, w0, w2p, a0, a2p, g2, k_k, k_a, bd)


def _bmm(a, b):
    return jnp.einsum("hmk,hkn->hmn", a.astype(BF16), b.astype(BF16), preferred_element_type=F32)


def _bmm_nt(a, b):
    return jnp.einsum("hmk,hnk->hmn", a.astype(BF16), b.astype(BF16), preferred_element_type=F32)


def _bmm_tn(a, b):
    return jnp.einsum("hkm,hkn->hmn", a.astype(BF16), b.astype(BF16), preferred_element_type=F32)


def _rwkvscan_kernel(r_ref, lw_ref, k_ref, v_ref, a_ref, b_ref, g_ref, lnw_ref, lnb_ref, rk_ref,
                     y_ref, s_ref):
    i = pl.program_id(0)

    @pl.when(i == 0)
    def _():
        s_ref[...] = jnp.zeros_like(s_ref)

    c = SCAN_CHUNK
    nc = SCAN_CHUNKS
    nh, _, n = r_ref.shape
    nb = nh * nc
    split = lambda ref: ref[...].reshape(nb, c, n)
    r, lw, k, v, a, b = (split(ref) for ref in (r_ref, lw_ref, k_ref, v_ref, a_ref, b_ref))

    row = lax.broadcasted_iota(jnp.int32, (nb, c, c), 1)
    col = lax.broadcasted_iota(jnp.int32, (nb, c, c), 2)
    lower = row >= col
    strict = row > col
    lw_hi = lw.astype(BF16).astype(F32)
    lw_mid = (lw - lw_hi).astype(BF16).astype(F32)
    cum = _bmm(lower, lw_hi) + _bmm(lower, lw_mid) + _bmm(lower, lw - lw_hi - lw_mid)
    tot = cum[:, c - 1:c, :]
    g_in = jnp.exp(cum)
    g_ex = jnp.exp(cum - lw)
    g_inv = jnp.exp(-cum)
    g_rem = jnp.exp(tot - cum)
    a_t = a * g_ex
    r_t = r * g_in
    b_t = b * g_inv
    k_t = k * g_inv
    l_ab = jnp.where(strict, _bmm_nt(a_t, b_t), 0.0)
    l_ak = jnp.where(strict, _bmm_nt(a_t, k_t), 0.0)
    m_rb = jnp.where(lower, _bmm_nt(r_t, b_t), 0.0)
    m_rk = jnp.where(lower, _bmm_nt(r_t, k_t), 0.0)
    same16 = (row // 16) == (col // 16)
    same32 = (row // 32) == (col // 32)
    diag16 = jnp.where(same16, l_ab, 0.0)
    inv = jnp.where(row == col, 1.0, 0.0) + diag16
    lp = diag16
    for _ in range(3):
        lp = _bmm(lp, lp)
        inv = inv + _bmm(inv, lp)
    off32 = jnp.where(jnp.logical_and(same32, jnp.logical_not(same16)), l_ab, 0.0)
    inv = inv + _bmm(_bmm(inv, off32), inv)
    off64 = jnp.where(same32, 0.0, l_ab)
    inv = inv + _bmm(_bmm(inv, off64), inv)
    a_hat = _bmm(inv, a_t)
    w_hat = _bmm(inv, _bmm(l_ak, v))
    r_hat = r_t + _bmm(m_rb, a_hat)
    y0 = _bmm(m_rb, w_hat) + _bmm(m_rk, v)
    b_hat = b * g_rem
    k_hat = k * g_rem
    decay = jnp.exp(tot)

    chunk = lambda x, j: x.reshape(nh, nc, x.shape[1], n)[:, j]
    s = s_ref[...]
    ys = []
    for j in range(nc):
        u = _bmm_nt(chunk(a_hat, j), s) + chunk(w_hat, j)
        ys.append(_bmm_nt(chunk(r_hat, j), s) + chunk(y0, j))
        s = s * chunk(decay, j) + _bmm_tn(u, chunk(b_hat, j)) + _bmm_tn(chunk(v, j), chunk(k_hat, j))
    s_ref[...] = s
    y = jnp.concatenate(ys, axis=1)

    r = r_ref[...]
    mean = jnp.mean(y, axis=-1, keepdims=True)
    yc = y - mean
    var = jnp.mean(yc * yc, axis=-1, keepdims=True)
    yn = yc * lax.rsqrt(var + GN_EPS) * lnw_ref[...] + lnb_ref[...]
    bonus = jnp.sum(r * k_ref[...] * rk_ref[...], axis=-1, keepdims=True) * v_ref[...]
    out = (yn + bonus) * g_ref[...]
    for hd in range(nh):
        y_ref[:, hd * n:(hd + 1) * n] = out[hd]


def _rwkvscan(r, lw, k, v, a, b, g, ln_w, ln_b, r_k):
    nh, t, n = r.shape
    rows = SCAN_CHUNK * SCAN_CHUNKS
    blk = pl.BlockSpec((nh, rows, n), lambda i: (0, i, 0))
    par = pl.BlockSpec((nh, 1, n), lambda i: (0, 0, 0))
    return pl.pallas_call(
        _rwkvscan_kernel,
        grid=(t // rows,),
        in_specs=[blk] * 7 + [par] * 3,
        out_specs=pl.BlockSpec((rows, nh * n), lambda i: (i, 0)),
        out_shape=jax.ShapeDtypeStruct((t, nh * n), F32),
        scratch_shapes=[pltpu.VMEM((nh, n, n), F32)],
        compiler_params=_cparams("arbitrary"),
        name="rwkvscan",
    )(r, lw, k, v, a, b, g, ln_w, ln_b, r_k)


def _mobasel_kernel(q_ref, km_ref, o_ref):
    j = pl.program_id(1)
    q = q_ref[0]
    km = km_ref[0]
    nb = km.shape[0]
    tq = q.shape[0]
    gate = lax.dot_general(km, q, (((1,), (1,)), ((), ())), precision=HI, preferred_element_type=F32)
    n_id = lax.broadcasted_iota(jnp.int32, (nb, tq), 0)
    t_id = lax.broadcasted_iota(jnp.int32, (nb, tq), 1) + j * tq
    q_blk = t_id // MOBA_BLOCK
    cand = n_id < nb - 1
    past = n_id < q_blk
    neg_inf = jnp.float32(-jnp.inf)
    gate = jnp.where(cand, jnp.where(past, gate, GATE_NEG), neg_inf)
    sel = jnp.zeros((nb, tq), F32)
    for _ in range(min(MOBA_TOP_K, nb - 1)):
        m = jnp.max(gate, axis=0, keepdims=True)
        idx = jnp.min(jnp.where(gate == m, n_id, nb), axis=0, keepdims=True)
        pick = n_id == idx
        sel = jnp.where(pick, 1.0, sel)
        gate = jnp.where(pick, neg_inf, gate)
    bias = jnp.where(jnp.logical_or(jnp.logical_and(past, sel > 0.5), n_id == q_blk), 0.0, MASK_NEG)
    pad = jnp.zeros((BIAS_ROWS - KV_GROUP, tq), F32)
    for g in range(nb // KV_GROUP):
        o_ref[0, g] = jnp.concatenate([bias[g * KV_GROUP:(g + 1) * KV_GROUP], pad], axis=0).astype(BF16)


def _mobasel(q_hm, kmean):
    nh, t, n = q_hm.shape
    nb = kmean.shape[1]
    ng = nb // KV_GROUP
    tq = min(t, 2048)
    return pl.pallas_call(
        _mobasel_kernel,
        grid=(nh, t // tq),
        in_specs=[pl.BlockSpec((1, tq, n), lambda h, j: (h, j, 0)),
                  pl.BlockSpec((1, nb, n), lambda h, j: (h, 0, 0))],
        out_specs=pl.BlockSpec((1, ng, BIAS_ROWS, tq), lambda h, j: (h, 0, 0, j)),
        out_shape=jax.ShapeDtypeStruct((nh, ng, BIAS_ROWS, t), BF16),
        compiler_params=_cparams("arbitrary", "arbitrary"),
        name="mobasel",
    )(q_hm, kmean)


def _mobaattn_kernel(qt_ref, k_ref, vt_ref, bias_ref, o_ref, sa_ref, sb_ref, m_ref, acc_ref):
    i = pl.program_id(1)
    bs = MOBA_BLOCK
    gd = i // KV_GROUP
    q_pad = jnp.zeros((K_AUG - HEAD_DIM - BIAS_ROWS, bs), BF16)
    heads = range(HEADS_PER_STEP)

    def scores_to(dst, g):
        for hh in heads:
            q_aug = jnp.concatenate([qt_ref[hh], bias_ref[hh, g], q_pad], axis=0)
            dst[hh] = jnp.dot(k_ref[hh, g], q_aug, preferred_element_type=F32)

    def update_from(src, g, keep=None):
        for hh in heads:
            s = src[hh] if keep is None else jnp.where(keep, src[hh], MASK_NEG)
            m = m_ref[hh]
            m_new = jnp.maximum(m, jnp.max(s, axis=0, keepdims=True))
            acc = jnp.exp2(m - m_new) * acc_ref[hh]
            for b in range(KV_GROUP):
                p = jnp.exp2((s[b * bs:(b + 1) * bs] - m_new).astype(BF16))
                acc = acc + jnp.dot(vt_ref[hh, g, :, pl.ds(b * bs, bs)], p, preferred_element_type=F32)
            acc_ref[hh] = acc
            m_ref[hh] = m_new

    def finish(src):
        causal = lax.broadcasted_iota(jnp.int32, (bs, bs), 0) <= lax.broadcasted_iota(jnp.int32, (bs, bs), 1)
        own = jnp.concatenate([jnp.logical_or(causal, i % KV_GROUP != b) for b in range(KV_GROUP)], axis=0)
        update_from(src, gd, own)
        for hh in heads:
            acc = acc_ref[hh]
            o_ref[hh] = acc[:HEAD_DIM] / acc[HEAD_DIM:HEAD_DIM + 1]

    m_ref[...] = jnp.full(m_ref.shape, MASK_NEG, F32)
    acc_ref[...] = jnp.zeros(acc_ref.shape, F32)
    scores_to(sa_ref, 0)

    def pair(k, _):
        g = 2 * k
        scores_to(sb_ref, g + 1)
        update_from(sa_ref, g)
        scores_to(sa_ref, g + 2)
        update_from(sb_ref, g + 1)
        return 0

    lax.fori_loop(0, gd // 2, pair, 0)

    @pl.when(gd % 2 == 1)
    def _():
        scores_to(sb_ref, gd)
        update_from(sa_ref, gd - 1)
        finish(sb_ref)

    @pl.when(gd % 2 == 0)
    def _():
        finish(sa_ref)


def _mobaattn(qt, k_aug, vt_aug, bias):
    nh, n, t = qt.shape
    gs = MOBA_BLOCK * KV_GROUP
    ng = t // gs
    hp = HEADS_PER_STEP
    return pl.pallas_call(
        _mobaattn_kernel,
        grid=(nh // hp, t // MOBA_BLOCK),
        in_specs=[pl.BlockSpec((hp, n, MOBA_BLOCK), lambda h, i: (h, 0, i)),
                  pl.BlockSpec((hp, ng, gs, K_AUG), lambda h, i: (h, 0, 0, 0)),
                  pl.BlockSpec((hp, ng, V_AUG, gs), lambda h, i: (h, 0, 0, 0)),
                  pl.BlockSpec((hp, ng, BIAS_ROWS, MOBA_BLOCK), lambda h, i: (h, 0, 0, i))],
        out_specs=pl.BlockSpec((hp, n, MOBA_BLOCK), lambda h, i: (h, 0, i)),
        out_shape=jax.ShapeDtypeStruct((nh, n, t), F32),
        scratch_shapes=[pltpu.VMEM((hp, gs, MOBA_BLOCK), F32), pltpu.VMEM((hp, gs, MOBA_BLOCK), F32),
                        pltpu.VMEM((hp, 1, MOBA_BLOCK), F32), pltpu.VMEM((hp, V_AUG, MOBA_BLOCK), F32)],
        compiler_params=_cparams("arbitrary", "arbitrary"),
        name="mobaattn",
    )(qt, k_aug, vt_aug, bias)


def _load_token_tiles(ref, n_tokens, lead=(), first=0):
    chunks = [ref[lead + (pl.ds(first * TOKEN_ROWS + ch, n_tokens, stride=TOKEN_ROWS), slice(None))]
              for ch in range(TOKEN_ROWS)]
    return jnp.concatenate(chunks, axis=1)


def _store_token_tiles(ref, val, first=0):
    for ch in range(TOKEN_ROWS):
        ref[pl.ds(first * TOKEN_ROWS + ch, val.shape[0], stride=TOKEN_ROWS), :] = val[:, ch * LANES:(ch + 1) * LANES]


def _mix_kernel(ya_ref, yb_ref, sg_ref, x_ref, wa_ref, wb_ref, wo_ref, g1_ref, n2_ref, sc_ref, sh_ref,
                rw_ref, rb_ref, x1_ref, h2_ref, eid_ref, rank_ref, wt_ref, cnt_ref, base_ref):
    i = pl.program_id(0)

    @pl.when(i == 0)
    def _():
        base_ref[...] = jnp.zeros_like(base_ref)

    d = x_ref.shape[1]
    tm = x_ref.shape[0]
    pa = jnp.dot(ya_ref[...].astype(BF16), wa_ref[...], preferred_element_type=F32)
    yb = yb_ref[...].reshape(WIDTH, tm).T
    pb = jnp.dot(yb.astype(BF16), wb_ref[...], preferred_element_type=F32)
    sg = sg_ref[...]
    mixed = sg[:, :d] * pa + sg[:, d:] * pb
    mixed = jnp.dot(mixed.astype(BF16), wo_ref[...], preferred_element_type=F32)
    x1 = x_ref[...] + g1_ref[...] * mixed
    x1_ref[...] = x1
    ms = jnp.mean(x1 * x1, axis=-1, keepdims=True)
    h2 = x1 * lax.rsqrt(ms + RMS_EPS) * n2_ref[...]
    h2 = h2 * (1.0 + sc_ref[...]) + sh_ref[...]
    _store_token_tiles(h2_ref, h2)

    ne = rw_ref.shape[1] // 2
    h2_hi = h2.astype(BF16)
    h2_lo = (h2 - h2_hi.astype(F32)).astype(BF16)
    by_hi = jnp.dot(h2_hi, rw_ref[...], preferred_element_type=F32)
    by_lo = jnp.dot(h2_lo, rw_ref[:, :ne], preferred_element_type=F32)
    logits = by_hi[:, :ne] + by_hi[:, ne:] + by_lo + rb_ref[...]
    e_id = lax.broadcasted_iota(jnp.int32, (tm, ne), 1)
    neg_inf = jnp.float32(-jnp.inf)
    work = logits
    picks, vals = [], []
    for _ in range(EXPERT_TOP_K):
        m = jnp.max(work, axis=-1, keepdims=True)
        idx = jnp.min(jnp.where(work == m, e_id, ne), axis=-1, keepdims=True)
        pick = e_id == idx
        picks.append((idx, pick))
        vals.append(m)
        work = jnp.where(pick, neg_inf, work)
    den = sum(jnp.exp(v - vals[0]) for v in vals)
    chosen = jnp.zeros((tm, ne), F32)
    for _, pick in picks:
        chosen = jnp.where(pick, 1.0, chosen)
    r_id = lax.broadcasted_iota(jnp.int32, (tm, tm), 0)
    c_id = lax.broadcasted_iota(jnp.int32, (tm, tm), 1)
    before = jnp.where(r_id > c_id, 1.0, 0.0).astype(BF16)
    ahead = jnp.dot(before, chosen.astype(BF16), preferred_element_type=F32) + base_ref[0:1, :]
    k_id = lax.broadcasted_iota(jnp.int32, (tm, EXPERT_TOP_K), 1)
    eid = jnp.zeros((tm, EXPERT_TOP_K), jnp.int32)
    rank = jnp.zeros((tm, EXPERT_TOP_K), jnp.int32)
    wt = jnp.zeros((tm, EXPERT_TOP_K), F32)
    for kk, ((idx, pick), v) in enumerate(zip(picks, vals)):
        rk = jnp.sum(jnp.where(pick, ahead, 0.0), axis=-1, keepdims=True).astype(jnp.int32)
        eid = jnp.where(k_id == kk, idx, eid)
        rank = jnp.where(k_id == kk, rk, rank)
        wt = jnp.where(k_id == kk, jnp.exp(v - vals[0]) / den, wt)
    eid_ref[...] = eid
    rank_ref[...] = rank
    wt_ref[...] = wt
    total = base_ref[0:1, :] + jnp.sum(chosen, axis=0, keepdims=True)
    base_ref[0:1, :] = total
    cnt_ref[...] = jnp.broadcast_to(total, cnt_ref.shape)


def _mix(ya, yb, sg, x2, w_a, w_b, w_o, gate1, norm2_g, scale2, shift2, router_w, router_b):
    t, d = x2.shape
    tm = 512
    ne = router_w.shape[1]
    rw_hi = router_w.astype(BF16)
    router_w = jnp.concatenate([rw_hi, (router_w - rw_hi.astype(F32)).astype(BF16)], axis=1)
    full = lambda a: pl.BlockSpec(a.shape, lambda i: (0, 0))
    row = lambda n: pl.BlockSpec((tm, n), lambda i: (i, 0))
    return pl.pallas_call(
        _mix_kernel,
        grid=(t // tm,),
        in_specs=[row(WIDTH), pl.BlockSpec((N_HEADS, HEAD_DIM, tm), lambda i: (0, 0, i)), row(2 * d), row(d),
                  full(w_a), full(w_b), full(w_o),
                  full(gate1), full(norm2_g), full(scale2), full(shift2), full(router_w), full(router_b)],
        out_specs=[row(d), pl.BlockSpec((tm * TOKEN_ROWS, LANES), lambda i: (i, 0)),
                   row(EXPERT_TOP_K), row(EXPERT_TOP_K), row(EXPERT_TOP_K),
                   pl.BlockSpec((8, ne), lambda i: (0, 0))],
        out_shape=[jax.ShapeDtypeStruct((t, d), F32), jax.ShapeDtypeStruct((t * TOKEN_ROWS, LANES), F32),
                   jax.ShapeDtypeStruct((t, EXPERT_TOP_K), jnp.int32),
                   jax.ShapeDtypeStruct((t, EXPERT_TOP_K), jnp.int32),
                   jax.ShapeDtypeStruct((t, EXPERT_TOP_K), F32),
                   jax.ShapeDtypeStruct((8, ne), F32)],
        scratch_shapes=[pltpu.VMEM((8, ne), F32)],
        compiler_params=_cparams("arbitrary"),
        name="mix",
    )(ya, yb, sg, x2, w_a, w_b, w_o, gate1, norm2_g, scale2, shift2, router_w, router_b)


DISPATCH_TILE = 256


def _slots_kernel(off_ref, eid_ref, rank_ref, o_ref):
    eid = eid_ref[...]
    slot = rank_ref[...]
    for e in range(N_EXPERTS):
        slot = slot + jnp.where(eid == e, off_ref[e], 0)
    o_ref[...] = slot * TOKEN_ROWS


def _slots(off, eid, rank):
    n = eid.size
    lanes = 128
    shape = (n // lanes, lanes)
    full = pl.BlockSpec(shape, lambda i: (0, 0))
    out = pl.pallas_call(
        _slots_kernel,
        grid=(1,),
        in_specs=[pl.BlockSpec(memory_space=pltpu.SMEM), full, full],
        out_specs=full,
        out_shape=jax.ShapeDtypeStruct(shape, jnp.int32),
        compiler_params=_cparams("arbitrary"),
        name="slots",
    )(off, eid.reshape(shape), rank.reshape(shape))
    return out.reshape(n)


def _dispatch_kernel(slot_ref, last_ref, nu_ref, h_ref, xs_ref, zeros_ref, sem, zsem):
    tm = DISPATCH_TILE
    tile_rows = EXPERT_TILE * TOKEN_ROWS
    n_tiles = xs_ref.shape[0] // tile_rows

    @pl.when(pl.program_id(0) == 0)
    def _():
        zeros_ref[...] = jnp.zeros_like(zeros_ref)
        zero_tile = lambda j: pltpu.make_async_copy(
            zeros_ref, xs_ref.at[pl.ds(pl.multiple_of(j * tile_rows, tile_rows), tile_rows)], zsem)

        def each_padded_tile(fn):
            def expert(e, c):
                @pl.when(last_ref[e] >= 0)
                def _():
                    fn(last_ref[e])
                return c
            lax.fori_loop(0, N_EXPERTS, expert, 0)
            lax.fori_loop(nu_ref[0], n_tiles, lambda j, c: (fn(j), c)[1], 0)

        each_padded_tile(lambda j: zero_tile(j).start())
        each_padded_tile(lambda j: zero_tile(j).wait())

    def body(t, _):
        for kk in range(EXPERT_TOP_K):
            row = pl.multiple_of(slot_ref[t * EXPERT_TOP_K + kk], TOKEN_ROWS)
            pltpu.make_async_copy(h_ref.at[pl.ds(t * TOKEN_ROWS, TOKEN_ROWS)], xs_ref.at[pl.ds(row, TOKEN_ROWS)],
                                  sem).start(priority=kk % 2)
        return 0

    lax.fori_loop(0, tm, body, 0, unroll=2)
    for _ in range(EXPERT_TOP_K):
        pltpu.make_async_copy(h_ref, xs_ref.at[pl.ds(0, tm * TOKEN_ROWS)], sem).wait()


def _dispatch(slot_flat, last_tile, n_used, h2, n_slots):
    rows, lanes = h2.shape
    tm = DISPATCH_TILE
    t = rows // TOKEN_ROWS
    smem = pl.BlockSpec(memory_space=pltpu.SMEM)
    return pl.pallas_call(
        _dispatch_kernel,
        grid=(t // tm,),
        in_specs=[pl.BlockSpec((tm * EXPERT_TOP_K,), lambda i: (i,), memory_space=pltpu.SMEM), smem, smem,
                  pl.BlockSpec((tm * TOKEN_ROWS, lanes), lambda i: (i, 0))],
        out_specs=pl.BlockSpec(memory_space=pl.ANY),
        out_shape=jax.ShapeDtypeStruct((n_slots * TOKEN_ROWS, lanes), F32),
        scratch_shapes=[pltpu.VMEM((EXPERT_TILE * TOKEN_ROWS, lanes), F32),
                        pltpu.SemaphoreType.DMA(()), pltpu.SemaphoreType.DMA(())],
        compiler_params=pltpu.CompilerParams(dimension_semantics=("arbitrary",), has_side_effects=True),
        name="dispatch",
    )(slot_flat, last_tile, n_used, h2)


def _experts_kernel(te_ref, nu_ref, nx_ref, rp_ref, x_ref, wg_hbm, bg_ref, wu_hbm, bu_ref, wd_hbm, bd_ref, y_ref,
                    wbuf, wg_s, wu_s, wd_s, sems):
    i = pl.program_id(0)
    used = i < nu_ref[0]
    new_expert = jnp.logical_or(i == 0, te_ref[i] != te_ref[jnp.maximum(i - 1, 0)])

    def fetch(e, slot):
        return [pltpu.make_async_copy(w.at[e], wbuf.at[slot, j], sems.at[slot, j])
                for j, w in enumerate((wg_hbm, wu_hbm, wd_hbm))]

    @pl.when(jnp.logical_and(used, new_expert))
    def _():
        slot = rp_ref[i]

        @pl.when(i == 0)
        def _():
            for copy in fetch(te_ref[0], 0):
                copy.start()

        for copy in fetch(te_ref[i], slot):
            copy.wait()
        wg_s[...] = wbuf[slot, 0].astype(BF16)
        wu_s[...] = wbuf[slot, 1].astype(BF16)
        wd_s[...] = wbuf[slot, 2].astype(BF16)

        @pl.when(nx_ref[i] >= 0)
        def _():
            for copy in fetch(nx_ref[i], 1 - slot):
                copy.start()

    @pl.when(used)
    def _():
        x = _load_token_tiles(x_ref, EXPERT_TILE).astype(BF16)
        gt = jnp.minimum(jnp.dot(x, wg_s[...], preferred_element_type=F32) + bg_ref[0], SWIGLU_LIMIT)
        up = jnp.clip(jnp.dot(x, wu_s[...], preferred_element_type=F32) + bu_ref[0], -SWIGLU_LIMIT, SWIGLU_LIMIT)
        hid = (up + 1.0) * gt * _sigmoid(SWIGLU_ALPHA * gt)
        _store_token_tiles(y_ref, jnp.dot(hid.astype(BF16), wd_s[...], preferred_element_type=F32) + bd_ref[0])

    @pl.when(jnp.logical_not(used))
    def _():
        y_ref[...] = jnp.zeros_like(y_ref)


def _experts(tile_expert, n_used, next_expert, run_parity, xs, w_gate, b_gate, w_up, b_up, w_down, b_down):
    rows, lanes = xs.shape
    d = TOKEN_ROWS * lanes
    f = w_gate.shape[2]
    assert d == f
    tm = EXPERT_TILE
    n_tiles = rows // (tm * TOKEN_ROWS)
    row = pl.BlockSpec((tm * TOKEN_ROWS, lanes), lambda i, te, nu, nx, rp: (i, 0))
    bias = lambda n: pl.BlockSpec((1, 1, n), lambda i, te, nu, nx, rp: (te[i], 0, 0))
    hbm = pl.BlockSpec(memory_space=pl.ANY)
    grid_spec = pltpu.PrefetchScalarGridSpec(
        num_scalar_prefetch=4,
        grid=(n_tiles,),
        in_specs=[row, hbm, bias(f), hbm, bias(f), hbm, bias(d)],
        out_specs=row,
        scratch_shapes=[pltpu.VMEM((2, 3, d, f), F32),
                        pltpu.VMEM((d, f), BF16), pltpu.VMEM((d, f), BF16), pltpu.VMEM((f, d), BF16),
                        pltpu.SemaphoreType.DMA((2, 3))],
    )
    return pl.pallas_call(
        _experts_kernel,
        grid_spec=grid_spec,
        out_shape=jax.ShapeDtypeStruct((n_tiles * tm * TOKEN_ROWS, lanes), F32),
        compiler_params=_cparams("arbitrary"),
        name="experts",
    )(tile_expert, n_used, next_expert, run_parity, xs, w_gate, b_gate, w_up, b_up, w_down, b_down)


def _combine_kernel(slot_ref, next_ref, ys_ref, wt_ref, x1_ref, g2_ref, nf_ref, o_ref, buf, sems):
    i = pl.program_id(0)
    n = pl.num_programs(0)
    tm = DISPATCH_TILE

    def issue(slots, p):
        def body(t, _):
            for kk in range(EXPERT_TOP_K):
                row = pl.multiple_of(slots[t * EXPERT_TOP_K + kk], TOKEN_ROWS)
                pltpu.make_async_copy(ys_ref.at[pl.ds(row, TOKEN_ROWS)],
                                      buf.at[p, kk, pl.ds(t * TOKEN_ROWS, TOKEN_ROWS)],
                                      sems.at[p]).start(priority=kk % 2)
            return 0
        lax.fori_loop(0, tm, body, 0, unroll=2)

    p = i % 2

    @pl.when(i == 0)
    def _():
        issue(slot_ref, 0)

    @pl.when(i + 1 < n)
    def _():
        issue(next_ref, 1 - p)

    for kk in range(EXPERT_TOP_K):
        pltpu.make_async_copy(ys_ref.at[pl.ds(0, tm * TOKEN_ROWS)], buf.at[p, kk], sems.at[p]).wait()
    wt = wt_ref[...]
    moe = wt[:, 0:1] * _load_token_tiles(buf, tm, (p, 0))
    for kk in range(1, EXPERT_TOP_K):
        moe = moe + wt[:, kk:kk + 1] * _load_token_tiles(buf, tm, (p, kk))
    x = x1_ref[...] + g2_ref[...] * moe
    ms = jnp.mean(x * x, axis=-1, keepdims=True)
    o_ref[...] = x * lax.rsqrt(ms + RMS_EPS) * nf_ref[...]


def _combine(slot_flat, ys, wt, x1, gate2, normf_g):
    t, d = x1.shape
    tm = DISPATCH_TILE
    n = t // tm
    full = lambda a: pl.BlockSpec(a.shape, lambda i: (0, 0))
    row = lambda w: pl.BlockSpec((tm, w), lambda i: (i, 0))
    slots = lambda index: pl.BlockSpec((tm * EXPERT_TOP_K,), index, memory_space=pltpu.SMEM)
    return pl.pallas_call(
        _combine_kernel,
        grid=(n,),
        in_specs=[slots(lambda i: (i,)), slots(lambda i: (jnp.minimum(i + 1, n - 1),)),
                  pl.BlockSpec(memory_space=pl.ANY),
                  row(EXPERT_TOP_K), row(d), full(gate2), full(normf_g)],
        out_specs=row(d),
        out_shape=jax.ShapeDtypeStruct((t, d), F32),
        scratch_shapes=[pltpu.VMEM((2, EXPERT_TOP_K, tm * TOKEN_ROWS, LANES), F32), pltpu.SemaphoreType.DMA((2,))],
        compiler_params=_cparams("arbitrary"),
        name="combine",
    )(slot_flat, slot_flat, ys, wt, x1, gate2, normf_g)


def kernel(x, c, w_ada, b_ada, norm1_g, w_in, rwkv_mu, rwkv_w0, rwkv_w2, rwkv_a0, rwkv_a2, rwkv_g2, rwkv_k_k, rwkv_k_a, rwkv_r_k, rwkv_ln_w, rwkv_ln_b, w_branch_a, w_branch_b, w_out, norm2_g, router_w, router_b, exp_w_gate, exp_b_gate, exp_w_up, exp_b_up, exp_w_down, exp_b_down, normf_g):
    bsz, t, d = x.shape
    assert bsz == 1 and t % (MOBA_BLOCK * KV_GROUP) == 0
    x2 = x.reshape(t, d)
    row = lambda a: a.reshape(1, -1)

    mod = _mod(c, w_ada, b_ada)
    shift1, scale1, gate1, shift2, scale2, gate2 = [mod[:, j * d:(j + 1) * d] for j in range(6)]

    qkv_end = RWKV_PROJ + 3 * WIDTH
    w_in_b = w_in.astype(BF16)
    p_rwkv, sg, kmean, q_hm, qt, k_aug, vt_aug = _inproj(
        x2, row(norm1_g), scale1, shift1, w_in_b[:, :RWKV_PROJ], w_in_b[:, RWKV_PROJ:qkv_end], w_in_b[:, qkv_end:])

    zeros_lora = jnp.zeros((DECAY_LORA, WIDTH), F32)
    w2p = jnp.concatenate([rwkv_w2, zeros_lora], axis=0)
    a2p = jnp.concatenate([zeros_lora, rwkv_a2], axis=0)
    head_of = jnp.arange(WIDTH) // HEAD_DIM
    bd = (head_of[:, None] == head_of[None, :]).astype(BF16)
    r, lw, k, v, av, bv, g = _rwkvprep(p_rwkv, row(rwkv_mu), row(rwkv_w0), w2p.astype(BF16), row(rwkv_a0),
                                       a2p.astype(BF16), rwkv_g2.astype(BF16), row(rwkv_k_k), row(rwkv_k_a), bd)
    per_head = lambda a: a.reshape(N_HEADS, 1, HEAD_DIM)
    y_a = _rwkvscan(r, lw, k, v, av, bv, g, per_head(rwkv_ln_w), per_head(rwkv_ln_b), per_head(rwkv_r_k))

    nb = t // MOBA_BLOCK
    bias = _mobasel(q_hm, kmean.reshape(nb, N_HEADS, HEAD_DIM).transpose(1, 0, 2))
    y_b = _mobaattn(qt, k_aug, vt_aug, bias)

    x1, h2, eid, rank, wt, cnt = _mix(y_a, y_b, sg, x2, w_branch_a.astype(BF16), w_branch_b.astype(BF16),
                                      w_out.astype(BF16), gate1, row(norm2_g), scale2, shift2,
                                      router_w, row(router_b))

    counts = cnt[0].astype(jnp.int32)
    tiles_per = (counts + EXPERT_TILE - 1) // EXPERT_TILE
    tile_end = jnp.cumsum(tiles_per)
    off = (tile_end - tiles_per) * EXPERT_TILE
    n_tiles = (t * EXPERT_TOP_K + N_EXPERTS * (EXPERT_TILE - 1)) // EXPERT_TILE
    n_used = tile_end[-1:]
    tile_expert = jnp.minimum(jnp.sum(tile_end[None, :] <= jnp.arange(n_tiles)[:, None], axis=1),
                              N_EXPERTS - 1).astype(jnp.int32)
    last_used = tile_expert[jnp.maximum(n_used[0] - 1, 0)]
    tile_expert = jnp.where(jnp.arange(n_tiles) < n_used[0], tile_expert, last_used)

    slot_flat = _slots(off, eid, rank)
    last_tile = jnp.where(tiles_per > 0, tile_end - 1, -1).astype(jnp.int32)
    xs = _dispatch(slot_flat, last_tile, n_used, h2, n_tiles * EXPERT_TILE)
    f = exp_w_gate.shape[2]
    nonempty = tiles_per > 0
    run_index = jnp.cumsum(nonempty) - 1
    ids = jnp.arange(N_EXPERTS)
    later = jnp.logical_and(nonempty[None, :], ids[None, :] > ids[:, None])
    next_nonempty = jnp.min(jnp.where(later, ids[None, :], N_EXPERTS), axis=1)
    next_nonempty = jnp.where(next_nonempty < N_EXPERTS, next_nonempty, -1)
    next_expert = next_nonempty[tile_expert].astype(jnp.int32)
    run_parity = (run_index[tile_expert] % 2).astype(jnp.int32)
    ys = _experts(tile_expert, n_used, next_expert, run_parity, xs,
                  exp_w_gate, exp_b_gate.reshape(N_EXPERTS, 1, f),
                  exp_w_up, exp_b_up.reshape(N_EXPERTS, 1, f),
                  exp_w_down, exp_b_down.reshape(N_EXPERTS, 1, d))
    out = _combine(slot_flat, ys, wt, x1, gate2, row(normf_g))
    return out.reshape(bsz, t, d)
```

```python
import jax
import jax.numpy as jnp
from jax import lax
from jax.experimental import pallas as pl
from jax.experimental.pallas import tpu as pltpu

F32 = jnp.float32
BF16 = jnp.bfloat16
HI = lax.Precision.HIGHEST

HEAD_DIM = 64
N_HEADS = 8
WIDTH = N_HEADS * HEAD_DIM
DECAY_LORA = 64
AAA_LORA = 64
GATE_LORA = 128
RWKV_PROJ = 3 * WIDTH + DECAY_LORA + AAA_LORA + GATE_LORA
GN_EPS = 64e-5
MOBA_BLOCK = 256
MOBA_TOP_K = 3
KV_GROUP = 4
BIAS_ROWS = 16
K_AUG = 128
V_AUG = HEAD_DIM + 16
HEADS_PER_STEP = 2
LOG2E = 1.4426950408889634
N_EXPERTS = 32
EXPERT_TOP_K = 4
SWIGLU_LIMIT = 7.0
SWIGLU_ALPHA = 1.702
RMS_EPS = 1e-5
GATE_NEG = -1e30
MASK_NEG = -1e30

SCAN_CHUNK = 64
SCAN_CHUNKS = 4
EXPERT_TILE = 256
VMEM_LIMIT = 56 * 1024 * 1024
TOKEN_ROWS = 8
LANES = 128


def _cparams(*sem):
    return pltpu.CompilerParams(dimension_semantics=sem, vmem_limit_bytes=VMEM_LIMIT)


def _sigmoid(x):
    return 1.0 / (1.0 + jnp.exp(-x))


def _mod_kernel(c_ref, w_ref, b_ref, o_ref):
    c = c_ref[...]
    s = c * _sigmoid(c)
    o_ref[...] = jnp.dot(s, w_ref[...], precision=HI, preferred_element_type=F32) + b_ref[...]


def _mod(c, w_ada, b_ada):
    d = c.shape[-1]
    n = w_ada.shape[1]
    c8 = jnp.broadcast_to(c[:1], (8, d))
    tn = 1024
    out = pl.pallas_call(
        _mod_kernel,
        grid=(n // tn,),
        in_specs=[pl.BlockSpec((8, d), lambda j: (0, 0)),
                  pl.BlockSpec((d, tn), lambda j: (0, j)),
                  pl.BlockSpec((1, tn), lambda j: (0, j))],
        out_specs=pl.BlockSpec((8, tn), lambda j: (0, j)),
        out_shape=jax.ShapeDtypeStruct((8, n), F32),
        compiler_params=_cparams("arbitrary"),
        name="mod",
    )(c8, w_ada, b_ada.reshape(1, n))
    return out[:1]


def _inproj_kernel(x_ref, g_ref, sc_ref, sh_ref, wr_ref, wq_ref, wg_ref,
                   mu_ref, w0_ref, w2_ref, a0_ref, a2_ref, g2_ref, kk_ref, ka_ref, bd_ref,
                   sg_ref, km_ref, qh_ref, qt_ref, kaug_ref, vaug_ref,
                   r_out, lw_out, k_out, v_out, a_out, b_out, g_out, prev_ref):
    i = pl.program_id(0)
    x = x_ref[...]
    tm = x.shape[0]
    ms = jnp.mean(x * x, axis=-1, keepdims=True)
    h = x * lax.rsqrt(ms + RMS_EPS) * g_ref[...]
    h = h * (1.0 + sc_ref[...]) + sh_ref[...]
    hb = h.astype(BF16)
    _rwkv_prep(i, jnp.dot(hb, wr_ref[...], preferred_element_type=F32), prev_ref,
               mu_ref, w0_ref, w2_ref, a0_ref, a2_ref, g2_ref, kk_ref, ka_ref, bd_ref,
               (r_out, lw_out, k_out, v_out, a_out, b_out, g_out))
    sg_ref[...] = _sigmoid(jnp.dot(hb, wg_ref[...], preferred_element_type=F32))
    qkv = jnp.dot(hb, wq_ref[...], preferred_element_type=F32)
    km_ref[0] = jnp.mean(qkv[:, WIDTH:2 * WIDTH], axis=0, keepdims=True)
    col = lax.broadcasted_iota(jnp.int32, (tm, K_AUG - HEAD_DIM), 1)
    indicator = jnp.where(col == i % KV_GROUP, 1.0, 0.0).astype(BF16)
    row = lax.broadcasted_iota(jnp.int32, (V_AUG - HEAD_DIM, tm), 0)
    ones_row = jnp.where(row == 0, 1.0, 0.0).astype(BF16)
    for hd in range(N_HEADS):
        lo = hd * HEAD_DIM
        q = qkv[:, lo:lo + HEAD_DIM]
        kh = qkv[:, WIDTH + lo:WIDTH + lo + HEAD_DIM]
        vh = qkv[:, 2 * WIDTH + lo:2 * WIDTH + lo + HEAD_DIM]
        qh_ref[hd] = q
        qt_ref[hd] = (q * (HEAD_DIM ** -0.5 * LOG2E)).T.astype(BF16)
        kaug_ref[hd, 0] = jnp.concatenate([kh.astype(BF16), indicator], axis=1)
        vaug_ref[hd, 0] = jnp.concatenate([vh.T.astype(BF16), ones_row], axis=0)


def _inproj(x2, norm_g, scale, shift, w_r, w_q, w_g, rwkv_params):
    t, d = x2.shape
    tm = MOBA_BLOCK
    gs = MOBA_BLOCK * KV_GROUP
    ng = t // gs
    full = lambda a: pl.BlockSpec(a.shape, lambda i: (0, 0))
    row = lambda n: pl.BlockSpec((tm, n), lambda i: (i, 0))
    return pl.pallas_call(
        _inproj_kernel,
        grid=(t // tm,),
        in_specs=[row(d), full(norm_g), full(scale), full(shift), full(w_r), full(w_q), full(w_g)]
        + [full(a) for a in rwkv_params],
        out_specs=[row(w_g.shape[1]),
                   pl.BlockSpec((1, 1, WIDTH), lambda i: (i, 0, 0)),
                   pl.BlockSpec((N_HEADS, tm, HEAD_DIM), lambda i: (0, i, 0)),
                   pl.BlockSpec((N_HEADS, HEAD_DIM, tm), lambda i: (0, 0, i)),
                   pl.BlockSpec((N_HEADS, 1, tm, K_AUG), lambda i: (0, i // KV_GROUP, i % KV_GROUP, 0)),
                   pl.BlockSpec((N_HEADS, 1, V_AUG, tm), lambda i: (0, i // KV_GROUP, 0, i % KV_GROUP))]
        + [pl.BlockSpec((N_HEADS, tm, HEAD_DIM), lambda i: (0, i, 0))] * 7,
        out_shape=[jax.ShapeDtypeStruct((t, w_g.shape[1]), F32),
                   jax.ShapeDtypeStruct((t // tm, 1, WIDTH), F32),
                   jax.ShapeDtypeStruct((N_HEADS, t, HEAD_DIM), F32),
                   jax.ShapeDtypeStruct((N_HEADS, HEAD_DIM, t), BF16),
                   jax.ShapeDtypeStruct((N_HEADS, ng, gs, K_AUG), BF16),
                   jax.ShapeDtypeStruct((N_HEADS, ng, V_AUG, gs), BF16)]
        + [jax.ShapeDtypeStruct((N_HEADS, t, HEAD_DIM), F32)] * 7,
        scratch_shapes=[pltpu.VMEM((8, RWKV_PROJ), F32)],
        compiler_params=_cparams("arbitrary"),
        name="inproj",
    )(x2, norm_g, scale, shift, w_r, w_q, w_g, *rwkv_params)


def _rwkv_prep(i, p, prev_ref, mu_ref, w0_ref, w2_ref, a0_ref, a2_ref, g2_ref, kk_ref, ka_ref, bd_ref, out_refs):
    @pl.when(i == 0)
    def _():
        prev_ref[...] = jnp.zeros_like(prev_ref)

    tm = p.shape[0]
    rolled = pltpu.roll(p, 1, 0)
    row0 = lax.broadcasted_iota(jnp.int32, p.shape, 0) == 0
    p_prev = jnp.where(row0, prev_ref[0:1, :], rolled)
    prev_ref[0:1, :] = p[tm - 1:tm, :]
    ps = p + (p_prev - p) * mu_ref[...]
    r = ps[:, 0:WIDTH]
    k = ps[:, WIDTH:2 * WIDTH]
    v = ps[:, 2 * WIDTH:3 * WIDTH]
    xwa = ps[:, 3 * WIDTH:3 * WIDTH + DECAY_LORA + AAA_LORA]
    xg = ps[:, 3 * WIDTH + DECAY_LORA + AAA_LORA:RWKV_PROJ]
    dot = lambda lhs, w_ref: jnp.dot(lhs.astype(BF16), w_ref[...], preferred_element_type=F32)
    z = w0_ref[...] + dot(jnp.tanh(xwa), w2_ref)
    nz = -z
    softplus = jnp.maximum(nz, 0.0) + jnp.log(1.0 + jnp.exp(-jnp.abs(nz)))
    w = -softplus - 0.5
    a = _sigmoid(a0_ref[...] + dot(xwa, a2_ref))
    g = dot(_sigmoid(xg), g2_ref)
    kk = k * kk_ref[...]
    sq = kk * kk
    sq_hi = sq.astype(BF16)
    ss = dot(sq_hi, bd_ref) + dot(sq - sq_hi.astype(F32), bd_ref)
    kk = kk / jnp.maximum(jnp.sqrt(ss), 1e-12)
    vals = (r, -jnp.exp(w),
            k * (1.0 + (a - 1.0) * ka_ref[...]), v, -kk, kk * a, g)
    for ref, val in zip(out_refs, vals):
        for hd in range(N_HEADS):
            ref[hd] = val[:, hd * HEAD_DIM:(hd + 1) * HEAD_DIM]


def _bmm(a, b):
    return jnp.einsum("hmk,hkn->hmn", a.astype(BF16), b.astype(BF16), preferred_element_type=F32)


def _bmm_nt(a, b):
    return jnp.einsum("hmk,hnk->hmn", a.astype(BF16), b.astype(BF16), preferred_element_type=F32)


def _bmm_tn(a, b):
    return jnp.einsum("hkm,hkn->hmn", a.astype(BF16), b.astype(BF16), preferred_element_type=F32)


def _rwkvscan_kernel(r_ref, lw_ref, k_ref, v_ref, a_ref, b_ref, g_ref, lnw_ref, lnb_ref, rk_ref,
                     y_ref, s_ref):
    i = pl.program_id(0)

    @pl.when(i == 0)
    def _():
        s_ref[...] = jnp.zeros_like(s_ref)

    c = SCAN_CHUNK
    nc = SCAN_CHUNKS
    nh, _, n = r_ref.shape
    nb = nh * nc
    split = lambda ref: ref[...].reshape(nb, c, n)
    r, lw, k, v, a, b = (split(ref) for ref in (r_ref, lw_ref, k_ref, v_ref, a_ref, b_ref))

    row = lax.broadcasted_iota(jnp.int32, (nb, c, c), 1)
    col = lax.broadcasted_iota(jnp.int32, (nb, c, c), 2)
    lower = row >= col
    strict = row > col
    lw_hi = lw.astype(BF16).astype(F32)
    lw_mid = (lw - lw_hi).astype(BF16).astype(F32)
    cum = _bmm(lower, lw_hi) + _bmm(lower, lw_mid) + _bmm(lower, lw - lw_hi - lw_mid)
    tot = cum[:, c - 1:c, :]
    g_in = jnp.exp(cum)
    g_ex = jnp.exp(cum - lw)
    g_inv = jnp.exp(-cum)
    g_rem = jnp.exp(tot - cum)
    a_t = a * g_ex
    r_t = r * g_in
    b_t = b * g_inv
    k_t = k * g_inv
    l_ab = jnp.where(strict, _bmm_nt(a_t, b_t), 0.0)
    l_ak = jnp.where(strict, _bmm_nt(a_t, k_t), 0.0)
    m_rb = jnp.where(lower, _bmm_nt(r_t, b_t), 0.0)
    m_rk = jnp.where(lower, _bmm_nt(r_t, k_t), 0.0)
    same16 = (row // 16) == (col // 16)
    same32 = (row // 32) == (col // 32)
    diag16 = jnp.where(same16, l_ab, 0.0)
    inv = jnp.where(row == col, 1.0, 0.0) + diag16
    lp = diag16
    for _ in range(3):
        lp = _bmm(lp, lp)
        inv = inv + _bmm(inv, lp)
    off32 = jnp.where(jnp.logical_and(same32, jnp.logical_not(same16)), l_ab, 0.0)
    inv = inv + _bmm(_bmm(inv, off32), inv)
    off64 = jnp.where(same32, 0.0, l_ab)
    inv = inv + _bmm(_bmm(inv, off64), inv)
    a_hat = _bmm(inv, a_t)
    w_hat = _bmm(inv, _bmm(l_ak, v))
    r_hat = r_t + _bmm(m_rb, a_hat)
    y0 = _bmm(m_rb, w_hat) + _bmm(m_rk, v)
    b_hat = b * g_rem
    k_hat = k * g_rem
    decay = jnp.exp(tot)

    chunk = lambda x, j: x.reshape(nh, nc, x.shape[1], n)[:, j]
    s = s_ref[...]
    ys = []
    for j in range(nc):
        u = _bmm_nt(chunk(a_hat, j), s) + chunk(w_hat, j)
        ys.append(_bmm_nt(chunk(r_hat, j), s) + chunk(y0, j))
        s = s * chunk(decay, j) + _bmm_tn(u, chunk(b_hat, j)) + _bmm_tn(chunk(v, j), chunk(k_hat, j))
    s_ref[...] = s
    y = jnp.concatenate(ys, axis=1)

    r = r_ref[...]
    mean = jnp.mean(y, axis=-1, keepdims=True)
    yc = y - mean
    var = jnp.mean(yc * yc, axis=-1, keepdims=True)
    yn = yc * lax.rsqrt(var + GN_EPS) * lnw_ref[...] + lnb_ref[...]
    bonus = jnp.sum(r * k_ref[...] * rk_ref[...], axis=-1, keepdims=True) * v_ref[...]
    out = (yn + bonus) * g_ref[...]
    for hd in range(nh):
        y_ref[:, hd * n:(hd + 1) * n] = out[hd]


def _rwkvscan(r, lw, k, v, a, b, g, ln_w, ln_b, r_k):
    nh, t, n = r.shape
    rows = SCAN_CHUNK * SCAN_CHUNKS
    blk = pl.BlockSpec((nh, rows, n), lambda i: (0, i, 0))
    par = pl.BlockSpec((nh, 1, n), lambda i: (0, 0, 0))
    return pl.pallas_call(
        _rwkvscan_kernel,
        grid=(t // rows,),
        in_specs=[blk] * 7 + [par] * 3,
        out_specs=pl.BlockSpec((rows, nh * n), lambda i: (i, 0)),
        out_shape=jax.ShapeDtypeStruct((t, nh * n), F32),
        scratch_shapes=[pltpu.VMEM((nh, n, n), F32)],
        compiler_params=_cparams("arbitrary"),
        name="rwkvscan",
    )(r, lw, k, v, a, b, g, ln_w, ln_b, r_k)


def _mobasel_kernel(q_ref, km_ref, o_ref):
    j = pl.program_id(1)
    q = q_ref[0]
    km = km_ref[0]
    nb = km.shape[0]
    tq = q.shape[0]
    gate = lax.dot_general(km, q, (((1,), (1,)), ((), ())), precision=HI, preferred_element_type=F32)
    n_id = lax.broadcasted_iota(jnp.int32, (nb, tq), 0)
    t_id = lax.broadcasted_iota(jnp.int32, (nb, tq), 1) + j * tq
    q_blk = t_id // MOBA_BLOCK
    cand = n_id < nb - 1
    past = n_id < q_blk
    neg_inf = jnp.float32(-jnp.inf)
    gate = jnp.where(cand, jnp.where(past, gate, GATE_NEG), neg_inf)
    sel = jnp.zeros((nb, tq), F32)
    for _ in range(min(MOBA_TOP_K, nb - 1)):
        m = jnp.max(gate, axis=0, keepdims=True)
        idx = jnp.min(jnp.where(gate == m, n_id, nb), axis=0, keepdims=True)
        pick = n_id == idx
        sel = jnp.where(pick, 1.0, sel)
        gate = jnp.where(pick, neg_inf, gate)
    bias = jnp.where(jnp.logical_or(jnp.logical_and(past, sel > 0.5), n_id == q_blk), 0.0, MASK_NEG)
    pad = jnp.zeros((BIAS_ROWS - KV_GROUP, tq), F32)
    for g in range(nb // KV_GROUP):
        o_ref[0, g] = jnp.concatenate([bias[g * KV_GROUP:(g + 1) * KV_GROUP], pad], axis=0).astype(BF16)


def _mobasel(q_hm, kmean):
    nh, t, n = q_hm.shape
    nb = kmean.shape[1]
    ng = nb // KV_GROUP
    tq = min(t, 2048)
    return pl.pallas_call(
        _mobasel_kernel,
        grid=(nh, t // tq),
        in_specs=[pl.BlockSpec((1, tq, n), lambda h, j: (h, j, 0)),
                  pl.BlockSpec((1, nb, n), lambda h, j: (h, 0, 0))],
        out_specs=pl.BlockSpec((1, ng, BIAS_ROWS, tq), lambda h, j: (h, 0, 0, j)),
        out_shape=jax.ShapeDtypeStruct((nh, ng, BIAS_ROWS, t), BF16),
        compiler_params=_cparams("arbitrary", "arbitrary"),
        name="mobasel",
    )(q_hm, kmean)


def _mobaattn_kernel(qt_ref, k_ref, vt_ref, bias_ref, o_ref, sa_ref, sb_ref, m_ref, acc_ref):
    i = pl.program_id(1)
    bs = MOBA_BLOCK
    gd = i // KV_GROUP
    q_pad = jnp.zeros((K_AUG - HEAD_DIM - BIAS_ROWS, bs), BF16)
    heads = range(HEADS_PER_STEP)

    def scores_to(dst, g):
        for hh in heads:
            q_aug = jnp.concatenate([qt_ref[hh], bias_ref[hh, g], q_pad], axis=0)
            dst[hh] = jnp.dot(k_ref[hh, g], q_aug, preferred_element_type=F32)

    def update_from(src, g, keep=None):
        for hh in heads:
            s = src[hh] if keep is None else jnp.where(keep, src[hh], MASK_NEG)
            m = m_ref[hh]
            m_new = jnp.maximum(m, jnp.max(s, axis=0, keepdims=True))
            acc = jnp.exp2(m - m_new) * acc_ref[hh]
            for b in range(KV_GROUP):
                p = jnp.exp2((s[b * bs:(b + 1) * bs] - m_new).astype(BF16))
                acc = acc + jnp.dot(vt_ref[hh, g, :, pl.ds(b * bs, bs)], p, preferred_element_type=F32)
            acc_ref[hh] = acc
            m_ref[hh] = m_new

    def finish(src):
        causal = lax.broadcasted_iota(jnp.int32, (bs, bs), 0) <= lax.broadcasted_iota(jnp.int32, (bs, bs), 1)
        own = jnp.concatenate([jnp.logical_or(causal, i % KV_GROUP != b) for b in range(KV_GROUP)], axis=0)
        update_from(src, gd, own)
        for hh in heads:
            acc = acc_ref[hh]
            o_ref[hh] = acc[:HEAD_DIM] / acc[HEAD_DIM:HEAD_DIM + 1]

    m_ref[...] = jnp.full(m_ref.shape, MASK_NEG, F32)
    acc_ref[...] = jnp.zeros(acc_ref.shape, F32)
    scores_to(sa_ref, 0)

    def pair(k, _):
        g = 2 * k
        scores_to(sb_ref, g + 1)
        update_from(sa_ref, g)
        scores_to(sa_ref, g + 2)
        update_from(sb_ref, g + 1)
        return 0

    lax.fori_loop(0, gd // 2, pair, 0)

    @pl.when(gd % 2 == 1)
    def _():
        scores_to(sb_ref, gd)
        update_from(sa_ref, gd - 1)
        finish(sb_ref)

    @pl.when(gd % 2 == 0)
    def _():
        finish(sa_ref)


def _mobaattn(qt, k_aug, vt_aug, bias):
    nh, n, t = qt.shape
    gs = MOBA_BLOCK * KV_GROUP
    ng = t // gs
    hp = HEADS_PER_STEP
    return pl.pallas_call(
        _mobaattn_kernel,
        grid=(nh // hp, t // MOBA_BLOCK),
        in_specs=[pl.BlockSpec((hp, n, MOBA_BLOCK), lambda h, i: (h, 0, i)),
                  pl.BlockSpec((hp, ng, gs, K_AUG), lambda h, i: (h, 0, 0, 0)),
                  pl.BlockSpec((hp, ng, V_AUG, gs), lambda h, i: (h, 0, 0, 0)),
                  pl.BlockSpec((hp, ng, BIAS_ROWS, MOBA_BLOCK), lambda h, i: (h, 0, 0, i))],
        out_specs=pl.BlockSpec((hp, n, MOBA_BLOCK), lambda h, i: (h, 0, i)),
        out_shape=jax.ShapeDtypeStruct((nh, n, t), F32),
        scratch_shapes=[pltpu.VMEM((hp, gs, MOBA_BLOCK), F32), pltpu.VMEM((hp, gs, MOBA_BLOCK), F32),
                        pltpu.VMEM((hp, 1, MOBA_BLOCK), F32), pltpu.VMEM((hp, V_AUG, MOBA_BLOCK), F32)],
        compiler_params=_cparams("arbitrary", "arbitrary"),
        name="mobaattn",
    )(qt, k_aug, vt_aug, bias)


def _load_token_tiles(ref, n_tokens, lead=(), first=0):
    chunks = [ref[lead + (pl.ds(first * TOKEN_ROWS + ch, n_tokens, stride=TOKEN_ROWS), slice(None))]
              for ch in range(TOKEN_ROWS)]
    return jnp.concatenate(chunks, axis=1)


def _store_token_tiles(ref, val, first=0):
    for ch in range(TOKEN_ROWS):
        ref[pl.ds(first * TOKEN_ROWS + ch, val.shape[0], stride=TOKEN_ROWS), :] = val[:, ch * LANES:(ch + 1) * LANES]


def _mix_kernel(ya_ref, yb_ref, sg_ref, x_ref, wa_ref, wb_ref, wo_ref, g1_ref, n2_ref, sc_ref, sh_ref,
                rw_ref, rb_ref, x1_ref, h2_ref, eid_ref, rank_ref, wt_ref, cnt_ref, base_ref):
    i = pl.program_id(0)

    @pl.when(i == 0)
    def _():
        base_ref[...] = jnp.zeros_like(base_ref)

    d = x_ref.shape[1]
    tm = x_ref.shape[0]
    pa = jnp.dot(ya_ref[...].astype(BF16), wa_ref[...], preferred_element_type=F32)
    yb = yb_ref[...].reshape(WIDTH, tm).T
    pb = jnp.dot(yb.astype(BF16), wb_ref[...], preferred_element_type=F32)
    sg = sg_ref[...]
    mixed = sg[:, :d] * pa + sg[:, d:] * pb
    mixed = jnp.dot(mixed.astype(BF16), wo_ref[...], preferred_element_type=F32)
    x1 = x_ref[...] + g1_ref[...] * mixed
    x1_ref[...] = x1
    ms = jnp.mean(x1 * x1, axis=-1, keepdims=True)
    h2 = x1 * lax.rsqrt(ms + RMS_EPS) * n2_ref[...]
    h2 = h2 * (1.0 + sc_ref[...]) + sh_ref[...]
    _store_token_tiles(h2_ref, h2)

    ne = rw_ref.shape[1] // 2
    h2_hi = h2.astype(BF16)
    h2_lo = (h2 - h2_hi.astype(F32)).astype(BF16)
    by_hi = jnp.dot(h2_hi, rw_ref[...], preferred_element_type=F32)
    by_lo = jnp.dot(h2_lo, rw_ref[:, :ne], preferred_element_type=F32)
    logits = by_hi[:, :ne] + by_hi[:, ne:] + by_lo + rb_ref[...]
    e_id = lax.broadcasted_iota(jnp.int32, (tm, ne), 1)
    neg_inf = jnp.float32(-jnp.inf)
    work = logits
    picks, vals = [], []
    for _ in range(EXPERT_TOP_K):
        m = jnp.max(work, axis=-1, keepdims=True)
        idx = jnp.min(jnp.where(work == m, e_id, ne), axis=-1, keepdims=True)
        pick = e_id == idx
        picks.append((idx, pick))
        vals.append(m)
        work = jnp.where(pick, neg_inf, work)
    den = sum(jnp.exp(v - vals[0]) for v in vals)
    chosen = jnp.zeros((tm, ne), F32)
    for _, pick in picks:
        chosen = jnp.where(pick, 1.0, chosen)
    r_id = lax.broadcasted_iota(jnp.int32, (tm, tm), 0)
    c_id = lax.broadcasted_iota(jnp.int32, (tm, tm), 1)
    before = jnp.where(r_id > c_id, 1.0, 0.0).astype(BF16)
    ahead = jnp.dot(before, chosen.astype(BF16), preferred_element_type=F32) + base_ref[0:1, :]
    k_id = lax.broadcasted_iota(jnp.int32, (tm, EXPERT_TOP_K), 1)
    eid = jnp.zeros((tm, EXPERT_TOP_K), jnp.int32)
    rank = jnp.zeros((tm, EXPERT_TOP_K), jnp.int32)
    wt = jnp.zeros((tm, EXPERT_TOP_K), F32)
    for kk, ((idx, pick), v) in enumerate(zip(picks, vals)):
        rk = jnp.sum(jnp.where(pick, ahead, 0.0), axis=-1, keepdims=True).astype(jnp.int32)
        eid = jnp.where(k_id == kk, idx, eid)
        rank = jnp.where(k_id == kk, rk, rank)
        wt = jnp.where(k_id == kk, jnp.exp(v - vals[0]) / den, wt)
    eid_ref[...] = eid
    rank_ref[...] = rank
    wt_ref[...] = wt
    total = base_ref[0:1, :] + jnp.sum(chosen, axis=0, keepdims=True)
    base_ref[0:1, :] = total
    cnt_ref[...] = jnp.broadcast_to(total, cnt_ref.shape)


def _mix(ya, yb, sg, x2, w_a, w_b, w_o, gate1, norm2_g, scale2, shift2, router_w, router_b):
    t, d = x2.shape
    tm = 512
    ne = router_w.shape[1]
    rw_hi = router_w.astype(BF16)
    router_w = jnp.concatenate([rw_hi, (router_w - rw_hi.astype(F32)).astype(BF16)], axis=1)
    full = lambda a: pl.BlockSpec(a.shape, lambda i: (0, 0))
    row = lambda n: pl.BlockSpec((tm, n), lambda i: (i, 0))
    return pl.pallas_call(
        _mix_kernel,
        grid=(t // tm,),
        in_specs=[row(WIDTH), pl.BlockSpec((N_HEADS, HEAD_DIM, tm), lambda i: (0, 0, i)), row(2 * d), row(d),
                  full(w_a), full(w_b), full(w_o),
                  full(gate1), full(norm2_g), full(scale2), full(shift2), full(router_w), full(router_b)],
        out_specs=[row(d), pl.BlockSpec((tm * TOKEN_ROWS, LANES), lambda i: (i, 0)),
                   row(EXPERT_TOP_K), row(EXPERT_TOP_K), row(EXPERT_TOP_K),
                   pl.BlockSpec((8, ne), lambda i: (0, 0))],
        out_shape=[jax.ShapeDtypeStruct((t, d), F32), jax.ShapeDtypeStruct((t * TOKEN_ROWS, LANES), F32),
                   jax.ShapeDtypeStruct((t, EXPERT_TOP_K), jnp.int32),
                   jax.ShapeDtypeStruct((t, EXPERT_TOP_K), jnp.int32),
                   jax.ShapeDtypeStruct((t, EXPERT_TOP_K), F32),
                   jax.ShapeDtypeStruct((8, ne), F32)],
        scratch_shapes=[pltpu.VMEM((8, ne), F32)],
        compiler_params=_cparams("arbitrary"),
        name="mix",
    )(ya, yb, sg, x2, w_a, w_b, w_o, gate1, norm2_g, scale2, shift2, router_w, router_b)


DISPATCH_TILE = 256


def _slots_kernel(off_ref, eid_ref, rank_ref, o_ref):
    eid = eid_ref[...]
    slot = rank_ref[...]
    for e in range(N_EXPERTS):
        slot = slot + jnp.where(eid == e, off_ref[e], 0)
    o_ref[...] = slot * TOKEN_ROWS


def _slots(off, eid, rank):
    n = eid.size
    lanes = 128
    shape = (n // lanes, lanes)
    full = pl.BlockSpec(shape, lambda i: (0, 0))
    out = pl.pallas_call(
        _slots_kernel,
        grid=(1,),
        in_specs=[pl.BlockSpec(memory_space=pltpu.SMEM), full, full],
        out_specs=full,
        out_shape=jax.ShapeDtypeStruct(shape, jnp.int32),
        compiler_params=_cparams("arbitrary"),
        name="slots",
    )(off, eid.reshape(shape), rank.reshape(shape))
    return out.reshape(n)


def _dispatch_kernel(slot_ref, last_ref, nu_ref, h_ref, xs_ref, zeros_ref, sem, zsem):
    tm = DISPATCH_TILE
    tile_rows = EXPERT_TILE * TOKEN_ROWS
    n_tiles = xs_ref.shape[0] // tile_rows

    @pl.when(pl.program_id(0) == 0)
    def _():
        zeros_ref[...] = jnp.zeros_like(zeros_ref)
        zero_tile = lambda j: pltpu.make_async_copy(
            zeros_ref, xs_ref.at[pl.ds(pl.multiple_of(j * tile_rows, tile_rows), tile_rows)], zsem)

        def each_padded_tile(fn):
            def expert(e, c):
                @pl.when(last_ref[e] >= 0)
                def _():
                    fn(last_ref[e])
                return c
            lax.fori_loop(0, N_EXPERTS, expert, 0)
            lax.fori_loop(nu_ref[0], n_tiles, lambda j, c: (fn(j), c)[1], 0)

        each_padded_tile(lambda j: zero_tile(j).start())
        each_padded_tile(lambda j: zero_tile(j).wait())

    def body(t, _):
        for kk in range(EXPERT_TOP_K):
            row = pl.multiple_of(slot_ref[t * EXPERT_TOP_K + kk], TOKEN_ROWS)
            pltpu.make_async_copy(h_ref.at[pl.ds(t * TOKEN_ROWS, TOKEN_ROWS)], xs_ref.at[pl.ds(row, TOKEN_ROWS)],
                                  sem).start(priority=kk % 2)
        return 0

    lax.fori_loop(0, tm, body, 0, unroll=2)
    for _ in range(EXPERT_TOP_K):
        pltpu.make_async_copy(h_ref, xs_ref.at[pl.ds(0, tm * TOKEN_ROWS)], sem).wait()


def _dispatch(slot_flat, last_tile, n_used, h2, n_slots):
    rows, lanes = h2.shape
    tm = DISPATCH_TILE
    t = rows // TOKEN_ROWS
    smem = pl.BlockSpec(memory_space=pltpu.SMEM)
    return pl.pallas_call(
        _dispatch_kernel,
        grid=(t // tm,),
        in_specs=[pl.BlockSpec((tm * EXPERT_TOP_K,), lambda i: (i,), memory_space=pltpu.SMEM), smem, smem,
                  pl.BlockSpec((tm * TOKEN_ROWS, lanes), lambda i: (i, 0))],
        out_specs=pl.BlockSpec(memory_space=pl.ANY),
        out_shape=jax.ShapeDtypeStruct((n_slots * TOKEN_ROWS, lanes), F32),
        scratch_shapes=[pltpu.VMEM((EXPERT_TILE * TOKEN_ROWS, lanes), F32),
                        pltpu.SemaphoreType.DMA(()), pltpu.SemaphoreType.DMA(())],
        compiler_params=pltpu.CompilerParams(dimension_semantics=("arbitrary",), has_side_effects=True),
        name="dispatch",
    )(slot_flat, last_tile, n_used, h2)


def _experts_kernel(te_ref, nu_ref, nx_ref, rp_ref, x_ref, wg_hbm, bg_ref, wu_hbm, bu_ref, wd_hbm, bd_ref, y_ref,
                    wbuf, wg_s, wu_s, wd_s, sems):
    i = pl.program_id(0)
    used = i < nu_ref[0]
    new_expert = jnp.logical_or(i == 0, te_ref[i] != te_ref[jnp.maximum(i - 1, 0)])

    def fetch(e, slot):
        return [pltpu.make_async_copy(w.at[e], wbuf.at[slot, j], sems.at[slot, j])
                for j, w in enumerate((wg_hbm, wu_hbm, wd_hbm))]

    @pl.when(jnp.logical_and(used, new_expert))
    def _():
        slot = rp_ref[i]

        @pl.when(i == 0)
        def _():
            for copy in fetch(te_ref[0], 0):
                copy.start()

        for copy in fetch(te_ref[i], slot):
            copy.wait()
        wg_s[...] = wbuf[slot, 0].astype(BF16)
        wu_s[...] = wbuf[slot, 1].astype(BF16)
        wd_s[...] = wbuf[slot, 2].astype(BF16)

        @pl.when(nx_ref[i] >= 0)
        def _():
            for copy in fetch(nx_ref[i], 1 - slot):
                copy.start()

    @pl.when(used)
    def _():
        x = _load_token_tiles(x_ref, EXPERT_TILE).astype(BF16)
        gt = jnp.minimum(jnp.dot(x, wg_s[...], preferred_element_type=F32) + bg_ref[0], SWIGLU_LIMIT)
        up = jnp.clip(jnp.dot(x, wu_s[...], preferred_element_type=F32) + bu_ref[0], -SWIGLU_LIMIT, SWIGLU_LIMIT)
        hid = (up + 1.0) * gt * _sigmoid(SWIGLU_ALPHA * gt)
        _store_token_tiles(y_ref, jnp.dot(hid.astype(BF16), wd_s[...], preferred_element_type=F32) + bd_ref[0])

    @pl.when(jnp.logical_not(used))
    def _():
        y_ref[...] = jnp.zeros_like(y_ref)


def _experts(tile_expert, n_used, next_expert, run_parity, xs, w_gate, b_gate, w_up, b_up, w_down, b_down):
    rows, lanes = xs.shape
    d = TOKEN_ROWS * lanes
    f = w_gate.shape[2]
    assert d == f
    tm = EXPERT_TILE
    n_tiles = rows // (tm * TOKEN_ROWS)
    row = pl.BlockSpec((tm * TOKEN_ROWS, lanes), lambda i, te, nu, nx, rp: (i, 0))
    bias = lambda n: pl.BlockSpec((1, 1, n), lambda i, te, nu, nx, rp: (te[i], 0, 0))
    hbm = pl.BlockSpec(memory_space=pl.ANY)
    grid_spec = pltpu.PrefetchScalarGridSpec(
        num_scalar_prefetch=4,
        grid=(n_tiles,),
        in_specs=[row, hbm, bias(f), hbm, bias(f), hbm, bias(d)],
        out_specs=row,
        scratch_shapes=[pltpu.VMEM((2, 3, d, f), F32),
                        pltpu.VMEM((d, f), BF16), pltpu.VMEM((d, f), BF16), pltpu.VMEM((f, d), BF16),
                        pltpu.SemaphoreType.DMA((2, 3))],
    )
    return pl.pallas_call(
        _experts_kernel,
        grid_spec=grid_spec,
        out_shape=jax.ShapeDtypeStruct((n_tiles * tm * TOKEN_ROWS, lanes), F32),
        compiler_params=_cparams("arbitrary"),
        name="experts",
    )(tile_expert, n_used, next_expert, run_parity, xs, w_gate, b_gate, w_up, b_up, w_down, b_down)


def _combine_kernel(slot_ref, next_ref, ys_ref, wt_ref, x1_ref, g2_ref, nf_ref, o_ref, buf, sems):
    i = pl.program_id(0)
    n = pl.num_programs(0)
    tm = DISPATCH_TILE

    def issue(slots, p):
        def body(t, _):
            for kk in range(EXPERT_TOP_K):
                row = pl.multiple_of(slots[t * EXPERT_TOP_K + kk], TOKEN_ROWS)
                pltpu.make_async_copy(ys_ref.at[pl.ds(row, TOKEN_ROWS)],
                                      buf.at[p, kk, pl.ds(t * TOKEN_ROWS, TOKEN_ROWS)],
                                      sems.at[p]).start(priority=kk % 2)
            return 0
        lax.fori_loop(0, tm, body, 0, unroll=2)

    p = i % 2

    @pl.when(i == 0)
    def _():
        issue(slot_ref, 0)

    @pl.when(i + 1 < n)
    def _():
        issue(next_ref, 1 - p)

    for kk in range(EXPERT_TOP_K):
        pltpu.make_async_copy(ys_ref.at[pl.ds(0, tm * TOKEN_ROWS)], buf.at[p, kk], sems.at[p]).wait()
    wt = wt_ref[...]
    moe = wt[:, 0:1] * _load_token_tiles(buf, tm, (p, 0))
    for kk in range(1, EXPERT_TOP_K):
        moe = moe + wt[:, kk:kk + 1] * _load_token_tiles(buf, tm, (p, kk))
    x = x1_ref[...] + g2_ref[...] * moe
    ms = jnp.mean(x * x, axis=-1, keepdims=True)
    o_ref[...] = x * lax.rsqrt(ms + RMS_EPS) * nf_ref[...]


def _combine(slot_flat, ys, wt, x1, gate2, normf_g):
    t, d = x1.shape
    tm = DISPATCH_TILE
    n = t // tm
    full = lambda a: pl.BlockSpec(a.shape, lambda i: (0, 0))
    row = lambda w: pl.BlockSpec((tm, w), lambda i: (i, 0))
    slots = lambda index: pl.BlockSpec((tm * EXPERT_TOP_K,), index, memory_space=pltpu.SMEM)
    return pl.pallas_call(
        _combine_kernel,
        grid=(n,),
        in_specs=[slots(lambda i: (i,)), slots(lambda i: (jnp.minimum(i + 1, n - 1),)),
                  pl.BlockSpec(memory_space=pl.ANY),
                  row(EXPERT_TOP_K), row(d), full(gate2), full(normf_g)],
        out_specs=row(d),
        out_shape=jax.ShapeDtypeStruct((t, d), F32),
        scratch_shapes=[pltpu.VMEM((2, EXPERT_TOP_K, tm * TOKEN_ROWS, LANES), F32), pltpu.SemaphoreType.DMA((2,))],
        compiler_params=_cparams("arbitrary"),
        name="combine",
    )(slot_flat, slot_flat, ys, wt, x1, gate2, normf_g)


def kernel(x, c, w_ada, b_ada, norm1_g, w_in, rwkv_mu, rwkv_w0, rwkv_w2, rwkv_a0, rwkv_a2, rwkv_g2, rwkv_k_k, rwkv_k_a, rwkv_r_k, rwkv_ln_w, rwkv_ln_b, w_branch_a, w_branch_b, w_out, norm2_g, router_w, router_b, exp_w_gate, exp_b_gate, exp_w_up, exp_b_up, exp_w_down, exp_b_down, normf_g):
    bsz, t, d = x.shape
    assert bsz == 1 and t % (MOBA_BLOCK * KV_GROUP) == 0
    x2 = x.reshape(t, d)
    row = lambda a: a.reshape(1, -1)

    mod = _mod(c, w_ada, b_ada)
    shift1, scale1, gate1, shift2, scale2, gate2 = [mod[:, j * d:(j + 1) * d] for j in range(6)]

    qkv_end = RWKV_PROJ + 3 * WIDTH
    w_in_b = w_in.astype(BF16)
    zeros_lora = jnp.zeros((DECAY_LORA, WIDTH), F32)
    w2p = jnp.concatenate([rwkv_w2, zeros_lora], axis=0)
    a2p = jnp.concatenate([zeros_lora, rwkv_a2], axis=0)
    head_of = jnp.arange(WIDTH) // HEAD_DIM
    bd = (head_of[:, None] == head_of[None, :]).astype(BF16)
    rwkv_params = (row(rwkv_mu), row(rwkv_w0), w2p.astype(BF16), row(rwkv_a0), a2p.astype(BF16),
                   rwkv_g2.astype(BF16), row(rwkv_k_k), row(rwkv_k_a), bd)
    sg, kmean, q_hm, qt, k_aug, vt_aug, r, lw, k, v, av, bv, g = _inproj(
        x2, row(norm1_g), scale1, shift1, w_in_b[:, :RWKV_PROJ], w_in_b[:, RWKV_PROJ:qkv_end], w_in_b[:, qkv_end:],
        rwkv_params)

    per_head = lambda a: a.reshape(N_HEADS, 1, HEAD_DIM)
    y_a = _rwkvscan(r, lw, k, v, av, bv, g, per_head(rwkv_ln_w), per_head(rwkv_ln_b), per_head(rwkv_r_k))

    nb = t // MOBA_BLOCK
    bias = _mobasel(q_hm, kmean.reshape(nb, N_HEADS, HEAD_DIM).transpose(1, 0, 2))
    y_b = _mobaattn(qt, k_aug, vt_aug, bias)

    x1, h2, eid, rank, wt, cnt = _mix(y_a, y_b, sg, x2, w_branch_a.astype(BF16), w_branch_b.astype(BF16),
                                      w_out.astype(BF16), gate1, row(norm2_g), scale2, shift2,
                                      router_w, row(router_b))

    counts = cnt[0].astype(jnp.int32)
    tiles_per = (counts + EXPERT_TILE - 1) // EXPERT_TILE
    tile_end = jnp.cumsum(tiles_per)
    off = (tile_end - tiles_per) * EXPERT_TILE
    n_tiles = (t * EXPERT_TOP_K + N_EXPERTS * (EXPERT_TILE - 1)) // EXPERT_TILE
    n_used = tile_end[-1:]
    tile_expert = jnp.minimum(jnp.sum(tile_end[None, :] <= jnp.arange(n_tiles)[:, None], axis=1),
                              N_EXPERTS - 1).astype(jnp.int32)
    last_used = tile_expert[jnp.maximum(n_used[0] - 1, 0)]
    tile_expert = jnp.where(jnp.arange(n_tiles) < n_used[0], tile_expert, last_used)

    slot_flat = _slots(off, eid, rank)
    last_tile = jnp.where(tiles_per > 0, tile_end - 1, -1).astype(jnp.int32)
    xs = _dispatch(slot_flat, last_tile, n_used, h2, n_tiles * EXPERT_TILE)
    f = exp_w_gate.shape[2]
    nonempty = tiles_per > 0
    run_index = jnp.cumsum(nonempty) - 1
    ids = jnp.arange(N_EXPERTS)
    later = jnp.logical_and(nonempty[None, :], ids[None, :] > ids[:, None])
    next_nonempty = jnp.min(jnp.where(later, ids[None, :], N_EXPERTS), axis=1)
    next_nonempty = jnp.where(next_nonempty < N_EXPERTS, next_nonempty, -1)
    next_expert = next_nonempty[tile_expert].astype(jnp.int32)
    run_parity = (run_index[tile_expert] % 2).astype(jnp.int32)
    ys = _experts(tile_expert, n_used, next_expert, run_parity, xs,
                  exp_w_gate, exp_b_gate.reshape(N_EXPERTS, 1, f),
                  exp_w_up, exp_b_up.reshape(N_EXPERTS, 1, f),
                  exp_w_down, exp_b_down.reshape(N_EXPERTS, 1, d))
    out = _combine(slot_flat, ys, wt, x1, gate2, row(normf_g))
    return out.reshape(bsz, t, d)
```

```python
import jax
import jax.numpy as jnp
from jax import lax
from jax.experimental import pallas as pl
from jax.experimental.pallas import tpu as pltpu

F32 = jnp.float32
BF16 = jnp.bfloat16
HI = lax.Precision.HIGHEST

HEAD_DIM = 64
N_HEADS = 8
WIDTH = N_HEADS * HEAD_DIM
DECAY_LORA = 64
AAA_LORA = 64
GATE_LORA = 128
RWKV_PROJ = 3 * WIDTH + DECAY_LORA + AAA_LORA + GATE_LORA
GN_EPS = 64e-5
MOBA_BLOCK = 256
MOBA_TOP_K = 3
KV_GROUP = 4
BIAS_ROWS = 16
K_AUG = 128
V_AUG = HEAD_DIM + 16
HEADS_PER_STEP = 2
LOG2E = 1.4426950408889634
N_EXPERTS = 32
EXPERT_TOP_K = 4
SWIGLU_LIMIT = 7.0
SWIGLU_ALPHA = 1.702
RMS_EPS = 1e-5
GATE_NEG = -1e30
MASK_NEG = -1e30

SCAN_CHUNK = 64
SCAN_CHUNKS = 4
EXPERT_TILE = 256
VMEM_LIMIT = 56 * 1024 * 1024
TOKEN_ROWS = 8
LANES = 128


def _cparams(*sem):
    return pltpu.CompilerParams(dimension_semantics=sem, vmem_limit_bytes=VMEM_LIMIT)


def _sigmoid(x):
    return 1.0 / (1.0 + jnp.exp(-x))


def _mod_kernel(c_ref, w_ref, b_ref, o_ref):
    c = c_ref[...]
    s = c * _sigmoid(c)
    o_ref[...] = jnp.dot(s, w_ref[...], precision=HI, preferred_element_type=F32) + b_ref[...]


def _mod(c, w_ada, b_ada):
    d = c.shape[-1]
    n = w_ada.shape[1]
    c8 = jnp.broadcast_to(c[:1], (8, d))
    tn = 1024
    out = pl.pallas_call(
        _mod_kernel,
        grid=(n // tn,),
        in_specs=[pl.BlockSpec((8, d), lambda j: (0, 0)),
                  pl.BlockSpec((d, tn), lambda j: (0, j)),
                  pl.BlockSpec((1, tn), lambda j: (0, j))],
        out_specs=pl.BlockSpec((8, tn), lambda j: (0, j)),
        out_shape=jax.ShapeDtypeStruct((8, n), F32),
        compiler_params=_cparams("arbitrary"),
        name="mod",
    )(c8, w_ada, b_ada.reshape(1, n))
    return out[:1]


def _inproj_kernel(x_ref, g_ref, sc_ref, sh_ref, wr_ref, wq_ref, wg_ref,
                   mu_ref, w0_ref, w2_ref, a0_ref, a2_ref, g2_ref, kk_ref, ka_ref, bd_ref,
                   sg_ref, km_ref, qh_ref, qt_ref, kaug_ref, vaug_ref,
                   r_out, lw_out, k_out, v_out, a_out, b_out, g_out, prev_ref):
    i = pl.program_id(0)
    x = x_ref[...]
    tm = x.shape[0]
    ms = jnp.mean(x * x, axis=-1, keepdims=True)
    h = x * lax.rsqrt(ms + RMS_EPS) * g_ref[...]
    h = h * (1.0 + sc_ref[...]) + sh_ref[...]
    hb = h.astype(BF16)
    _rwkv_prep(i, jnp.dot(hb, wr_ref[...], preferred_element_type=F32), prev_ref,
               mu_ref, w0_ref, w2_ref, a0_ref, a2_ref, g2_ref, kk_ref, ka_ref, bd_ref,
               (r_out, lw_out, k_out, v_out, a_out, b_out, g_out))
    sg_ref[...] = _sigmoid(jnp.dot(hb, wg_ref[...], preferred_element_type=F32))
    qkv = jnp.dot(hb, wq_ref[...], preferred_element_type=F32)
    km_ref[0] = jnp.mean(qkv[:, WIDTH:2 * WIDTH], axis=0, keepdims=True)
    col = lax.broadcasted_iota(jnp.int32, (tm, K_AUG - HEAD_DIM), 1)
    indicator = jnp.where(col == i % KV_GROUP, 1.0, 0.0).astype(BF16)
    row = lax.broadcasted_iota(jnp.int32, (V_AUG - HEAD_DIM, tm), 0)
    ones_row = jnp.where(row == 0, 1.0, 0.0).astype(BF16)
    for hd in range(N_HEADS):
        lo = hd * HEAD_DIM
        q = qkv[:, lo:lo + HEAD_DIM]
        kh = qkv[:, WIDTH + lo:WIDTH + lo + HEAD_DIM]
        vh = qkv[:, 2 * WIDTH + lo:2 * WIDTH + lo + HEAD_DIM]
        qh_ref[hd] = q
        qt_ref[hd] = (q * (HEAD_DIM ** -0.5 * LOG2E)).T.astype(BF16)
        kaug_ref[hd, 0] = jnp.concatenate([kh.astype(BF16), indicator], axis=1)
        vaug_ref[hd, 0] = jnp.concatenate([vh.T.astype(BF16), ones_row], axis=0)


def _inproj(x2, norm_g, scale, shift, w_r, w_q, w_g, rwkv_params):
    t, d = x2.shape
    tm = MOBA_BLOCK
    gs = MOBA_BLOCK * KV_GROUP
    ng = t // gs
    full = lambda a: pl.BlockSpec(a.shape, lambda i: (0, 0))
    row = lambda n: pl.BlockSpec((tm, n), lambda i: (i, 0))
    return pl.pallas_call(
        _inproj_kernel,
        grid=(t // tm,),
        in_specs=[row(d), full(norm_g), full(scale), full(shift), full(w_r), full(w_q), full(w_g)]
        + [full(a) for a in rwkv_params],
        out_specs=[row(w_g.shape[1]),
                   pl.BlockSpec((1, 1, WIDTH), lambda i: (i, 0, 0)),
                   pl.BlockSpec((N_HEADS, tm, HEAD_DIM), lambda i: (0, i, 0)),
                   pl.BlockSpec((N_HEADS, HEAD_DIM, tm), lambda i: (0, 0, i)),
                   pl.BlockSpec((N_HEADS, 1, tm, K_AUG), lambda i: (0, i // KV_GROUP, i % KV_GROUP, 0)),
                   pl.BlockSpec((N_HEADS, 1, V_AUG, tm), lambda i: (0, i // KV_GROUP, 0, i % KV_GROUP))]
        + [pl.BlockSpec((N_HEADS, tm, HEAD_DIM), lambda i: (0, i, 0))] * 7,
        out_shape=[jax.ShapeDtypeStruct((t, w_g.shape[1]), F32),
                   jax.ShapeDtypeStruct((t // tm, 1, WIDTH), F32),
                   jax.ShapeDtypeStruct((N_HEADS, t, HEAD_DIM), F32),
                   jax.ShapeDtypeStruct((N_HEADS, HEAD_DIM, t), BF16),
                   jax.ShapeDtypeStruct((N_HEADS, ng, gs, K_AUG), BF16),
                   jax.ShapeDtypeStruct((N_HEADS, ng, V_AUG, gs), BF16)]
        + [jax.ShapeDtypeStruct((N_HEADS, t, HEAD_DIM), F32)] * 7,
        scratch_shapes=[pltpu.VMEM((8, RWKV_PROJ), F32)],
        compiler_params=_cparams("arbitrary"),
        name="inproj",
    )(x2, norm_g, scale, shift, w_r, w_q, w_g, *rwkv_params)


def _rwkv_prep(i, p, prev_ref, mu_ref, w0_ref, w2_ref, a0_ref, a2_ref, g2_ref, kk_ref, ka_ref, bd_ref, out_refs):
    @pl.when(i == 0)
    def _():
        prev_ref[...] = jnp.zeros_like(prev_ref)

    tm = p.shape[0]
    rolled = pltpu.roll(p, 1, 0)
    row0 = lax.broadcasted_iota(jnp.int32, p.shape, 0) == 0
    p_prev = jnp.where(row0, prev_ref[0:1, :], rolled)
    prev_ref[0:1, :] = p[tm - 1:tm, :]
    ps = p + (p_prev - p) * mu_ref[...]
    r = ps[:, 0:WIDTH]
    k = ps[:, WIDTH:2 * WIDTH]
    v = ps[:, 2 * WIDTH:3 * WIDTH]
    xwa = ps[:, 3 * WIDTH:3 * WIDTH + DECAY_LORA + AAA_LORA]
    xg = ps[:, 3 * WIDTH + DECAY_LORA + AAA_LORA:RWKV_PROJ]
    dot = lambda lhs, w_ref: jnp.dot(lhs.astype(BF16), w_ref[...], preferred_element_type=F32)
    z = w0_ref[...] + dot(jnp.tanh(xwa), w2_ref)
    nz = -z
    softplus = jnp.maximum(nz, 0.0) + jnp.log(1.0 + jnp.exp(-jnp.abs(nz)))
    w = -softplus - 0.5
    a = _sigmoid(a0_ref[...] + dot(xwa, a2_ref))
    g = dot(_sigmoid(xg), g2_ref)
    kk = k * kk_ref[...]
    sq = kk * kk
    sq_hi = sq.astype(BF16)
    ss = dot(sq_hi, bd_ref) + dot(sq - sq_hi.astype(F32), bd_ref)
    kk = kk / jnp.maximum(jnp.sqrt(ss), 1e-12)
    vals = (r, -jnp.exp(w),
            k * (1.0 + (a - 1.0) * ka_ref[...]), v, -kk, kk * a, g)
    for ref, val in zip(out_refs, vals):
        for hd in range(N_HEADS):
            ref[hd] = val[:, hd * HEAD_DIM:(hd + 1) * HEAD_DIM]


def _bmm(a, b):
    return jnp.einsum("hmk,hkn->hmn", a.astype(BF16), b.astype(BF16), preferred_element_type=F32)


def _bmm_nt(a, b):
    return jnp.einsum("hmk,hnk->hmn", a.astype(BF16), b.astype(BF16), preferred_element_type=F32)


def _bmm_tn(a, b):
    return jnp.einsum("hkm,hkn->hmn", a.astype(BF16), b.astype(BF16), preferred_element_type=F32)


def _rwkvscan_kernel(r_ref, lw_ref, k_ref, v_ref, a_ref, b_ref, g_ref, lnw_ref, lnb_ref, rk_ref,
                     y_ref, s_ref):
    i = pl.program_id(0)

    @pl.when(i == 0)
    def _():
        s_ref[...] = jnp.zeros_like(s_ref)

    c = SCAN_CHUNK
    nc = SCAN_CHUNKS
    nh, _, n = r_ref.shape
    nb = nh * nc
    split = lambda ref: ref[...].reshape(nb, c, n)
    r, lw, k, v, a, b = (split(ref) for ref in (r_ref, lw_ref, k_ref, v_ref, a_ref, b_ref))

    row = lax.broadcasted_iota(jnp.int32, (nb, c, c), 1)
    col = lax.broadcasted_iota(jnp.int32, (nb, c, c), 2)
    lower = row >= col
    strict = row > col
    lw_hi = lw.astype(BF16).astype(F32)
    lw_mid = (lw - lw_hi).astype(BF16).astype(F32)
    cum = _bmm(lower, lw_hi) + _bmm(lower, lw_mid) + _bmm(lower, lw - lw_hi - lw_mid)
    tot = cum[:, c - 1:c, :]
    g_in = jnp.exp(cum)
    g_ex = jnp.exp(cum - lw)
    g_inv = jnp.exp(-cum)
    g_rem = jnp.exp(tot - cum)
    a_t = a * g_ex
    r_t = r * g_in
    b_t = b * g_inv
    k_t = k * g_inv
    l_ab = jnp.where(strict, _bmm_nt(a_t, b_t), 0.0)
    l_ak = jnp.where(strict, _bmm_nt(a_t, k_t), 0.0)
    m_rb = jnp.where(lower, _bmm_nt(r_t, b_t), 0.0)
    m_rk = jnp.where(lower, _bmm_nt(r_t, k_t), 0.0)
    same16 = (row // 16) == (col // 16)
    same32 = (row // 32) == (col // 32)
    diag16 = jnp.where(same16, l_ab, 0.0)
    inv = jnp.where(row == col, 1.0, 0.0) + diag16
    lp = diag16
    for _ in range(3):
        lp = _bmm(lp, lp)
        inv = inv + _bmm(inv, lp)
    off32 = jnp.where(jnp.logical_and(same32, jnp.logical_not(same16)), l_ab, 0.0)
    inv = inv + _bmm(_bmm(inv, off32), inv)
    off64 = jnp.where(same32, 0.0, l_ab)
    inv = inv + _bmm(_bmm(inv, off64), inv)
    a_hat = _bmm(inv, a_t)
    w_hat = _bmm(inv, _bmm(l_ak, v))
    r_hat = r_t + _bmm(m_rb, a_hat)
    y0 = _bmm(m_rb, w_hat) + _bmm(m_rk, v)
    b_hat = b * g_rem
    k_hat = k * g_rem
    decay = jnp.exp(tot)

    chunk = lambda x, j: x.reshape(nh, nc, x.shape[1], n)[:, j]
    s = s_ref[...]
    ys = []
    for j in range(nc):
        u = _bmm_nt(chunk(a_hat, j), s) + chunk(w_hat, j)
        ys.append(_bmm_nt(chunk(r_hat, j), s) + chunk(y0, j))
        s = s * chunk(decay, j) + _bmm_tn(u, chunk(b_hat, j)) + _bmm_tn(chunk(v, j), chunk(k_hat, j))
    s_ref[...] = s
    y = jnp.concatenate(ys, axis=1)

    r = r_ref[...]
    mean = jnp.mean(y, axis=-1, keepdims=True)
    yc = y - mean
    var = jnp.mean(yc * yc, axis=-1, keepdims=True)
    yn = yc * lax.rsqrt(var + GN_EPS) * lnw_ref[...] + lnb_ref[...]
    bonus = jnp.sum(r * k_ref[...] * rk_ref[...], axis=-1, keepdims=True) * v_ref[...]
    out = (yn + bonus) * g_ref[...]
    for hd in range(nh):
        y_ref[:, hd * n:(hd + 1) * n] = out[hd]


def _rwkvscan(r, lw, k, v, a, b, g, ln_w, ln_b, r_k):
    nh, t, n = r.shape
    rows = SCAN_CHUNK * SCAN_CHUNKS
    blk = pl.BlockSpec((nh, rows, n), lambda i: (0, i, 0))
    par = pl.BlockSpec((nh, 1, n), lambda i: (0, 0, 0))
    return pl.pallas_call(
        _rwkvscan_kernel,
        grid=(t // rows,),
        in_specs=[blk] * 7 + [par] * 3,
        out_specs=pl.BlockSpec((rows, nh * n), lambda i: (i, 0)),
        out_shape=jax.ShapeDtypeStruct((t, nh * n), F32),
        scratch_shapes=[pltpu.VMEM((nh, n, n), F32)],
        compiler_params=_cparams("arbitrary"),
        name="rwkvscan",
    )(r, lw, k, v, a, b, g, ln_w, ln_b, r_k)


def _mobasel_kernel(q_ref, km_ref, o_ref):
    j = pl.program_id(1)
    q = q_ref[0]
    km = km_ref[0]
    nb = km.shape[0]
    tq = q.shape[0]
    gate = lax.dot_general(km, q, (((1,), (1,)), ((), ())), precision=HI, preferred_element_type=F32)
    n_id = lax.broadcasted_iota(jnp.int32, (nb, tq), 0)
    t_id = lax.broadcasted_iota(jnp.int32, (nb, tq), 1) + j * tq
    q_blk = t_id // MOBA_BLOCK
    cand = n_id < nb - 1
    past = n_id < q_blk
    neg_inf = jnp.float32(-jnp.inf)
    gate = jnp.where(cand, jnp.where(past, gate, GATE_NEG), neg_inf)
    sel = jnp.zeros((nb, tq), F32)
    for _ in range(min(MOBA_TOP_K, nb - 1)):
        m = jnp.max(gate, axis=0, keepdims=True)
        idx = jnp.min(jnp.where(gate == m, n_id, nb), axis=0, keepdims=True)
        pick = n_id == idx
        sel = jnp.where(pick, 1.0, sel)
        gate = jnp.where(pick, neg_inf, gate)
    bias = jnp.where(jnp.logical_or(jnp.logical_and(past, sel > 0.5), n_id == q_blk), 0.0, MASK_NEG)
    pad = jnp.zeros((BIAS_ROWS - KV_GROUP, tq), F32)
    for g in range(nb // KV_GROUP):
        o_ref[0, g] = jnp.concatenate([bias[g * KV_GROUP:(g + 1) * KV_GROUP], pad], axis=0).astype(BF16)


def _mobasel(q_hm, kmean):
    nh, t, n = q_hm.shape
    nb = kmean.shape[1]
    ng = nb // KV_GROUP
    tq = min(t, 2048)
    return pl.pallas_call(
        _mobasel_kernel,
        grid=(nh, t // tq),
        in_specs=[pl.BlockSpec((1, tq, n), lambda h, j: (h, j, 0)),
                  pl.BlockSpec((1, nb, n), lambda h, j: (h, 0, 0))],
        out_specs=pl.BlockSpec((1, ng, BIAS_ROWS, tq), lambda h, j: (h, 0, 0, j)),
        out_shape=jax.ShapeDtypeStruct((nh, ng, BIAS_ROWS, t), BF16),
        compiler_params=_cparams("arbitrary", "arbitrary"),
        name="mobasel",
    )(q_hm, kmean)


def _mobaattn_kernel(qt_ref, k_ref, vt_ref, bias_ref, o_ref, sa_ref, sb_ref, m_ref, acc_ref):
    i = pl.program_id(1)
    bs = MOBA_BLOCK
    gd = i // KV_GROUP
    q_pad = jnp.zeros((K_AUG - HEAD_DIM - BIAS_ROWS, bs), BF16)
    heads = range(HEADS_PER_STEP)

    def scores_to(dst, g):
        for hh in heads:
            q_aug = jnp.concatenate([qt_ref[hh], bias_ref[hh, g], q_pad], axis=0)
            dst[hh] = jnp.dot(k_ref[hh, g], q_aug, preferred_element_type=F32)

    def update_from(src, g, keep=None):
        for hh in heads:
            s = src[hh] if keep is None else jnp.where(keep, src[hh], MASK_NEG)
            m = m_ref[hh]
            m_new = jnp.maximum(m, jnp.max(s, axis=0, keepdims=True))
            acc = jnp.exp2(m - m_new) * acc_ref[hh]
            for b in range(KV_GROUP):
                p = jnp.exp2((s[b * bs:(b + 1) * bs] - m_new).astype(BF16))
                acc = acc + jnp.dot(vt_ref[hh, g, :, pl.ds(b * bs, bs)], p, preferred_element_type=F32)
            acc_ref[hh] = acc
            m_ref[hh] = m_new

    def finish(src):
        causal = lax.broadcasted_iota(jnp.int32, (bs, bs), 0) <= lax.broadcasted_iota(jnp.int32, (bs, bs), 1)
        own = jnp.concatenate([jnp.logical_or(causal, i % KV_GROUP != b) for b in range(KV_GROUP)], axis=0)
        update_from(src, gd, own)
        for hh in heads:
            acc = acc_ref[hh]
            o_ref[hh] = acc[:HEAD_DIM] / acc[HEAD_DIM:HEAD_DIM + 1]

    m_ref[...] = jnp.full(m_ref.shape, MASK_NEG, F32)
    acc_ref[...] = jnp.zeros(acc_ref.shape, F32)
    scores_to(sa_ref, 0)

    def pair(k, _):
        g = 2 * k
        scores_to(sb_ref, g + 1)
        update_from(sa_ref, g)
        scores_to(sa_ref, g + 2)
        update_from(sb_ref, g + 1)
        return 0

    lax.fori_loop(0, gd // 2, pair, 0)

    @pl.when(gd % 2 == 1)
    def _():
        scores_to(sb_ref, gd)
        update_from(sa_ref, gd - 1)
        finish(sb_ref)

    @pl.when(gd % 2 == 0)
    def _():
        finish(sa_ref)


def _mobaattn(qt, k_aug, vt_aug, bias):
    nh, n, t = qt.shape
    gs = MOBA_BLOCK * KV_GROUP
    ng = t // gs
    hp = HEADS_PER_STEP
    return pl.pallas_call(
        _mobaattn_kernel,
        grid=(nh // hp, t // MOBA_BLOCK),
        in_specs=[pl.BlockSpec((hp, n, MOBA_BLOCK), lambda h, i: (h, 0, i)),
                  pl.BlockSpec((hp, ng, gs, K_AUG), lambda h, i: (h, 0, 0, 0)),
                  pl.BlockSpec((hp, ng, V_AUG, gs), lambda h, i: (h, 0, 0, 0)),
                  pl.BlockSpec((hp, ng, BIAS_ROWS, MOBA_BLOCK), lambda h, i: (h, 0, 0, i))],
        out_specs=pl.BlockSpec((hp, n, MOBA_BLOCK), lambda h, i: (h, 0, i)),
        out_shape=jax.ShapeDtypeStruct((nh, n, t), F32),
        scratch_shapes=[pltpu.VMEM((hp, gs, MOBA_BLOCK), F32), pltpu.VMEM((hp, gs, MOBA_BLOCK), F32),
                        pltpu.VMEM((hp, 1, MOBA_BLOCK), F32), pltpu.VMEM((hp, V_AUG, MOBA_BLOCK), F32)],
        compiler_params=_cparams("arbitrary", "arbitrary"),
        name="mobaattn",
    )(qt, k_aug, vt_aug, bias)


def _load_token_tiles(ref, n_tokens, lead=(), first=0):
    chunks = [ref[lead + (pl.ds(first * TOKEN_ROWS + ch, n_tokens, stride=TOKEN_ROWS), slice(None))]
              for ch in range(TOKEN_ROWS)]
    return jnp.concatenate(chunks, axis=1)


def _store_token_tiles(ref, val, first=0):
    for ch in range(TOKEN_ROWS):
        ref[pl.ds(first * TOKEN_ROWS + ch, val.shape[0], stride=TOKEN_ROWS), :] = val[:, ch * LANES:(ch + 1) * LANES]


def _mix_kernel(ya_ref, yb_ref, sg_ref, x_ref, wa_ref, wb_ref, wo_ref, g1_ref, n2_ref, sc_ref, sh_ref,
                rw_ref, rb_ref, x1_ref, h2_ref, eid_ref, rank_ref, wt_ref, cnt_ref, base_ref):
    i = pl.program_id(0)

    @pl.when(i == 0)
    def _():
        base_ref[...] = jnp.zeros_like(base_ref)

    d = x_ref.shape[1]
    tm = x_ref.shape[0]
    pa = jnp.dot(ya_ref[...].astype(BF16), wa_ref[...], preferred_element_type=F32)
    yb = yb_ref[...].reshape(WIDTH, tm).T
    pb = jnp.dot(yb.astype(BF16), wb_ref[...], preferred_element_type=F32)
    sg = sg_ref[...]
    mixed = sg[:, :d] * pa + sg[:, d:] * pb
    mixed = jnp.dot(mixed.astype(BF16), wo_ref[...], preferred_element_type=F32)
    x1 = x_ref[...] + g1_ref[...] * mixed
    x1_ref[...] = x1
    ms = jnp.mean(x1 * x1, axis=-1, keepdims=True)
    h2 = x1 * lax.rsqrt(ms + RMS_EPS) * n2_ref[...]
    h2 = h2 * (1.0 + sc_ref[...]) + sh_ref[...]
    _store_token_tiles(h2_ref, h2)

    ne = rw_ref.shape[1] // 2
    h2_hi = h2.astype(BF16)
    h2_lo = (h2 - h2_hi.astype(F32)).astype(BF16)
    by_hi = jnp.dot(h2_hi, rw_ref[...], preferred_element_type=F32)
    by_lo = jnp.dot(h2_lo, rw_ref[:, :ne], preferred_element_type=F32)
    logits = by_hi[:, :ne] + by_hi[:, ne:] + by_lo + rb_ref[...]
    e_id = lax.broadcasted_iota(jnp.int32, (tm, ne), 1)
    neg_inf = jnp.float32(-jnp.inf)
    work = logits
    picks, vals = [], []
    for _ in range(EXPERT_TOP_K):
        m = jnp.max(work, axis=-1, keepdims=True)
        idx = jnp.min(jnp.where(work == m, e_id, ne), axis=-1, keepdims=True)
        pick = e_id == idx
        picks.append((idx, pick))
        vals.append(m)
        work = jnp.where(pick, neg_inf, work)
    den = sum(jnp.exp(v - vals[0]) for v in vals)
    chosen = jnp.zeros((tm, ne), F32)
    for _, pick in picks:
        chosen = jnp.where(pick, 1.0, chosen)
    r_id = lax.broadcasted_iota(jnp.int32, (tm, tm), 0)
    c_id = lax.broadcasted_iota(jnp.int32, (tm, tm), 1)
    before = jnp.where(r_id > c_id, 1.0, 0.0).astype(BF16)
    ahead = jnp.dot(before, chosen.astype(BF16), preferred_element_type=F32) + base_ref[0:1, :]
    k_id = lax.broadcasted_iota(jnp.int32, (tm, EXPERT_TOP_K), 1)
    eid = jnp.zeros((tm, EXPERT_TOP_K), jnp.int32)
    rank = jnp.zeros((tm, EXPERT_TOP_K), jnp.int32)
    wt = jnp.zeros((tm, EXPERT_TOP_K), F32)
    for kk, ((idx, pick), v) in enumerate(zip(picks, vals)):
        rk = jnp.sum(jnp.where(pick, ahead, 0.0), axis=-1, keepdims=True).astype(jnp.int32)
        eid = jnp.where(k_id == kk, idx, eid)
        rank = jnp.where(k_id == kk, rk, rank)
        wt = jnp.where(k_id == kk, jnp.exp(v - vals[0]) / den, wt)
    eid_ref[...] = eid
    rank_ref[...] = rank
    wt_ref[...] = wt
    total = base_ref[0:1, :] + jnp.sum(chosen, axis=0, keepdims=True)
    base_ref[0:1, :] = total
    cnt_ref[...] = jnp.broadcast_to(total, cnt_ref.shape)


def _mix(ya, yb, sg, x2, w_a, w_b, w_o, gate1, norm2_g, scale2, shift2, router_w, router_b):
    t, d = x2.shape
    tm = 512
    ne = router_w.shape[1]
    rw_hi = router_w.astype(BF16)
    router_w = jnp.concatenate([rw_hi, (router_w - rw_hi.astype(F32)).astype(BF16)], axis=1)
    full = lambda a: pl.BlockSpec(a.shape, lambda i: (0, 0))
    row = lambda n: pl.BlockSpec((tm, n), lambda i: (i, 0))
    return pl.pallas_call(
        _mix_kernel,
        grid=(t // tm,),
        in_specs=[row(WIDTH), pl.BlockSpec((N_HEADS, HEAD_DIM, tm), lambda i: (0, 0, i)), row(2 * d), row(d),
                  full(w_a), full(w_b), full(w_o),
                  full(gate1), full(norm2_g), full(scale2), full(shift2), full(router_w), full(router_b)],
        out_specs=[row(d), pl.BlockSpec((tm * TOKEN_ROWS, LANES), lambda i: (i, 0)),
                   row(EXPERT_TOP_K), row(EXPERT_TOP_K), row(EXPERT_TOP_K),
                   pl.BlockSpec((8, ne), lambda i: (0, 0))],
        out_shape=[jax.ShapeDtypeStruct((t, d), F32), jax.ShapeDtypeStruct((t * TOKEN_ROWS, LANES), F32),
                   jax.ShapeDtypeStruct((t, EXPERT_TOP_K), jnp.int32),
                   jax.ShapeDtypeStruct((t, EXPERT_TOP_K), jnp.int32),
                   jax.ShapeDtypeStruct((t, EXPERT_TOP_K), F32),
                   jax.ShapeDtypeStruct((8, ne), F32)],
        scratch_shapes=[pltpu.VMEM((8, ne), F32)],
        compiler_params=_cparams("arbitrary"),
        name="mix",
    )(ya, yb, sg, x2, w_a, w_b, w_o, gate1, norm2_g, scale2, shift2, router_w, router_b)


DISPATCH_TILE = 512


def _slots_kernel(off_ref, eid_ref, rank_ref, o_ref):
    eid = eid_ref[...]
    slot = rank_ref[...]
    for e in range(N_EXPERTS):
        slot = slot + jnp.where(eid == e, off_ref[e], 0)
    o_ref[...] = slot * TOKEN_ROWS


def _slots(off, eid, rank):
    n = eid.size
    lanes = 128
    shape = (n // lanes, lanes)
    full = pl.BlockSpec(shape, lambda i: (0, 0))
    out = pl.pallas_call(
        _slots_kernel,
        grid=(1,),
        in_specs=[pl.BlockSpec(memory_space=pltpu.SMEM), full, full],
        out_specs=full,
        out_shape=jax.ShapeDtypeStruct(shape, jnp.int32),
        compiler_params=_cparams("arbitrary"),
        name="slots",
    )(off, eid.reshape(shape), rank.reshape(shape))
    return out.reshape(n)


def _dispatch_kernel(slot_ref, last_ref, nu_ref, h_ref, xs_ref, zeros_ref, sem, zsem):
    tm = DISPATCH_TILE
    tile_rows = EXPERT_TILE * TOKEN_ROWS
    n_tiles = xs_ref.shape[0] // tile_rows

    @pl.when(pl.program_id(0) == 0)
    def _():
        zeros_ref[...] = jnp.zeros_like(zeros_ref)
        zero_tile = lambda j: pltpu.make_async_copy(
            zeros_ref, xs_ref.at[pl.ds(pl.multiple_of(j * tile_rows, tile_rows), tile_rows)], zsem)

        def each_padded_tile(fn):
            def expert(e, c):
                @pl.when(last_ref[e] >= 0)
                def _():
                    fn(last_ref[e])
                return c
            lax.fori_loop(0, N_EXPERTS, expert, 0)
            lax.fori_loop(nu_ref[0], n_tiles, lambda j, c: (fn(j), c)[1], 0)

        each_padded_tile(lambda j: zero_tile(j).start())
        each_padded_tile(lambda j: zero_tile(j).wait())

    def body(t, _):
        for kk in range(EXPERT_TOP_K):
            row = pl.multiple_of(slot_ref[t * EXPERT_TOP_K + kk], TOKEN_ROWS)
            pltpu.make_async_copy(h_ref.at[pl.ds(t * TOKEN_ROWS, TOKEN_ROWS)], xs_ref.at[pl.ds(row, TOKEN_ROWS)],
                                  sem).start(priority=kk % 2)
        return 0

    lax.fori_loop(0, tm, body, 0, unroll=2)
    for _ in range(EXPERT_TOP_K):
        pltpu.make_async_copy(h_ref, xs_ref.at[pl.ds(0, tm * TOKEN_ROWS)], sem).wait()


def _dispatch(slot_flat, last_tile, n_used, h2, n_slots):
    rows, lanes = h2.shape
    tm = DISPATCH_TILE
    t = rows // TOKEN_ROWS
    smem = pl.BlockSpec(memory_space=pltpu.SMEM)
    return pl.pallas_call(
        _dispatch_kernel,
        grid=(t // tm,),
        in_specs=[pl.BlockSpec((tm * EXPERT_TOP_K,), lambda i: (i,), memory_space=pltpu.SMEM), smem, smem,
                  pl.BlockSpec((tm * TOKEN_ROWS, lanes), lambda i: (i, 0))],
        out_specs=pl.BlockSpec(memory_space=pl.ANY),
        out_shape=jax.ShapeDtypeStruct((n_slots * TOKEN_ROWS, lanes), F32),
        scratch_shapes=[pltpu.VMEM((EXPERT_TILE * TOKEN_ROWS, lanes), F32),
                        pltpu.SemaphoreType.DMA(()), pltpu.SemaphoreType.DMA(())],
        compiler_params=pltpu.CompilerParams(dimension_semantics=("arbitrary",), has_side_effects=True),
        name="dispatch",
    )(slot_flat, last_tile, n_used, h2)


def _experts_kernel(te_ref, nu_ref, nx_ref, rp_ref, x_ref, wg_hbm, bg_ref, wu_hbm, bu_ref, wd_hbm, bd_ref, y_ref,
                    wbuf, wg_s, wu_s, wd_s, sems):
    i = pl.program_id(0)
    used = i < nu_ref[0]
    new_expert = jnp.logical_or(i == 0, te_ref[i] != te_ref[jnp.maximum(i - 1, 0)])

    def fetch(e, slot):
        return [pltpu.make_async_copy(w.at[e], wbuf.at[slot, j], sems.at[slot, j])
                for j, w in enumerate((wg_hbm, wu_hbm, wd_hbm))]

    @pl.when(jnp.logical_and(used, new_expert))
    def _():
        slot = rp_ref[i]

        @pl.when(i == 0)
        def _():
            for copy in fetch(te_ref[0], 0):
                copy.start()

        for copy in fetch(te_ref[i], slot):
            copy.wait()
        wg_s[...] = wbuf[slot, 0].astype(BF16)
        wu_s[...] = wbuf[slot, 1].astype(BF16)
        wd_s[...] = wbuf[slot, 2].astype(BF16)

        @pl.when(nx_ref[i] >= 0)
        def _():
            for copy in fetch(nx_ref[i], 1 - slot):
                copy.start()

    @pl.when(used)
    def _():
        x = _load_token_tiles(x_ref, EXPERT_TILE).astype(BF16)
        gt = jnp.minimum(jnp.dot(x, wg_s[...], preferred_element_type=F32) + bg_ref[0], SWIGLU_LIMIT)
        up = jnp.clip(jnp.dot(x, wu_s[...], preferred_element_type=F32) + bu_ref[0], -SWIGLU_LIMIT, SWIGLU_LIMIT)
        hid = (up + 1.0) * gt * _sigmoid(SWIGLU_ALPHA * gt)
        _store_token_tiles(y_ref, jnp.dot(hid.astype(BF16), wd_s[...], preferred_element_type=F32) + bd_ref[0])

    @pl.when(jnp.logical_not(used))
    def _():
        y_ref[...] = jnp.zeros_like(y_ref)


def _experts(tile_expert, n_used, next_expert, run_parity, xs, w_gate, b_gate, w_up, b_up, w_down, b_down):
    rows, lanes = xs.shape
    d = TOKEN_ROWS * lanes
    f = w_gate.shape[2]
    assert d == f
    tm = EXPERT_TILE
    n_tiles = rows // (tm * TOKEN_ROWS)
    row = pl.BlockSpec((tm * TOKEN_ROWS, lanes), lambda i, te, nu, nx, rp: (i, 0))
    bias = lambda n: pl.BlockSpec((1, 1, n), lambda i, te, nu, nx, rp: (te[i], 0, 0))
    hbm = pl.BlockSpec(memory_space=pl.ANY)
    grid_spec = pltpu.PrefetchScalarGridSpec(
        num_scalar_prefetch=4,
        grid=(n_tiles,),
        in_specs=[row, hbm, bias(f), hbm, bias(f), hbm, bias(d)],
        out_specs=row,
        scratch_shapes=[pltpu.VMEM((2, 3, d, f), F32),
                        pltpu.VMEM((d, f), BF16), pltpu.VMEM((d, f), BF16), pltpu.VMEM((f, d), BF16),
                        pltpu.SemaphoreType.DMA((2, 3))],
    )
    return pl.pallas_call(
        _experts_kernel,
        grid_spec=grid_spec,
        out_shape=jax.ShapeDtypeStruct((n_tiles * tm * TOKEN_ROWS, lanes), F32),
        compiler_params=_cparams("arbitrary"),
        name="experts",
    )(tile_expert, n_used, next_expert, run_parity, xs, w_gate, b_gate, w_up, b_up, w_down, b_down)


def _combine_kernel(slot_ref, next_ref, ys_ref, wt_ref, x1_ref, g2_ref, nf_ref, o_ref, buf, sems):
    i = pl.program_id(0)
    n = pl.num_programs(0)
    tm = DISPATCH_TILE

    def issue(slots, p):
        def body(t, _):
            for kk in range(EXPERT_TOP_K):
                row = pl.multiple_of(slots[t * EXPERT_TOP_K + kk], TOKEN_ROWS)
                pltpu.make_async_copy(ys_ref.at[pl.ds(row, TOKEN_ROWS)],
                                      buf.at[p, kk, pl.ds(t * TOKEN_ROWS, TOKEN_ROWS)],
                                      sems.at[p]).start(priority=kk % 2)
            return 0
        lax.fori_loop(0, tm, body, 0, unroll=2)

    p = i % 2

    @pl.when(i == 0)
    def _():
        issue(slot_ref, 0)

    @pl.when(i + 1 < n)
    def _():
        issue(next_ref, 1 - p)

    for kk in range(EXPERT_TOP_K):
        pltpu.make_async_copy(ys_ref.at[pl.ds(0, tm * TOKEN_ROWS)], buf.at[p, kk], sems.at[p]).wait()
    wt = wt_ref[...]
    moe = wt[:, 0:1] * _load_token_tiles(buf, tm, (p, 0))
    for kk in range(1, EXPERT_TOP_K):
        moe = moe + wt[:, kk:kk + 1] * _load_token_tiles(buf, tm, (p, kk))
    x = x1_ref[...] + g2_ref[...] * moe
    ms = jnp.mean(x * x, axis=-1, keepdims=True)
    o_ref[...] = x * lax.rsqrt(ms + RMS_EPS) * nf_ref[...]


def _combine(slot_flat, ys, wt, x1, gate2, normf_g):
    t, d = x1.shape
    tm = DISPATCH_TILE
    n = t // tm
    full = lambda a: pl.BlockSpec(a.shape, lambda i: (0, 0))
    row = lambda w: pl.BlockSpec((tm, w), lambda i: (i, 0))
    slots = lambda index: pl.BlockSpec((tm * EXPERT_TOP_K,), index, memory_space=pltpu.SMEM)
    return pl.pallas_call(
        _combine_kernel,
        grid=(n,),
        in_specs=[slots(lambda i: (i,)), slots(lambda i: (jnp.minimum(i + 1, n - 1),)),
                  pl.BlockSpec(memory_space=pl.ANY),
                  row(EXPERT_TOP_K), row(d), full(gate2), full(normf_g)],
        out_specs=row(d),
        out_shape=jax.ShapeDtypeStruct((t, d), F32),
        scratch_shapes=[pltpu.VMEM((2, EXPERT_TOP_K, tm * TOKEN_ROWS, LANES), F32), pltpu.SemaphoreType.DMA((2,))],
        compiler_params=_cparams("arbitrary"),
        name="combine",
    )(slot_flat, slot_flat, ys, wt, x1, gate2, normf_g)


def kernel(x, c, w_ada, b_ada, norm1_g, w_in, rwkv_mu, rwkv_w0, rwkv_w2, rwkv_a0, rwkv_a2, rwkv_g2, rwkv_k_k, rwkv_k_a, rwkv_r_k, rwkv_ln_w, rwkv_ln_b, w_branch_a, w_branch_b, w_out, norm2_g, router_w, router_b, exp_w_gate, exp_b_gate, exp_w_up, exp_b_up, exp_w_down, exp_b_down, normf_g):
    bsz, t, d = x.shape
    assert bsz == 1 and t % (MOBA_BLOCK * KV_GROUP) == 0
    x2 = x.reshape(t, d)
    row = lambda a: a.reshape(1, -1)

    mod = _mod(c, w_ada, b_ada)
    shift1, scale1, gate1, shift2, scale2, gate2 = [mod[:, j * d:(j + 1) * d] for j in range(6)]

    qkv_end = RWKV_PROJ + 3 * WIDTH
    w_in_b = w_in.astype(BF16)
    zeros_lora = jnp.zeros((DECAY_LORA, WIDTH), F32)
    w2p = jnp.concatenate([rwkv_w2, zeros_lora], axis=0)
    a2p = jnp.concatenate([zeros_lora, rwkv_a2], axis=0)
    head_of = jnp.arange(WIDTH) // HEAD_DIM
    bd = (head_of[:, None] == head_of[None, :]).astype(BF16)
    rwkv_params = (row(rwkv_mu), row(rwkv_w0), w2p.astype(BF16), row(rwkv_a0), a2p.astype(BF16),
                   rwkv_g2.astype(BF16), row(rwkv_k_k), row(rwkv_k_a), bd)
    sg, kmean, q_hm, qt, k_aug, vt_aug, r, lw, k, v, av, bv, g = _inproj(
        x2, row(norm1_g), scale1, shift1, w_in_b[:, :RWKV_PROJ], w_in_b[:, RWKV_PROJ:qkv_end], w_in_b[:, qkv_end:],
        rwkv_params)

    per_head = lambda a: a.reshape(N_HEADS, 1, HEAD_DIM)
    y_a = _rwkvscan(r, lw, k, v, av, bv, g, per_head(rwkv_ln_w), per_head(rwkv_ln_b), per_head(rwkv_r_k))

    nb = t // MOBA_BLOCK
    bias = _mobasel(q_hm, kmean.reshape(nb, N_HEADS, HEAD_DIM).transpose(1, 0, 2))
    y_b = _mobaattn(qt, k_aug, vt_aug, bias)

    x1, h2, eid, rank, wt, cnt = _mix(y_a, y_b, sg, x2, w_branch_a.astype(BF16), w_branch_b.astype(BF16),
                                      w_out.astype(BF16), gate1, row(norm2_g), scale2, shift2,
                                      router_w, row(router_b))

    counts = cnt[0].astype(jnp.int32)
    tiles_per = (counts + EXPERT_TILE - 1) // EXPERT_TILE
    tile_end = jnp.cumsum(tiles_per)
    off = (tile_end - tiles_per) * EXPERT_TILE
    n_tiles = (t * EXPERT_TOP_K + N_EXPERTS * (EXPERT_TILE - 1)) // EXPERT_TILE
    n_used = tile_end[-1:]
    tile_expert = jnp.minimum(jnp.sum(tile_end[None, :] <= jnp.arange(n_tiles)[:, None], axis=1),
                              N_EXPERTS - 1).astype(jnp.int32)
    last_used = tile_expert[jnp.maximum(n_used[0] - 1, 0)]
    tile_expert = jnp.where(jnp.arange(n_tiles) < n_used[0], tile_expert, last_used)

    slot_flat = _slots(off, eid, rank)
    last_tile = jnp.where(tiles_per > 0, tile_end - 1, -1).astype(jnp.int32)
    xs = _dispatch(slot_flat, last_tile, n_used, h2, n_tiles * EXPERT_TILE)
    f = exp_w_gate.shape[2]
    nonempty = tiles_per > 0
    run_index = jnp.cumsum(nonempty) - 1
    ids = jnp.arange(N_EXPERTS)
    later = jnp.logical_and(nonempty[None, :], ids[None, :] > ids[:, None])
    next_nonempty = jnp.min(jnp.where(later, ids[None, :], N_EXPERTS), axis=1)
    next_nonempty = jnp.where(next_nonempty < N_EXPERTS, next_nonempty, -1)
    next_expert = next_nonempty[tile_expert].astype(jnp.int32)
    run_parity = (run_index[tile_expert] % 2).astype(jnp.int32)
    ys = _experts(tile_expert, n_used, next_expert, run_parity, xs,
                  exp_w_gate, exp_b_gate.reshape(N_EXPERTS, 1, f),
                  exp_w_up, exp_b_up.reshape(N_EXPERTS, 1, f),
                  exp_w_down, exp_b_down.reshape(N_EXPERTS, 1, d))
    out = _combine(slot_flat, ys, wt, x1, gate2, row(normf_g))
    return out.reshape(bsz, t, d)
```

```python
import jax
import jax.numpy as jnp
from jax import lax
from jax.experimental import pallas as pl
from jax.experimental.pallas import tpu as pltpu

F32 = jnp.float32
BF16 = jnp.bfloat16
HI = lax.Precision.HIGHEST

HEAD_DIM = 64
N_HEADS = 8
WIDTH = N_HEADS * HEAD_DIM
DECAY_LORA = 64
AAA_LORA = 64
GATE_LORA = 128
RWKV_PROJ = 3 * WIDTH + DECAY_LORA + AAA_LORA + GATE_LORA
GN_EPS = 64e-5
MOBA_BLOCK = 256
MOBA_TOP_K = 3
KV_GROUP = 4
BIAS_ROWS = 16
K_AUG = 128
V_AUG = HEAD_DIM + 16
HEADS_PER_STEP = 2
LOG2E = 1.4426950408889634
N_EXPERTS = 32
EXPERT_TOP_K = 4
SWIGLU_LIMIT = 7.0
SWIGLU_ALPHA = 1.702
RMS_EPS = 1e-5
GATE_NEG = -1e30
MASK_NEG = -1e30

SCAN_CHUNK = 64
SCAN_CHUNKS = 4
EXPERT_TILE = 256
VMEM_LIMIT = 56 * 1024 * 1024
TOKEN_ROWS = 8
LANES = 128


def _cparams(*sem):
    return pltpu.CompilerParams(dimension_semantics=sem, vmem_limit_bytes=VMEM_LIMIT)


def _sigmoid(x):
    return 1.0 / (1.0 + jnp.exp(-x))


def _mod_kernel(c_ref, w_ref, b_ref, o_ref):
    c = c_ref[...]
    s = c * _sigmoid(c)
    o_ref[...] = jnp.dot(s, w_ref[...], precision=HI, preferred_element_type=F32) + b_ref[...]


def _mod(c, w_ada, b_ada):
    d = c.shape[-1]
    n = w_ada.shape[1]
    c8 = jnp.broadcast_to(c[:1], (8, d))
    tn = 1024
    out = pl.pallas_call(
        _mod_kernel,
        grid=(n // tn,),
        in_specs=[pl.BlockSpec((8, d), lambda j: (0, 0)),
                  pl.BlockSpec((d, tn), lambda j: (0, j)),
                  pl.BlockSpec((1, tn), lambda j: (0, j))],
        out_specs=pl.BlockSpec((8, tn), lambda j: (0, j)),
        out_shape=jax.ShapeDtypeStruct((8, n), F32),
        compiler_params=_cparams("arbitrary"),
        name="mod",
    )(c8, w_ada, b_ada.reshape(1, n))
    return out[:1]


def _inproj_kernel(x_ref, g_ref, sc_ref, sh_ref, wr_ref, wq_ref, wg_ref,
                   mu_ref, w0_ref, w2_ref, a0_ref, a2_ref, g2_ref, kk_ref, ka_ref, bd_ref,
                   sg_ref, km_ref, qh_ref, qt_ref, kaug_ref, vaug_ref,
                   r_out, lw_out, k_out, v_out, a_out, b_out, g_out, prev_ref):
    i = pl.program_id(0)
    x = x_ref[...]
    tm = x.shape[0]
    ms = jnp.mean(x * x, axis=-1, keepdims=True)
    h = x * lax.rsqrt(ms + RMS_EPS) * g_ref[...]
    h = h * (1.0 + sc_ref[...]) + sh_ref[...]
    hb = h.astype(BF16)
    _rwkv_prep(i, jnp.dot(hb, wr_ref[...], preferred_element_type=F32), prev_ref,
               mu_ref, w0_ref, w2_ref, a0_ref, a2_ref, g2_ref, kk_ref, ka_ref, bd_ref,
               (r_out, lw_out, k_out, v_out, a_out, b_out, g_out))
    sg_ref[...] = _sigmoid(jnp.dot(hb, wg_ref[...], preferred_element_type=F32))
    qkv = jnp.dot(hb, wq_ref[...], preferred_element_type=F32)
    km_ref[0] = jnp.mean(qkv[:, WIDTH:2 * WIDTH], axis=0, keepdims=True)
    col = lax.broadcasted_iota(jnp.int32, (tm, K_AUG - HEAD_DIM), 1)
    indicator = jnp.where(col == i % KV_GROUP, 1.0, 0.0).astype(BF16)
    row = lax.broadcasted_iota(jnp.int32, (V_AUG - HEAD_DIM, tm), 0)
    ones_row = jnp.where(row == 0, 1.0, 0.0).astype(BF16)
    for hd in range(N_HEADS):
        lo = hd * HEAD_DIM
        q = qkv[:, lo:lo + HEAD_DIM]
        kh = qkv[:, WIDTH + lo:WIDTH + lo + HEAD_DIM]
        vh = qkv[:, 2 * WIDTH + lo:2 * WIDTH + lo + HEAD_DIM]
        qh_ref[hd] = q
        qt_ref[hd] = (q * (HEAD_DIM ** -0.5 * LOG2E)).T.astype(BF16)
        kaug_ref[hd, 0] = jnp.concatenate([kh.astype(BF16), indicator], axis=1)
        vaug_ref[hd, 0] = jnp.concatenate([vh.T.astype(BF16), ones_row], axis=0)


def _inproj(x2, norm_g, scale, shift, w_r, w_q, w_g, rwkv_params):
    t, d = x2.shape
    tm = MOBA_BLOCK
    gs = MOBA_BLOCK * KV_GROUP
    ng = t // gs
    full = lambda a: pl.BlockSpec(a.shape, lambda i: (0, 0))
    row = lambda n: pl.BlockSpec((tm, n), lambda i: (i, 0))
    return pl.pallas_call(
        _inproj_kernel,
        grid=(t // tm,),
        in_specs=[row(d), full(norm_g), full(scale), full(shift), full(w_r), full(w_q), full(w_g)]
        + [full(a) for a in rwkv_params],
        out_specs=[row(w_g.shape[1]),
                   pl.BlockSpec((1, 1, WIDTH), lambda i: (i, 0, 0)),
                   pl.BlockSpec((N_HEADS, tm, HEAD_DIM), lambda i: (0, i, 0)),
                   pl.BlockSpec((N_HEADS, HEAD_DIM, tm), lambda i: (0, 0, i)),
                   pl.BlockSpec((N_HEADS, 1, tm, K_AUG), lambda i: (0, i // KV_GROUP, i % KV_GROUP, 0)),
                   pl.BlockSpec((N_HEADS, 1, V_AUG, tm), lambda i: (0, i // KV_GROUP, 0, i % KV_GROUP))]
        + [pl.BlockSpec((N_HEADS, tm, HEAD_DIM), lambda i: (0, i, 0))] * 7,
        out_shape=[jax.ShapeDtypeStruct((t, w_g.shape[1]), F32),
                   jax.ShapeDtypeStruct((t // tm, 1, WIDTH), F32),
                   jax.ShapeDtypeStruct((N_HEADS, t, HEAD_DIM), F32),
                   jax.ShapeDtypeStruct((N_HEADS, HEAD_DIM, t), BF16),
                   jax.ShapeDtypeStruct((N_HEADS, ng, gs, K_AUG), BF16),
                   jax.ShapeDtypeStruct((N_HEADS, ng, V_AUG, gs), BF16)]
        + [jax.ShapeDtypeStruct((N_HEADS, t, HEAD_DIM), F32)] * 7,
        scratch_shapes=[pltpu.VMEM((8, RWKV_PROJ), F32)],
        compiler_params=_cparams("arbitrary"),
        name="inproj",
    )(x2, norm_g, scale, shift, w_r, w_q, w_g, *rwkv_params)


def _rwkv_prep(i, p, prev_ref, mu_ref, w0_ref, w2_ref, a0_ref, a2_ref, g2_ref, kk_ref, ka_ref, bd_ref, out_refs):
    @pl.when(i == 0)
    def _():
        prev_ref[...] = jnp.zeros_like(prev_ref)

    tm = p.shape[0]
    rolled = pltpu.roll(p, 1, 0)
    row0 = lax.broadcasted_iota(jnp.int32, p.shape, 0) == 0
    p_prev = jnp.where(row0, prev_ref[0:1, :], rolled)
    prev_ref[0:1, :] = p[tm - 1:tm, :]
    ps = p + (p_prev - p) * mu_ref[...]
    r = ps[:, 0:WIDTH]
    k = ps[:, WIDTH:2 * WIDTH]
    v = ps[:, 2 * WIDTH:3 * WIDTH]
    xwa = ps[:, 3 * WIDTH:3 * WIDTH + DECAY_LORA + AAA_LORA]
    xg = ps[:, 3 * WIDTH + DECAY_LORA + AAA_LORA:RWKV_PROJ]
    dot = lambda lhs, w_ref: jnp.dot(lhs.astype(BF16), w_ref[...], preferred_element_type=F32)
    z = w0_ref[...] + dot(jnp.tanh(xwa), w2_ref)
    nz = -z
    softplus = jnp.maximum(nz, 0.0) + jnp.log(1.0 + jnp.exp(-jnp.abs(nz)))
    w = -softplus - 0.5
    a = _sigmoid(a0_ref[...] + dot(xwa, a2_ref))
    g = dot(_sigmoid(xg), g2_ref)
    kk = k * kk_ref[...]
    sq = kk * kk
    sq_hi = sq.astype(BF16)
    ss = dot(sq_hi, bd_ref) + dot(sq - sq_hi.astype(F32), bd_ref)
    kk = kk / jnp.maximum(jnp.sqrt(ss), 1e-12)
    vals = (r, -jnp.exp(w),
            k * (1.0 + (a - 1.0) * ka_ref[...]), v, -kk, kk * a, g)
    for ref, val in zip(out_refs, vals):
        for hd in range(N_HEADS):
            ref[hd] = val[:, hd * HEAD_DIM:(hd + 1) * HEAD_DIM]


def _bmm(a, b):
    return jnp.einsum("hmk,hkn->hmn", a.astype(BF16), b.astype(BF16), preferred_element_type=F32)


def _bmm_nt(a, b):
    return jnp.einsum("hmk,hnk->hmn", a.astype(BF16), b.astype(BF16), preferred_element_type=F32)


def _bmm_tn(a, b):
    return jnp.einsum("hkm,hkn->hmn", a.astype(BF16), b.astype(BF16), preferred_element_type=F32)


def _rwkvscan_kernel(r_ref, lw_ref, k_ref, v_ref, a_ref, b_ref, g_ref, lnw_ref, lnb_ref, rk_ref,
                     y_ref, s_ref):
    i = pl.program_id(0)

    @pl.when(i == 0)
    def _():
        s_ref[...] = jnp.zeros_like(s_ref)

    c = SCAN_CHUNK
    nc = SCAN_CHUNKS
    nh, _, n = r_ref.shape
    nb = nh * nc
    split = lambda ref: ref[...].reshape(nb, c, n)
    r, lw, k, v, a, b = (split(ref) for ref in (r_ref, lw_ref, k_ref, v_ref, a_ref, b_ref))

    row = lax.broadcasted_iota(jnp.int32, (nb, c, c), 1)
    col = lax.broadcasted_iota(jnp.int32, (nb, c, c), 2)
    lower = row >= col
    strict = row > col
    lw_hi = lw.astype(BF16).astype(F32)
    lw_mid = (lw - lw_hi).astype(BF16).astype(F32)
    cum = _bmm(lower, lw_hi) + _bmm(lower, lw_mid) + _bmm(lower, lw - lw_hi - lw_mid)
    tot = cum[:, c - 1:c, :]
    g_in = jnp.exp(cum)
    g_ex = jnp.exp(cum - lw)
    g_inv = jnp.exp(-cum)
    g_rem = jnp.exp(tot - cum)
    a_t = a * g_ex
    r_t = r * g_in
    b_t = b * g_inv
    k_t = k * g_inv
    l_ab = jnp.where(strict, _bmm_nt(a_t, b_t), 0.0)
    l_ak = jnp.where(strict, _bmm_nt(a_t, k_t), 0.0)
    m_rb = jnp.where(lower, _bmm_nt(r_t, b_t), 0.0)
    m_rk = jnp.where(lower, _bmm_nt(r_t, k_t), 0.0)
    same16 = (row // 16) == (col // 16)
    same32 = (row // 32) == (col // 32)
    diag16 = jnp.where(same16, l_ab, 0.0)
    inv = jnp.where(row == col, 1.0, 0.0) + diag16
    lp = diag16
    for _ in range(3):
        lp = _bmm(lp, lp)
        inv = inv + _bmm(inv, lp)
    off32 = jnp.where(jnp.logical_and(same32, jnp.logical_not(same16)), l_ab, 0.0)
    inv = inv + _bmm(_bmm(inv, off32), inv)
    off64 = jnp.where(same32, 0.0, l_ab)
    inv = inv + _bmm(_bmm(inv, off64), inv)
    a_hat = _bmm(inv, a_t)
    w_hat = _bmm(inv, _bmm(l_ak, v))
    r_hat = r_t + _bmm(m_rb, a_hat)
    y0 = _bmm(m_rb, w_hat) + _bmm(m_rk, v)
    b_hat = b * g_rem
    k_hat = k * g_rem
    decay = jnp.exp(tot)

    chunk = lambda x, j: x.reshape(nh, nc, x.shape[1], n)[:, j]
    s = s_ref[...]
    ys = []
    for j in range(nc):
        u = _bmm_nt(chunk(a_hat, j), s) + chunk(w_hat, j)
        ys.append(_bmm_nt(chunk(r_hat, j), s) + chunk(y0, j))
        s = s * chunk(decay, j) + _bmm_tn(u, chunk(b_hat, j)) + _bmm_tn(chunk(v, j), chunk(k_hat, j))
    s_ref[...] = s
    y = jnp.concatenate(ys, axis=1)

    r = r_ref[...]
    mean = jnp.mean(y, axis=-1, keepdims=True)
    yc = y - mean
    var = jnp.mean(yc * yc, axis=-1, keepdims=True)
    yn = yc * lax.rsqrt(var + GN_EPS) * lnw_ref[...] + lnb_ref[...]
    bonus = jnp.sum(r * k_ref[...] * rk_ref[...], axis=-1, keepdims=True) * v_ref[...]
    out = (yn + bonus) * g_ref[...]
    for hd in range(nh):
        y_ref[:, hd * n:(hd + 1) * n] = out[hd]


def _rwkvscan(r, lw, k, v, a, b, g, ln_w, ln_b, r_k):
    nh, t, n = r.shape
    rows = SCAN_CHUNK * SCAN_CHUNKS
    blk = pl.BlockSpec((nh, rows, n), lambda i: (0, i, 0))
    par = pl.BlockSpec((nh, 1, n), lambda i: (0, 0, 0))
    return pl.pallas_call(
        _rwkvscan_kernel,
        grid=(t // rows,),
        in_specs=[blk] * 7 + [par] * 3,
        out_specs=pl.BlockSpec((rows, nh * n), lambda i: (i, 0)),
        out_shape=jax.ShapeDtypeStruct((t, nh * n), F32),
        scratch_shapes=[pltpu.VMEM((nh, n, n), F32)],
        compiler_params=_cparams("arbitrary"),
        name="rwkvscan",
    )(r, lw, k, v, a, b, g, ln_w, ln_b, r_k)


def _mobasel_kernel(q_ref, km_ref, o_ref):
    j = pl.program_id(1)
    q = q_ref[0]
    km = km_ref[0]
    nb = km.shape[0]
    tq = q.shape[0]
    gate = lax.dot_general(km, q, (((1,), (1,)), ((), ())), precision=HI, preferred_element_type=F32)
    n_id = lax.broadcasted_iota(jnp.int32, (nb, tq), 0)
    t_id = lax.broadcasted_iota(jnp.int32, (nb, tq), 1) + j * tq
    q_blk = t_id // MOBA_BLOCK
    cand = n_id < nb - 1
    past = n_id < q_blk
    neg_inf = jnp.float32(-jnp.inf)
    gate = jnp.where(cand, jnp.where(past, gate, GATE_NEG), neg_inf)
    sel = jnp.zeros((nb, tq), F32)
    for _ in range(min(MOBA_TOP_K, nb - 1)):
        m = jnp.max(gate, axis=0, keepdims=True)
        idx = jnp.min(jnp.where(gate == m, n_id, nb), axis=0, keepdims=True)
        pick = n_id == idx
        sel = jnp.where(pick, 1.0, sel)
        gate = jnp.where(pick, neg_inf, gate)
    bias = jnp.where(jnp.logical_or(jnp.logical_and(past, sel > 0.5), n_id == q_blk), 0.0, MASK_NEG)
    pad = jnp.zeros((BIAS_ROWS - KV_GROUP, tq), F32)
    for g in range(nb // KV_GROUP):
        o_ref[0, g] = jnp.concatenate([bias[g * KV_GROUP:(g + 1) * KV_GROUP], pad], axis=0).astype(BF16)


def _mobasel(q_hm, kmean):
    nh, t, n = q_hm.shape
    nb = kmean.shape[1]
    ng = nb // KV_GROUP
    tq = min(t, 2048)
    return pl.pallas_call(
        _mobasel_kernel,
        grid=(nh, t // tq),
        in_specs=[pl.BlockSpec((1, tq, n), lambda h, j: (h, j, 0)),
                  pl.BlockSpec((1, nb, n), lambda h, j: (h, 0, 0))],
        out_specs=pl.BlockSpec((1, ng, BIAS_ROWS, tq), lambda h, j: (h, 0, 0, j)),
        out_shape=jax.ShapeDtypeStruct((nh, ng, BIAS_ROWS, t), BF16),
        compiler_params=_cparams("arbitrary", "arbitrary"),
        name="mobasel",
    )(q_hm, kmean)


def _mobaattn_kernel(qt_ref, k_ref, vt_ref, bias_ref, o_ref, sa_ref, sb_ref, m_ref, acc_ref):
    i = pl.program_id(1)
    bs = MOBA_BLOCK
    gd = i // KV_GROUP
    q_pad = jnp.zeros((K_AUG - HEAD_DIM - BIAS_ROWS, bs), BF16)
    heads = range(HEADS_PER_STEP)

    def scores_to(dst, g):
        for hh in heads:
            q_aug = jnp.concatenate([qt_ref[hh], bias_ref[hh, g], q_pad], axis=0)
            dst[hh] = jnp.dot(k_ref[hh, g], q_aug, preferred_element_type=F32)

    def update_from(src, g, keep=None):
        for hh in heads:
            s = src[hh] if keep is None else jnp.where(keep, src[hh], MASK_NEG)
            m = m_ref[hh]
            m_new = jnp.maximum(m, jnp.max(s, axis=0, keepdims=True))
            acc = jnp.exp2(m - m_new) * acc_ref[hh]
            for b in range(KV_GROUP):
                p = jnp.exp2((s[b * bs:(b + 1) * bs] - m_new).astype(BF16))
                acc = acc + jnp.dot(vt_ref[hh, g, :, pl.ds(b * bs, bs)], p, preferred_element_type=F32)
            acc_ref[hh] = acc
            m_ref[hh] = m_new

    def finish(src):
        causal = lax.broadcasted_iota(jnp.int32, (bs, bs), 0) <= lax.broadcasted_iota(jnp.int32, (bs, bs), 1)
        own = pl.ds(pl.multiple_of((i % KV_GROUP) * bs, bs), bs)
        for hh in heads:
            src[hh, own, :] = jnp.where(causal, src[hh, own, :], MASK_NEG)
        update_from(src, gd)
        for hh in heads:
            acc = acc_ref[hh]
            o_ref[hh] = acc[:HEAD_DIM] / acc[HEAD_DIM:HEAD_DIM + 1]

    m_ref[...] = jnp.full(m_ref.shape, MASK_NEG, F32)
    acc_ref[...] = jnp.zeros(acc_ref.shape, F32)
    scores_to(sa_ref, 0)

    def pair(k, _):
        g = 2 * k
        scores_to(sb_ref, g + 1)
        update_from(sa_ref, g)
        scores_to(sa_ref, g + 2)
        update_from(sb_ref, g + 1)
        return 0

    lax.fori_loop(0, gd // 2, pair, 0)

    @pl.when(gd % 2 == 1)
    def _():
        scores_to(sb_ref, gd)
        update_from(sa_ref, gd - 1)
        finish(sb_ref)

    @pl.when(gd % 2 == 0)
    def _():
        finish(sa_ref)


def _mobaattn(qt, k_aug, vt_aug, bias):
    nh, n, t = qt.shape
    gs = MOBA_BLOCK * KV_GROUP
    ng = t // gs
    hp = HEADS_PER_STEP
    return pl.pallas_call(
        _mobaattn_kernel,
        grid=(nh // hp, t // MOBA_BLOCK),
        in_specs=[pl.BlockSpec((hp, n, MOBA_BLOCK), lambda h, i: (h, 0, i)),
                  pl.BlockSpec((hp, ng, gs, K_AUG), lambda h, i: (h, 0, 0, 0)),
                  pl.BlockSpec((hp, ng, V_AUG, gs), lambda h, i: (h, 0, 0, 0)),
                  pl.BlockSpec((hp, ng, BIAS_ROWS, MOBA_BLOCK), lambda h, i: (h, 0, 0, i))],
        out_specs=pl.BlockSpec((hp, n, MOBA_BLOCK), lambda h, i: (h, 0, i)),
        out_shape=jax.ShapeDtypeStruct((nh, n, t), F32),
        scratch_shapes=[pltpu.VMEM((hp, gs, MOBA_BLOCK), F32), pltpu.VMEM((hp, gs, MOBA_BLOCK), F32),
                        pltpu.VMEM((hp, 1, MOBA_BLOCK), F32), pltpu.VMEM((hp, V_AUG, MOBA_BLOCK), F32)],
        compiler_params=_cparams("arbitrary", "arbitrary"),
        name="mobaattn",
    )(qt, k_aug, vt_aug, bias)


def _load_token_tiles(ref, n_tokens, lead=(), first=0):
    chunks = [ref[lead + (pl.ds(first * TOKEN_ROWS + ch, n_tokens, stride=TOKEN_ROWS), slice(None))]
              for ch in range(TOKEN_ROWS)]
    return jnp.concatenate(chunks, axis=1)


def _store_token_tiles(ref, val, first=0):
    for ch in range(TOKEN_ROWS):
        ref[pl.ds(first * TOKEN_ROWS + ch, val.shape[0], stride=TOKEN_ROWS), :] = val[:, ch * LANES:(ch + 1) * LANES]


def _mix_kernel(ya_ref, yb_ref, sg_ref, x_ref, wa_ref, wb_ref, wo_ref, g1_ref, n2_ref, sc_ref, sh_ref,
                rw_ref, rb_ref, x1_ref, h2_ref, eid_ref, rank_ref, wt_ref, cnt_ref, base_ref):
    i = pl.program_id(0)

    @pl.when(i == 0)
    def _():
        base_ref[...] = jnp.zeros_like(base_ref)

    d = x_ref.shape[1]
    tm = x_ref.shape[0]
    halves = []
    half = tm // 2
    for first in (0, half):
        rows = slice(first, first + half)
        pa = jnp.dot(ya_ref[rows, :].astype(BF16), wa_ref[...], preferred_element_type=F32)
        yb = yb_ref[:, :, rows].reshape(WIDTH, half).T
        pb = jnp.dot(yb.astype(BF16), wb_ref[...], preferred_element_type=F32)
        sg = sg_ref[rows, :]
        mixed = sg[:, :d] * pa + sg[:, d:] * pb
        mixed = jnp.dot(mixed.astype(BF16), wo_ref[...], preferred_element_type=F32)
        x1 = x_ref[rows, :] + g1_ref[...] * mixed
        x1_ref[rows, :] = x1
        ms = jnp.mean(x1 * x1, axis=-1, keepdims=True)
        h2 = x1 * lax.rsqrt(ms + RMS_EPS) * n2_ref[...]
        halves.append(h2 * (1.0 + sc_ref[...]) + sh_ref[...])
    h2 = jnp.concatenate(halves, axis=0)
    _store_token_tiles(h2_ref, h2)

    ne = rw_ref.shape[1] // 2
    h2_hi = h2.astype(BF16)
    h2_lo = (h2 - h2_hi.astype(F32)).astype(BF16)
    by_hi = jnp.dot(h2_hi, rw_ref[...], preferred_element_type=F32)
    by_lo = jnp.dot(h2_lo, rw_ref[:, :ne], preferred_element_type=F32)
    logits = by_hi[:, :ne] + by_hi[:, ne:] + by_lo + rb_ref[...]
    e_id = lax.broadcasted_iota(jnp.int32, (tm, ne), 1)
    neg_inf = jnp.float32(-jnp.inf)
    work = logits
    picks, vals = [], []
    for _ in range(EXPERT_TOP_K):
        m = jnp.max(work, axis=-1, keepdims=True)
        idx = jnp.min(jnp.where(work == m, e_id, ne), axis=-1, keepdims=True)
        pick = e_id == idx
        picks.append((idx, pick))
        vals.append(m)
        work = jnp.where(pick, neg_inf, work)
    den = sum(jnp.exp(v - vals[0]) for v in vals)
    chosen = jnp.zeros((tm, ne), F32)
    for _, pick in picks:
        chosen = jnp.where(pick, 1.0, chosen)
    r_id = lax.broadcasted_iota(jnp.int32, (tm, tm), 0)
    c_id = lax.broadcasted_iota(jnp.int32, (tm, tm), 1)
    before = jnp.where(r_id > c_id, 1.0, 0.0).astype(BF16)
    ahead = jnp.dot(before, chosen.astype(BF16), preferred_element_type=F32) + base_ref[0:1, :]
    k_id = lax.broadcasted_iota(jnp.int32, (tm, EXPERT_TOP_K), 1)
    eid = jnp.zeros((tm, EXPERT_TOP_K), jnp.int32)
    rank = jnp.zeros((tm, EXPERT_TOP_K), jnp.int32)
    wt = jnp.zeros((tm, EXPERT_TOP_K), F32)
    for kk, ((idx, pick), v) in enumerate(zip(picks, vals)):
        rk = jnp.sum(jnp.where(pick, ahead, 0.0), axis=-1, keepdims=True).astype(jnp.int32)
        eid = jnp.where(k_id == kk, idx, eid)
        rank = jnp.where(k_id == kk, rk, rank)
        wt = jnp.where(k_id == kk, jnp.exp(v - vals[0]) / den, wt)
    eid_ref[...] = eid
    rank_ref[...] = rank
    wt_ref[...] = wt
    total = base_ref[0:1, :] + jnp.sum(chosen, axis=0, keepdims=True)
    base_ref[0:1, :] = total
    cnt_ref[...] = jnp.broadcast_to(total, cnt_ref.shape)


def _mix(ya, yb, sg, x2, w_a, w_b, w_o, gate1, norm2_g, scale2, shift2, router_w, router_b):
    t, d = x2.shape
    tm = 512
    ne = router_w.shape[1]
    rw_hi = router_w.astype(BF16)
    router_w = jnp.concatenate([rw_hi, (router_w - rw_hi.astype(F32)).astype(BF16)], axis=1)
    full = lambda a: pl.BlockSpec(a.shape, lambda i: (0, 0))
    row = lambda n: pl.BlockSpec((tm, n), lambda i: (i, 0))
    return pl.pallas_call(
        _mix_kernel,
        grid=(t // tm,),
        in_specs=[row(WIDTH), pl.BlockSpec((N_HEADS, HEAD_DIM, tm), lambda i: (0, 0, i)), row(2 * d), row(d),
                  full(w_a), full(w_b), full(w_o),
                  full(gate1), full(norm2_g), full(scale2), full(shift2), full(router_w), full(router_b)],
        out_specs=[row(d), pl.BlockSpec((tm * TOKEN_ROWS, LANES), lambda i: (i, 0)),
                   row(EXPERT_TOP_K), row(EXPERT_TOP_K), row(EXPERT_TOP_K),
                   pl.BlockSpec((8, ne), lambda i: (0, 0))],
        out_shape=[jax.ShapeDtypeStruct((t, d), F32), jax.ShapeDtypeStruct((t * TOKEN_ROWS, LANES), F32),
                   jax.ShapeDtypeStruct((t, EXPERT_TOP_K), jnp.int32),
                   jax.ShapeDtypeStruct((t, EXPERT_TOP_K), jnp.int32),
                   jax.ShapeDtypeStruct((t, EXPERT_TOP_K), F32),
                   jax.ShapeDtypeStruct((8, ne), F32)],
        scratch_shapes=[pltpu.VMEM((8, ne), F32)],
        compiler_params=_cparams("arbitrary"),
        name="mix",
    )(ya, yb, sg, x2, w_a, w_b, w_o, gate1, norm2_g, scale2, shift2, router_w, router_b)


DISPATCH_TILE = 512


def _slots_kernel(off_ref, eid_ref, rank_ref, o_ref):
    eid = eid_ref[...]
    slot = rank_ref[...]
    for e in range(N_EXPERTS):
        slot = slot + jnp.where(eid == e, off_ref[e], 0)
    o_ref[...] = slot * TOKEN_ROWS


def _slots(off, eid, rank):
    n = eid.size
    lanes = 128
    shape = (n // lanes, lanes)
    full = pl.BlockSpec(shape, lambda i: (0, 0))
    out = pl.pallas_call(
        _slots_kernel,
        grid=(1,),
        in_specs=[pl.BlockSpec(memory_space=pltpu.SMEM), full, full],
        out_specs=full,
        out_shape=jax.ShapeDtypeStruct(shape, jnp.int32),
        compiler_params=_cparams("arbitrary"),
        name="slots",
    )(off, eid.reshape(shape), rank.reshape(shape))
    return out.reshape(n)


def _dispatch_kernel(slot_ref, last_ref, nu_ref, h_ref, xs_ref, zeros_ref, sem, zsem):
    tm = DISPATCH_TILE
    tile_rows = EXPERT_TILE * TOKEN_ROWS
    n_tiles = xs_ref.shape[0] // tile_rows

    @pl.when(pl.program_id(0) == 0)
    def _():
        zeros_ref[...] = jnp.zeros_like(zeros_ref)
        zero_tile = lambda j: pltpu.make_async_copy(
            zeros_ref, xs_ref.at[pl.ds(pl.multiple_of(j * tile_rows, tile_rows), tile_rows)], zsem)

        def each_padded_tile(fn):
            def expert(e, c):
                @pl.when(last_ref[e] >= 0)
                def _():
                    fn(last_ref[e])
                return c
            lax.fori_loop(0, N_EXPERTS, expert, 0)
            lax.fori_loop(nu_ref[0], n_tiles, lambda j, c: (fn(j), c)[1], 0)

        each_padded_tile(lambda j: zero_tile(j).start())
        each_padded_tile(lambda j: zero_tile(j).wait())

    def body(t, _):
        for kk in range(EXPERT_TOP_K):
            row = pl.multiple_of(slot_ref[t * EXPERT_TOP_K + kk], TOKEN_ROWS)
            pltpu.make_async_copy(h_ref.at[pl.ds(t * TOKEN_ROWS, TOKEN_ROWS)], xs_ref.at[pl.ds(row, TOKEN_ROWS)],
                                  sem).start(priority=kk % 2)
        return 0

    lax.fori_loop(0, tm, body, 0, unroll=2)
    for _ in range(EXPERT_TOP_K):
        pltpu.make_async_copy(h_ref, xs_ref.at[pl.ds(0, tm * TOKEN_ROWS)], sem).wait()


def _dispatch(slot_flat, last_tile, n_used, h2, n_slots):
    rows, lanes = h2.shape
    tm = DISPATCH_TILE
    t = rows // TOKEN_ROWS
    smem = pl.BlockSpec(memory_space=pltpu.SMEM)
    return pl.pallas_call(
        _dispatch_kernel,
        grid=(t // tm,),
        in_specs=[pl.BlockSpec((tm * EXPERT_TOP_K,), lambda i: (i,), memory_space=pltpu.SMEM), smem, smem,
                  pl.BlockSpec((tm * TOKEN_ROWS, lanes), lambda i: (i, 0))],
        out_specs=pl.BlockSpec(memory_space=pl.ANY),
        out_shape=jax.ShapeDtypeStruct((n_slots * TOKEN_ROWS, lanes), F32),
        scratch_shapes=[pltpu.VMEM((EXPERT_TILE * TOKEN_ROWS, lanes), F32),
                        pltpu.SemaphoreType.DMA(()), pltpu.SemaphoreType.DMA(())],
        compiler_params=pltpu.CompilerParams(dimension_semantics=("arbitrary",), has_side_effects=True),
        name="dispatch",
    )(slot_flat, last_tile, n_used, h2)


def _experts_kernel(te_ref, nu_ref, nx_ref, rp_ref, x_ref, wg_hbm, bg_ref, wu_hbm, bu_ref, wd_hbm, bd_ref, y_ref,
                    wbuf, wg_s, wu_s, wd_s, sems):
    i = pl.program_id(0)
    used = i < nu_ref[0]
    new_expert = jnp.logical_or(i == 0, te_ref[i] != te_ref[jnp.maximum(i - 1, 0)])

    def fetch(e, slot):
        return [pltpu.make_async_copy(w.at[e], wbuf.at[slot, j], sems.at[slot, j])
                for j, w in enumerate((wg_hbm, wu_hbm, wd_hbm))]

    @pl.when(jnp.logical_and(used, new_expert))
    def _():
        slot = rp_ref[i]

        @pl.when(i == 0)
        def _():
            for copy in fetch(te_ref[0], 0):
                copy.start()

        for copy in fetch(te_ref[i], slot):
            copy.wait()
        wg_s[...] = wbuf[slot, 0].astype(BF16)
        wu_s[...] = wbuf[slot, 1].astype(BF16)
        wd_s[...] = wbuf[slot, 2].astype(BF16)

        @pl.when(nx_ref[i] >= 0)
        def _():
            for copy in fetch(nx_ref[i], 1 - slot):
                copy.start()

    @pl.when(used)
    def _():
        x = _load_token_tiles(x_ref, EXPERT_TILE).astype(BF16)
        gt = jnp.minimum(jnp.dot(x, wg_s[...], preferred_element_type=F32) + bg_ref[0], SWIGLU_LIMIT)
        up = jnp.clip(jnp.dot(x, wu_s[...], preferred_element_type=F32) + bu_ref[0], -SWIGLU_LIMIT, SWIGLU_LIMIT)
        hid = (up + 1.0) * gt * _sigmoid(SWIGLU_ALPHA * gt)
        _store_token_tiles(y_ref, jnp.dot(hid.astype(BF16), wd_s[...], preferred_element_type=F32) + bd_ref[0])

    @pl.when(jnp.logical_not(used))
    def _():
        y_ref[...] = jnp.zeros_like(y_ref)


def _experts(tile_expert, n_used, next_expert, run_parity, xs, w_gate, b_gate, w_up, b_up, w_down, b_down):
    rows, lanes = xs.shape
    d = TOKEN_ROWS * lanes
    f = w_gate.shape[2]
    assert d == f
    tm = EXPERT_TILE
    n_tiles = rows // (tm * TOKEN_ROWS)
    row = pl.BlockSpec((tm * TOKEN_ROWS, lanes), lambda i, te, nu, nx, rp: (i, 0))
    bias = lambda n: pl.BlockSpec((1, 1, n), lambda i, te, nu, nx, rp: (te[i], 0, 0))
    hbm = pl.BlockSpec(memory_space=pl.ANY)
    grid_spec = pltpu.PrefetchScalarGridSpec(
        num_scalar_prefetch=4,
        grid=(n_tiles,),
        in_specs=[row, hbm, bias(f), hbm, bias(f), hbm, bias(d)],
        out_specs=row,
        scratch_shapes=[pltpu.VMEM((2, 3, d, f), F32),
                        pltpu.VMEM((d, f), BF16), pltpu.VMEM((d, f), BF16), pltpu.VMEM((f, d), BF16),
                        pltpu.SemaphoreType.DMA((2, 3))],
    )
    return pl.pallas_call(
        _experts_kernel,
        grid_spec=grid_spec,
        out_shape=jax.ShapeDtypeStruct((n_tiles * tm * TOKEN_ROWS, lanes), F32),
        compiler_params=_cparams("arbitrary"),
        name="experts",
    )(tile_expert, n_used, next_expert, run_parity, xs, w_gate, b_gate, w_up, b_up, w_down, b_down)


def _combine_kernel(slot_ref, next_ref, ys_ref, wt_ref, x1_ref, g2_ref, nf_ref, o_ref, buf, sems):
    i = pl.program_id(0)
    n = pl.num_programs(0)
    tm = DISPATCH_TILE

    def issue(slots, p):
        def body(t, _):
            for kk in range(EXPERT_TOP_K):
                row = pl.multiple_of(slots[t * EXPERT_TOP_K + kk], TOKEN_ROWS)
                pltpu.make_async_copy(ys_ref.at[pl.ds(row, TOKEN_ROWS)],
                                      buf.at[p, kk, pl.ds(t * TOKEN_ROWS, TOKEN_ROWS)],
                                      sems.at[p]).start(priority=kk % 2)
            return 0
        lax.fori_loop(0, tm, body, 0, unroll=2)

    p = i % 2

    @pl.when(i == 0)
    def _():
        issue(slot_ref, 0)

    @pl.when(i + 1 < n)
    def _():
        issue(next_ref, 1 - p)

    for kk in range(EXPERT_TOP_K):
        pltpu.make_async_copy(ys_ref.at[pl.ds(0, tm * TOKEN_ROWS)], buf.at[p, kk], sems.at[p]).wait()
    wt = wt_ref[...]
    moe = wt[:, 0:1] * _load_token_tiles(buf, tm, (p, 0))
    for kk in range(1, EXPERT_TOP_K):
        moe = moe + wt[:, kk:kk + 1] * _load_token_tiles(buf, tm, (p, kk))
    x = x1_ref[...] + g2_ref[...] * moe
    ms = jnp.mean(x * x, axis=-1, keepdims=True)
    o_ref[...] = x * lax.rsqrt(ms + RMS_EPS) * nf_ref[...]


def _combine(slot_flat, ys, wt, x1, gate2, normf_g):
    t, d = x1.shape
    tm = DISPATCH_TILE
    n = t // tm
    full = lambda a: pl.BlockSpec(a.shape, lambda i: (0, 0))
    row = lambda w: pl.BlockSpec((tm, w), lambda i: (i, 0))
    slots = lambda index: pl.BlockSpec((tm * EXPERT_TOP_K,), index, memory_space=pltpu.SMEM)
    return pl.pallas_call(
        _combine_kernel,
        grid=(n,),
        in_specs=[slots(lambda i: (i,)), slots(lambda i: (jnp.minimum(i + 1, n - 1),)),
                  pl.BlockSpec(memory_space=pl.ANY),
                  row(EXPERT_TOP_K), row(d), full(gate2), full(normf_g)],
        out_specs=row(d),
        out_shape=jax.ShapeDtypeStruct((t, d), F32),
        scratch_shapes=[pltpu.VMEM((2, EXPERT_TOP_K, tm * TOKEN_ROWS, LANES), F32), pltpu.SemaphoreType.DMA((2,))],
        compiler_params=_cparams("arbitrary"),
        name="combine",
    )(slot_flat, slot_flat, ys, wt, x1, gate2, normf_g)


def kernel(x, c, w_ada, b_ada, norm1_g, w_in, rwkv_mu, rwkv_w0, rwkv_w2, rwkv_a0, rwkv_a2, rwkv_g2, rwkv_k_k, rwkv_k_a, rwkv_r_k, rwkv_ln_w, rwkv_ln_b, w_branch_a, w_branch_b, w_out, norm2_g, router_w, router_b, exp_w_gate, exp_b_gate, exp_w_up, exp_b_up, exp_w_down, exp_b_down, normf_g):
    bsz, t, d = x.shape
    assert bsz == 1 and t % (MOBA_BLOCK * KV_GROUP) == 0
    x2 = x.reshape(t, d)
    row = lambda a: a.reshape(1, -1)

    mod = _mod(c, w_ada, b_ada)
    shift1, scale1, gate1, shift2, scale2, gate2 = [mod[:, j * d:(j + 1) * d] for j in range(6)]

    qkv_end = RWKV_PROJ + 3 * WIDTH
    w_in_b = w_in.astype(BF16)
    zeros_lora = jnp.zeros((DECAY_LORA, WIDTH), F32)
    w2p = jnp.concatenate([rwkv_w2, zeros_lora], axis=0)
    a2p = jnp.concatenate([zeros_lora, rwkv_a2], axis=0)
    head_of = jnp.arange(WIDTH) // HEAD_DIM
    bd = (head_of[:, None] == head_of[None, :]).astype(BF16)
    rwkv_params = (row(rwkv_mu), row(rwkv_w0), w2p.astype(BF16), row(rwkv_a0), a2p.astype(BF16),
                   rwkv_g2.astype(BF16), row(rwkv_k_k), row(rwkv_k_a), bd)
    sg, kmean, q_hm, qt, k_aug, vt_aug, r, lw, k, v, av, bv, g = _inproj(
        x2, row(norm1_g), scale1, shift1, w_in_b[:, :RWKV_PROJ], w_in_b[:, RWKV_PROJ:qkv_end], w_in_b[:, qkv_end:],
        rwkv_params)

    per_head = lambda a: a.reshape(N_HEADS, 1, HEAD_DIM)
    y_a = _rwkvscan(r, lw, k, v, av, bv, g, per_head(rwkv_ln_w), per_head(rwkv_ln_b), per_head(rwkv_r_k))

    nb = t // MOBA_BLOCK
    bias = _mobasel(q_hm, kmean.reshape(nb, N_HEADS, HEAD_DIM).transpose(1, 0, 2))
    y_b = _mobaattn(qt, k_aug, vt_aug, bias)

    x1, h2, eid, rank, wt, cnt = _mix(y_a, y_b, sg, x2, w_branch_a.astype(BF16), w_branch_b.astype(BF16),
                                      w_out.astype(BF16), gate1, row(norm2_g), scale2, shift2,
                                      router_w, row(router_b))

    counts = cnt[0].astype(jnp.int32)
    tiles_per = (counts + EXPERT_TILE - 1) // EXPERT_TILE
    tile_end = jnp.cumsum(tiles_per)
    off = (tile_end - tiles_per) * EXPERT_TILE
    n_tiles = (t * EXPERT_TOP_K + N_EXPERTS * (EXPERT_TILE - 1)) // EXPERT_TILE
    n_used = tile_end[-1:]
    tile_expert = jnp.minimum(jnp.sum(tile_end[None, :] <= jnp.arange(n_tiles)[:, None], axis=1),
                              N_EXPERTS - 1).astype(jnp.int32)
    last_used = tile_expert[jnp.maximum(n_used[0] - 1, 0)]
    tile_expert = jnp.where(jnp.arange(n_tiles) < n_used[0], tile_expert, last_used)

    slot_flat = _slots(off, eid, rank)
    last_tile = jnp.where(tiles_per > 0, tile_end - 1, -1).astype(jnp.int32)
    xs = _dispatch(slot_flat, last_tile, n_used, h2, n_tiles * EXPERT_TILE)
    f = exp_w_gate.shape[2]
    nonempty = tiles_per > 0
    run_index = jnp.cumsum(nonempty) - 1
    ids = jnp.arange(N_EXPERTS)
    later = jnp.logical_and(nonempty[None, :], ids[None, :] > ids[:, None])
    next_nonempty = jnp.min(jnp.where(later, ids[None, :], N_EXPERTS), axis=1)
    next_nonempty = jnp.where(next_nonempty < N_EXPERTS, next_nonempty, -1)
    next_expert = next_nonempty[tile_expert].astype(jnp.int32)
    run_parity = (run_index[tile_expert] % 2).astype(jnp.int32)
    ys = _experts(tile_expert, n_used, next_expert, run_parity, xs,
                  exp_w_gate, exp_b_gate.reshape(N_EXPERTS, 1, f),
                  exp_w_up, exp_b_up.reshape(N_EXPERTS, 1, f),
                  exp_w_down, exp_b_down.reshape(N_EXPERTS, 1, d))
    out = _combine(slot_flat, ys, wt, x1, gate2, row(normf_g))
    return out.reshape(bsz, t, d)
```
